```python
import math
import jax, jax.numpy as jnp
from jax import lax
import numpy as np

D_MODEL = 1024
BATCH = 16
SEQ = 4096
DEPTH = 1
DEC_BATCH = 32
DEC_SEQ = 32
PAST_LEN = 1024

CHUNK = 64
MIX_WIDTH = D_MODEL
GDN_WIDTH = MIX_WIDTH // 2
CONV_WIDTH = MIX_WIDTH - GDN_WIDTH
GDN_HEAD_DIM = 128
GDN_HEADS = GDN_WIDTH // GDN_HEAD_DIM
QKV_WIDTH = 3 * GDN_WIDTH
GDN_CONV = 4
DW_CONV = 31
N_EXPERTS = 256
TOP_K = 8
N_GROUPS = 8
TOPK_GROUPS = 4
D_EXPERT = 256
D_SHARED = 256
ROUTED_SCALE = 2.5
MOE_BLOCK = 64
LN_EPS = 1e-5
RMS_EPS = 1e-6
L2_EPS = 1e-6
ALPHA = (2.0 * DEPTH) ** 0.25
BETA_INIT = (8.0 * DEPTH) ** -0.25
SPLIT_POINTS = [QKV_WIDTH, QKV_WIDTH + GDN_WIDTH, QKV_WIDTH + GDN_WIDTH + GDN_HEADS,
                QKV_WIDTH + GDN_WIDTH + 2 * GDN_HEADS,
                QKV_WIDTH + GDN_WIDTH + 2 * GDN_HEADS + CONV_WIDTH]
D_PROJ = QKV_WIDTH + GDN_WIDTH + 2 * GDN_HEADS + 2 * CONV_WIDTH

kernel_name = "hybrid_gdn_conformer_moe_stream_step"


def layer_norm(x, g=None, b=None):
    xf = x.astype(jnp.float32)
    mu = jnp.mean(xf, axis=-1, keepdims=True)
    var = jnp.mean(jnp.square(xf - mu), axis=-1, keepdims=True)
    y = (xf - mu) * lax.rsqrt(var + LN_EPS)
    if g is not None:
        y = y * g.astype(jnp.float32) + b.astype(jnp.float32)
    return y.astype(x.dtype)


def causal_dwconv(x, buf, w, b=None):
    width = w.shape[0]
    xx = jnp.concatenate([buf.astype(x.dtype), x], axis=1)
    y = lax.conv_general_dilated(xx, w[:, None, :].astype(x.dtype), window_strides=(1,),
                                 padding='VALID', dimension_numbers=('NWC', 'WIO', 'NWC'),
                                 feature_group_count=x.shape[-1])
    if b is not None:
        y = y + b.astype(x.dtype)
    return y, xx[:, xx.shape[1] - (width - 1):]


def l2norm(t):
    return t * lax.rsqrt(jnp.sum(t * t, axis=-1, keepdims=True) + L2_EPS)


def gated_delta_rule(q, k, v, g, beta, s0):
    B, L, H, DK = q.shape
    DV = v.shape[-1]
    C = min(CHUNK, L)
    N = L // C

    def blocks(t):
        t = t.reshape((B, N, C, H) + t.shape[3:])
        return jnp.moveaxis(jnp.moveaxis(t, 1, 0), 3, 2)

    qc, kc, vc, bc = blocks(q), blocks(k), blocks(v), blocks(beta)
    gc = jnp.cumsum(blocks(g), axis=-1)
    idx = jnp.arange(C)
    causal = idx[:, None] >= idx[None, :]
    strict = idx[:, None] > idx[None, :]
    decay = jnp.exp(jnp.where(causal, gc[..., :, None] - gc[..., None, :], -jnp.inf))
    kb = kc * bc[..., None]
    m = jnp.where(strict, jnp.einsum('nbhid,nbhjd->nbhij', kb, kc) * decay, 0.0)
    eye = jnp.eye(C, dtype=jnp.float32)
    t_inv = lax.linalg.triangular_solve(eye + m, jnp.broadcast_to(eye, m.shape), left_side=True,
                                        lower=True, unit_diagonal=True)
    u = jnp.einsum('nbhij,nbhje->nbhie', t_inv, vc * bc[..., None])
    w = jnp.einsum('nbhij,nbhjd->nbhid', t_inv, kb * jnp.exp(gc)[..., None])
    a_intra = jnp.where(causal, jnp.einsum('nbhid,nbhjd->nbhij', qc, kc) * decay, 0.0)
    q_dec = qc * jnp.exp(gc)[..., None]
    k_dec = kc * jnp.exp(gc[..., -1:] - gc)[..., None]
    g_last = jnp.exp(gc[..., -1])

    def step(s, inp):
        q_n, k_n, u_n, w_n, a_n, gl_n = inp
        v_new = u_n - jnp.einsum('bhcd,bhde->bhce', w_n, s)
        o_n = jnp.einsum('bhcd,bhde->bhce', q_n, s) + jnp.einsum('bhij,bhje->bhie', a_n, v_new)
        s = s * gl_n[..., None, None] + jnp.einsum('bhcd,bhce->bhde', k_n, v_new)
        return s, o_n

    s_fin, o = lax.scan(step, s0.astype(jnp.float32), (q_dec, k_dec, u, w, a_intra, g_last))
    o = jnp.moveaxis(jnp.moveaxis(o, 2, 3), 0, 1).reshape(B, L, H, DV)
    return o, s_fin


def swiglu(h, w_gate, w_up, w_down):
    return jnp.einsum('tf,fd->td', jax.nn.silu(h @ w_gate) * (h @ w_up), w_down)


def routed_moe(h, w_router, router_bias, we_gate, we_up, we_down):
    T, D = h.shape
    scores = jax.nn.sigmoid(jnp.einsum('td,de->te', h.astype(jnp.float32), w_router.astype(jnp.float32)))
    sel = scores + router_bias.astype(jnp.float32)
    grp_score = jnp.sum(lax.top_k(sel.reshape(T, N_GROUPS, N_EXPERTS // N_GROUPS), 2)[0], axis=-1)
    _, grp_idx = lax.top_k(grp_score, TOPK_GROUPS)
    grp_mask = jnp.any(grp_idx[..., None] == jnp.arange(N_GROUPS), axis=-2)
    sel = jnp.where(jnp.repeat(grp_mask, N_EXPERTS // N_GROUPS, axis=-1), sel, -jnp.inf)
    _, top_idx = lax.top_k(sel, TOP_K)
    top_w = jnp.take_along_axis(scores, top_idx, axis=-1)
    top_w = top_w / jnp.sum(top_w, axis=-1, keepdims=True) * ROUTED_SCALE

    A = T * TOP_K
    e_flat = top_idx.reshape(A).astype(jnp.int32)
    tok_flat = jnp.repeat(jnp.arange(T, dtype=jnp.int32), TOP_K)
    w_flat = top_w.reshape(A)
    e_s, tok_s, w_s = lax.sort((e_flat, tok_flat, w_flat), num_keys=1, is_stable=True)
    counts = jnp.bincount(e_flat, length=N_EXPERTS)
    starts = jnp.cumsum(counts) - counts
    padded = (counts + MOE_BLOCK - 1) // MOE_BLOCK * MOE_BLOCK
    pends = jnp.cumsum(padded)
    pstarts = pends - padded
    dest = pstarts[e_s] + (jnp.arange(A) - starts[e_s])
    n_blocks = -(-(A + min(N_EXPERTS, A) * (MOE_BLOCK - 1)) // MOE_BLOCK)
    P = n_blocks * MOE_BLOCK
    tok_buf = jnp.full((P,), T, jnp.int32).at[dest].set(tok_s)
    w_buf = jnp.zeros((P,), jnp.float32).at[dest].set(w_s)
    blk_exp = jnp.minimum(jnp.searchsorted(pends, jnp.arange(n_blocks) * MOE_BLOCK, side='right'),
                          N_EXPERTS - 1)
    h_pad = jnp.concatenate([h, jnp.zeros((1, D), h.dtype)], axis=0)

    def body(y, blk):
        tok, wt, e = blk
        rows = h_pad[tok]
        o = jnp.einsum('tf,fd->td', jax.nn.silu(rows @ we_gate[e]) * (rows @ we_up[e]), we_down[e])
        return y.at[tok].add(o.astype(jnp.float32) * wt[:, None]), None

    y, _ = lax.scan(body, jnp.zeros((T + 1, D), jnp.float32),
                    (tok_buf.reshape(n_blocks, MOE_BLOCK), w_buf.reshape(n_blocks, MOE_BLOCK), blk_exp))
    return y[:T].astype(h.dtype)


def layer(x, c, s_gdn, s_qkv, s_dw, w_ada, b_ada, w_in, conv_qkv_w, a_log, dt_bias, gdn_norm_w,
          dw_w, dw_b, cn_g, cn_b, w_out, ln1_g, ln1_b, w_router, router_bias, we_gate, we_up,
          we_down, ws_gate, ws_up, ws_down, ln2_g, ln2_b):
    B, L, D = x.shape
    mod = jnp.einsum('bd,de->be', jax.nn.silu(c), w_ada) + b_ada
    sh1, sc1, gt1, sh2, sc2, gt2 = jnp.split(mod[:, None, :].astype(x.dtype), 6, axis=-1)

    h = layer_norm(x) * (1 + sc1) + sh1
    proj = jnp.einsum('bld,dp->blp', h, w_in)
    qkv, z, b_raw, a_raw, glu_v, glu_g = jnp.split(proj, SPLIT_POINTS, axis=-1)

    qkv_c, new_qkv = causal_dwconv(qkv, s_qkv, conv_qkv_w)
    qkv_c = jax.nn.silu(qkv_c).astype(jnp.float32)
    q, k, v = jnp.split(qkv_c, 3, axis=-1)
    q = l2norm(q.reshape(B, L, GDN_HEADS, GDN_HEAD_DIM)) * (GDN_HEAD_DIM ** -0.5)
    k = l2norm(k.reshape(B, L, GDN_HEADS, GDN_HEAD_DIM))
    v = v.reshape(B, L, GDN_HEADS, GDN_HEAD_DIM)
    beta = jax.nn.sigmoid(b_raw.astype(jnp.float32))
    g = -jnp.exp(a_log.astype(jnp.float32)) * jax.nn.softplus(
        a_raw.astype(jnp.float32) + dt_bias.astype(jnp.float32))
    o, new_s = gated_delta_rule(q, k, v, g, beta, s_gdn)
    zf = z.astype(jnp.float32).reshape(B, L, GDN_HEADS, GDN_HEAD_DIM)
    o = (o * lax.rsqrt(jnp.mean(o * o, axis=-1, keepdims=True) + RMS_EPS)
         * gdn_norm_w.astype(jnp.float32) * jax.nn.silu(zf))
    o = o.reshape(B, L, GDN_WIDTH).astype(x.dtype)

    u = glu_v * jax.nn.sigmoid(glu_g)
    u_c, new_dw = causal_dwconv(u, s_dw, dw_w, dw_b)
    u_c = jax.nn.silu(layer_norm(u_c, cn_g, cn_b))

    mix = jnp.einsum('blc,cd->bld', jnp.concatenate([o, u_c], axis=-1), w_out)
    x = layer_norm(ALPHA * x + (1 + gt1) * mix, ln1_g, ln1_b)

    h = (layer_norm(x) * (1 + sc2) + sh2).reshape(B * L, D)
    ff = swiglu(h, ws_gate, ws_up, ws_down) + routed_moe(h, w_router, router_bias, we_gate, we_up, we_down)
    x = layer_norm(ALPHA * x + (1 + gt2) * ff.reshape(B, L, D), ln2_g, ln2_b)
    return x, new_s.astype(s_gdn.dtype), new_qkv.astype(s_qkv.dtype), new_dw.astype(s_dw.dtype)


def setup_inputs(seed: int = 0) -> dict:
    key = jax.random.key(seed)
    ks = jax.random.split(key, 32)
    f32 = jnp.float32

    def nrm(k, shape, scale):
        return jax.random.normal(k, shape, f32) * scale

    a_log = jnp.log(jax.random.uniform(ks[10], (DEPTH, GDN_HEADS), f32, 1.0, 16.0))
    dt = jnp.exp(jax.random.uniform(ks[11], (DEPTH, GDN_HEADS), f32, math.log(1e-3), math.log(1e-1)))
    dt_bias = dt + jnp.log(-jnp.expm1(-dt))
    return {
        "x_prompt": nrm(ks[0], (BATCH, SEQ, D_MODEL), 1.0),
        "x_sample": nrm(ks[1], (DEC_BATCH, DEC_SEQ, D_MODEL), 1.0),
        "state_gdn": nrm(ks[2], (DEPTH, DEC_BATCH, GDN_HEADS, GDN_HEAD_DIM, GDN_HEAD_DIM), 0.1),
        "state_qkv_conv": nrm(ks[3], (DEPTH, DEC_BATCH, GDN_CONV - 1, QKV_WIDTH), 1.0),
        "state_dw_conv": nrm(ks[4], (DEPTH, DEC_BATCH, DW_CONV - 1, CONV_WIDTH), 0.5),
        "c_prompt": nrm(ks[5], (BATCH, D_MODEL), 1.0),
        "c_sample": nrm(ks[6], (DEC_BATCH, D_MODEL), 1.0),
        "w_ada": nrm(ks[7], (DEPTH, D_MODEL, 6 * D_MODEL), 0.1 * D_MODEL ** -0.5),
        "b_ada": nrm(ks[8], (DEPTH, 6 * D_MODEL), 0.02),
        "w_in": nrm(ks[9], (DEPTH, D_MODEL, D_PROJ), D_MODEL ** -0.5),
        "conv_qkv_w": nrm(ks[12], (DEPTH, GDN_CONV, QKV_WIDTH), GDN_CONV ** -0.5),
        "a_log": a_log,
        "dt_bias": dt_bias,
        "gdn_norm_w": 1.0 + nrm(ks[13], (DEPTH, GDN_HEAD_DIM), 0.02),
        "dw_w": nrm(ks[14], (DEPTH, DW_CONV, CONV_WIDTH), DW_CONV ** -0.5),
        "dw_b": nrm(ks[15], (DEPTH, CONV_WIDTH), 0.02),
        "cn_g": 1.0 + nrm(ks[16], (DEPTH, CONV_WIDTH), 0.02),
        "cn_b": nrm(ks[17], (DEPTH, CONV_WIDTH), 0.02),
        "w_out": nrm(ks[18], (DEPTH, MIX_WIDTH, D_MODEL), BETA_INIT * MIX_WIDTH ** -0.5),
        "ln1_g": 1.0 + nrm(ks[19], (DEPTH, D_MODEL), 0.02),
        "ln1_b": nrm(ks[20], (DEPTH, D_MODEL), 0.02),
        "w_router": nrm(ks[21], (DEPTH, D_MODEL, N_EXPERTS), D_MODEL ** -0.5),
        "router_bias": nrm(ks[22], (DEPTH, N_EXPERTS), 0.01),
        "we_gate": nrm(ks[23], (DEPTH, N_EXPERTS, D_MODEL, D_EXPERT), D_MODEL ** -0.5),
        "we_up": nrm(ks[24], (DEPTH, N_EXPERTS, D_MODEL, D_EXPERT), D_MODEL ** -0.5),
        "we_down": nrm(ks[25], (DEPTH, N_EXPERTS, D_EXPERT, D_MODEL), BETA_INIT * D_EXPERT ** -0.5),
        "ws_gate": nrm(ks[26], (DEPTH, D_MODEL, D_SHARED), D_MODEL ** -0.5),
        "ws_up": nrm(ks[27], (DEPTH, D_MODEL, D_SHARED), D_MODEL ** -0.5),
        "ws_down": nrm(ks[28], (DEPTH, D_SHARED, D_MODEL), BETA_INIT * D_SHARED ** -0.5),
        "ln2_g": 1.0 + nrm(ks[29], (DEPTH, D_MODEL), 0.02),
        "ln2_b": nrm(ks[30], (DEPTH, D_MODEL), 0.02),
    }


def reference(x_prompt, x_sample, state_gdn, state_qkv_conv, state_dw_conv, c_prompt, c_sample,
              w_ada, b_ada, w_in, conv_qkv_w, a_log, dt_bias, gdn_norm_w, dw_w, dw_b, cn_g, cn_b,
              w_out, ln1_g, ln1_b, w_router, router_bias, we_gate, we_up, we_down, ws_gate, ws_up,
              ws_down, ln2_g, ln2_b):
    bp = x_prompt.shape[0]
    yp, ys = x_prompt, x_sample
    pg, pq, pd, sg, sq, sd = [], [], [], [], [], []
    for l in range(DEPTH):
        params = (w_ada[l], b_ada[l], w_in[l], conv_qkv_w[l], a_log[l], dt_bias[l], gdn_norm_w[l],
                  dw_w[l], dw_b[l], cn_g[l], cn_b[l], w_out[l], ln1_g[l], ln1_b[l], w_router[l],
                  router_bias[l], we_gate[l], we_up[l], we_down[l], ws_gate[l], ws_up[l], ws_down[l],
                  ln2_g[l], ln2_b[l])
        zg = jnp.zeros((bp, GDN_HEADS, GDN_HEAD_DIM, GDN_HEAD_DIM), state_gdn.dtype)
        zq = jnp.zeros((bp, GDN_CONV - 1, QKV_WIDTH), x_prompt.dtype)
        zd = jnp.zeros((bp, DW_CONV - 1, CONV_WIDTH), x_prompt.dtype)
        yp, g_p, q_p, d_p = layer(yp, c_prompt, zg, zq, zd, *params)
        ys, g_s, q_s, d_s = layer(ys, c_sample, state_gdn[l], state_qkv_conv[l], state_dw_conv[l], *params)
        pg.append(g_p); pq.append(q_p); pd.append(d_p)
        sg.append(g_s); sq.append(q_s); sd.append(d_s)
    return (yp, ys, jnp.stack(pg), jnp.stack(pq), jnp.stack(pd), jnp.stack(sg), jnp.stack(sq), jnp.stack(sd))
```

```python
import functools

import jax
import jax.numpy as jnp
from jax import lax
from jax.experimental import pallas as pl
from jax.experimental.pallas import tpu as pltpu

F32 = jnp.float32
BF16 = jnp.bfloat16
I32 = jnp.int32
U32 = jnp.uint32

D_MODEL = 1024
GDN_WIDTH = 512
CONV_WIDTH = 512
GDN_HEAD_DIM = 128
GDN_HEADS = 4
QKV_WIDTH = 3 * GDN_WIDTH
GDN_CONV = 4
DW_CONV = 31
CHUNK = 64
N_EXPERTS = 256
TOP_K = 8
N_GROUPS = 8
GROUP_SIZE = N_EXPERTS // N_GROUPS
TOPK_GROUPS = 4
D_EXPERT = 256
D_SHARED = 256
ROUTED_SCALE = 2.5
LN_EPS = 1e-5
RMS_EPS = 1e-6
L2_EPS = 1e-6
DEPTH = 1
ALPHA = (2.0 * DEPTH) ** 0.25

LANES = 128
SUBLANES = 8
VMEM_LIMIT_BYTES = 56 * 1024 * 1024

COL_Z = QKV_WIDTH
COL_GV = COL_Z + GDN_WIDTH
COL_GG = COL_GV + CONV_WIDTH
COL_BG = COL_GG + CONV_WIDTH
D_PROJ_PAD = COL_BG + LANES
QKV_TAIL = SUBLANES
DW_TAIL = 32
NEG_BIG = -1e30


def _cparams(sem):
    return pltpu.CompilerParams(dimension_semantics=sem, vmem_limit_bytes=VMEM_LIMIT_BYTES)


def _split_bf16(x):
    hi = x.astype(BF16)
    lo = (x - hi.astype(F32)).astype(BF16)
    return hi, lo


def _dot(a, b):
    return jnp.dot(a, b, preferred_element_type=F32)


def _dot_nt(a, b):
    return lax.dot_general(a, b, (((1,), (1,)), ((), ())), preferred_element_type=F32)


def _dot_tn(a, b):
    return lax.dot_general(a, b, (((0,), (0,)), ((), ())), preferred_element_type=F32)


def _dot_hp(a, b):
    ah, al = _split_bf16(a)
    bh, bl = _split_bf16(b)
    return _dot(ah, bh) + (_dot(ah, bl) + _dot(al, bh))


def _sigmoid(x):
    return 1.0 / (1.0 + jnp.exp(-x))


def _silu(x):
    return x * _sigmoid(x)


def _softplus(x):
    return jnp.maximum(x, 0.0) + jnp.log(1.0 + jnp.exp(-jnp.abs(x)))


def _layer_norm(x):
    mu = jnp.mean(x, axis=-1, keepdims=True)
    xc = x - mu
    var = jnp.mean(xc * xc, axis=-1, keepdims=True)
    return xc * lax.rsqrt(var + LN_EPS)


def _ada_body(c_ref, w_ref, b_ref, o_ref):
    o_ref[...] = _dot_hp(_silu(c_ref[...]), w_ref[...]) + b_ref[...]


def _ada_mod(c, w_ada, b_ada):
    bt = c.shape[0]
    n_col = w_ada.shape[1] // D_MODEL
    return pl.pallas_call(
        _ada_body,
        grid=(n_col,),
        in_specs=[
            pl.BlockSpec((bt, D_MODEL), lambda j: (0, 0)),
            pl.BlockSpec((D_MODEL, D_MODEL), lambda j: (0, j)),
            pl.BlockSpec((1, D_MODEL), lambda j: (0, j)),
        ],
        out_specs=pl.BlockSpec((bt, D_MODEL), lambda j: (0, j)),
        out_shape=jax.ShapeDtypeStruct((bt, w_ada.shape[1]), F32),
        compiler_params=_cparams(("arbitrary",)),
        name="ada_mod",
    )(c, w_ada, b_ada.reshape(1, -1))


def _front_body(x_ref, mod_ref, win_ref, wbgt_ref, cw_ref, dww_ref, dwb_ref, cng_ref, cnb_ref,
                gpc_ref, gpr_ref, sq_ref, sd_ref,
                q_ref, k_ref, v_ref, z_ref, bgc_ref, bgr_ref, uc_ref, nq_ref, nd_ref,
                qkv_buf, u_buf, ush_buf, *, tl):
    t = pl.program_id(1)

    @pl.when(t == 0)
    def _():
        qkv_buf[0:QKV_TAIL, :] = sq_ref[0]
        u_buf[0:DW_TAIL, :] = sd_ref[0]

    sh1 = mod_ref[0, 0:1, :]
    sc1 = mod_ref[0, 1:2, :]
    h = _layer_norm(x_ref[0]) * (1.0 + sc1) + sh1
    hb = h.astype(BF16)

    qkv_buf[QKV_TAIL:QKV_TAIL + tl, :] = _dot(hb, win_ref[:, 0:QKV_WIDTH])
    acc = cw_ref[0:1, :] * qkv_buf[QKV_TAIL - 3:QKV_TAIL - 3 + tl, :]
    for j in range(1, GDN_CONV):
        acc = acc + cw_ref[j:j + 1, :] * qkv_buf[QKV_TAIL - 3 + j:QKV_TAIL - 3 + j + tl, :]
    c = _silu(acc)
    for hd in range(GDN_HEADS):
        lo = hd * GDN_HEAD_DIM
        qh = c[:, lo:lo + GDN_HEAD_DIM]
        kh = c[:, GDN_WIDTH + lo:GDN_WIDTH + lo + GDN_HEAD_DIM]
        qn = qh * lax.rsqrt(jnp.sum(qh * qh, axis=-1, keepdims=True) + L2_EPS) * (GDN_HEAD_DIM ** -0.5)
        kn = kh * lax.rsqrt(jnp.sum(kh * kh, axis=-1, keepdims=True) + L2_EPS)
        q_ref[0, :, lo:lo + GDN_HEAD_DIM] = qn.astype(BF16)
        k_ref[0, :, lo:lo + GDN_HEAD_DIM] = kn.astype(BF16)
    v_ref[0] = c[:, 2 * GDN_WIDTH:3 * GDN_WIDTH].astype(BF16)
    z_ref[0] = _dot(hb, win_ref[:, COL_Z:COL_GV]).astype(BF16)

    raw_c = _dot(hb, win_ref[:, COL_BG:D_PROJ_PAD])
    lane = lax.broadcasted_iota(I32, raw_c.shape, 1)
    neg_a_c = -jnp.exp(gpc_ref[0:1, :])
    g_c = neg_a_c * _softplus(raw_c + gpc_ref[1:2, :])
    bgc_ref[0] = jnp.where(lane < GDN_HEADS, _sigmoid(raw_c), g_c)
    raw_r = _dot_nt(wbgt_ref[...], hb)
    row = lax.broadcasted_iota(I32, raw_r.shape, 0)
    neg_a_r = -jnp.exp(gpr_ref[:, 0:1])
    g_r = neg_a_r * _softplus(raw_r + gpr_ref[:, 1:2])
    bgr_ref[0] = jnp.where(row < GDN_HEADS, _sigmoid(raw_r), g_r)

    gv = _dot(hb, win_ref[:, COL_GV:COL_GG])
    gg = _dot(hb, win_ref[:, COL_GG:COL_BG])
    u_buf[DW_TAIL:DW_TAIL + tl, :] = gv * _sigmoid(gg)
    off = DW_TAIL - (DW_CONV - 1)
    dacc = dwb_ref[...]
    for r in range(SUBLANES):
        starts = [s for s in range(off, off + DW_CONV) if s % SUBLANES == r]
        n = starts[-1] - r + tl
        if r:
            ush_buf[0:n, :] = u_buf[r:r + n, :]
        src = ush_buf if r else u_buf
        for s in starts:
            dacc = dacc + dww_ref[s - off:s - off + 1, :] * src[s - r:s - r + tl, :]
    uc_ref[0] = _silu(_layer_norm(dacc) * cng_ref[...] + cnb_ref[...]).astype(BF16)

    nq = qkv_buf[tl:tl + QKV_TAIL, :]
    nd = u_buf[tl:tl + DW_TAIL, :]
    qkv_buf[0:QKV_TAIL, :] = nq
    u_buf[0:DW_TAIL, :] = nd
    nq_ref[0] = nq
    nd_ref[0] = nd


def _front(x, mod, sq_pad, sd_pad, prm, tl):
    b, l, _ = x.shape
    nt = l // tl
    tok = lambda w, dt: (pl.BlockSpec((1, tl, w), lambda i, j: (i, j, 0)), jax.ShapeDtypeStruct((b, l, w), dt))
    full = lambda a: pl.BlockSpec(a.shape, lambda i, j: (0,) * a.ndim)
    outs = [tok(GDN_WIDTH, BF16), tok(GDN_WIDTH, BF16), tok(GDN_WIDTH, BF16), tok(GDN_WIDTH, BF16),
            tok(LANES, F32),
            (pl.BlockSpec((1, SUBLANES, tl), lambda i, j: (i, 0, j)), jax.ShapeDtypeStruct((b, SUBLANES, l), F32)),
            tok(CONV_WIDTH, BF16),
            (pl.BlockSpec((1, QKV_TAIL, QKV_WIDTH), lambda i, j: (i, 0, 0)),
             jax.ShapeDtypeStruct((b, QKV_TAIL, QKV_WIDTH), F32)),
            (pl.BlockSpec((1, DW_TAIL, CONV_WIDTH), lambda i, j: (i, 0, 0)),
             jax.ShapeDtypeStruct((b, DW_TAIL, CONV_WIDTH), F32))]
    consts = [prm["w_in_r"], prm["w_bgt"], prm["conv_qkv_w"], prm["dw_w"], prm["dw_b"], prm["cn_g"], prm["cn_b"],
              prm["gp_c"], prm["gp_r"]]
    return pl.pallas_call(
        functools.partial(_front_body, tl=tl),
        grid=(b, nt),
        in_specs=[pl.BlockSpec((1, tl, D_MODEL), lambda i, j: (i, j, 0)),
                  pl.BlockSpec((1, 6, D_MODEL), lambda i, j: (i, 0, 0))]
                 + [full(a) for a in consts]
                 + [pl.BlockSpec((1, QKV_TAIL, QKV_WIDTH), lambda i, j: (i, 0, 0)),
                    pl.BlockSpec((1, DW_TAIL, CONV_WIDTH), lambda i, j: (i, 0, 0))],
        out_specs=[o[0] for o in outs],
        out_shape=[o[1] for o in outs],
        scratch_shapes=[pltpu.VMEM((QKV_TAIL + tl, QKV_WIDTH), F32), pltpu.VMEM((DW_TAIL + tl, CONV_WIDTH), F32),
                        pltpu.VMEM((DW_TAIL + tl, CONV_WIDTH), F32)],
        compiler_params=_cparams(("arbitrary", "arbitrary")),
        name="mixer_front",
    )(x, mod, *consts, sq_pad, sd_pad)


def _split3_bf16(x):
    hi = x.astype(BF16)
    r1 = x - hi.astype(F32)
    mid = r1.astype(BF16)
    lo = (r1 - mid.astype(F32)).astype(BF16)
    return hi, mid, lo


def _tri_inverse(ms, ri, ci):
    c = ms[0].shape[0]
    eye = jnp.where(ri == ci, 1.0, 0.0)
    pair = (ri >> 1) == (ci >> 1)
    ps = [eye - jnp.where(pair, m, 0.0) for m in ms]
    w = 2
    while w < c:
        s = w.bit_length() - 1
        sel = ((ri >> (s + 1)) == (ci >> (s + 1))) & ((ri >> s) > (ci >> s))
        pbs = [p.astype(BF16) for p in ps]
        xs = [_dot(pb, jnp.where(sel, m, 0.0).astype(BF16)).astype(BF16) for pb, m in zip(pbs, ms)]
        ps = [p - _dot(x, pb) for p, x, pb in zip(ps, xs, pbs)]
        w *= 2
    return ps


def _gdn_body(q_ref, k_ref, v_ref, z_ref, bgc_ref, bgr_ref, s0_ref, gnw_ref, o_ref, sout_ref,
              s_scr, ku_s, kw_s, au_s, qe_s, egl_s, *, c, g, unroll):
    cg = pl.program_id(1)

    @pl.when(cg == 0)
    def _():
        s_scr[...] = s0_ref[0]

    ri = lax.broadcasted_iota(I32, (c, c), 0)
    ci = lax.broadcasted_iota(I32, (c, c), 1)
    causal = ri >= ci
    strict = ri > ci
    lower = jnp.where(causal, 1.0, 0.0).astype(BF16)
    upper = jnp.where(ri <= ci, 1.0, 0.0).astype(BF16)
    gnw = gnw_ref[...]

    heads = range(GDN_HEADS)
    hcols = [slice(hd * GDN_HEAD_DIM, (hd + 1) * GDN_HEAD_DIM) for hd in heads]

    def intra(it, carry):
        prob = []
        for j in range(unroll):
            i = it * unroll + j
            rows = pl.ds(pl.multiple_of(i * c, c), c)
            bgc = bgc_ref[0, rows, :]
            bgr = bgr_ref[0, i]
            gc_c = sum(_dot(lower, part) for part in _split3_bf16(bgc))
            gc_r = sum(_dot(part, upper) for part in _split3_bf16(bgr))
            for hd in heads:
                gcc = gc_c[:, GDN_HEADS + hd:GDN_HEADS + hd + 1]
                gcr = gc_r[GDN_HEADS + hd:GDN_HEADS + hd + 1, :]
                prob.append(dict(i=i, hd=hd, rows=rows, cols=hcols[hd], beta=bgc[:, hd:hd + 1], gcc=gcc,
                                 decay=jnp.exp(jnp.where(causal, gcc - gcr, NEG_BIG))))
        for p in prob:
            p["kh"] = k_ref[0, p["rows"], p["cols"]]
            p["kb"] = p["kh"].astype(F32) * p["beta"]
        ms = [jnp.where(strict, _dot_nt(p["kb"].astype(BF16), p["kh"]) * p["decay"], 0.0) for p in prob]
        t_invs = _tri_inverse(ms, ri, ci)
        uws = []
        for p, t_inv in zip(prob, t_invs):
            p["egc"] = jnp.exp(p["gcc"])
            vf = v_ref[0, p["rows"], p["cols"]].astype(F32)
            rhs = jnp.concatenate([vf * p["beta"], p["kb"] * p["egc"]], axis=1).astype(BF16)
            uws.append(_dot(t_inv.astype(BF16), rhs).astype(BF16))
        for p in prob:
            p["qh"] = q_ref[0, p["rows"], p["cols"]]
            p["a"] = jnp.where(causal, _dot_nt(p["qh"], p["kh"]) * p["decay"], 0.0).astype(BF16)
            p["g_last"] = p["gcc"][c - 1:c, :]
            p["kd"] = (p["kh"].astype(F32) * jnp.exp(p["g_last"] - p["gcc"])).astype(BF16)
        kuws = [_dot_tn(p["kd"], uw) for p, uw in zip(prob, uws)]
        auws = [_dot(p["a"], uw) for p, uw in zip(prob, uws)]
        for p, kuw, auw in zip(prob, kuws, auws):
            rows, cols, hd = p["rows"], p["cols"], p["hd"]
            srows = pl.ds(pl.multiple_of(p["i"] * GDN_HEAD_DIM, GDN_HEAD_DIM), GDN_HEAD_DIM)
            ku_s[srows, cols] = kuw[:, :GDN_HEAD_DIM]
            kw_s[srows, cols] = kuw[:, GDN_HEAD_DIM:].astype(BF16)
            au_s[rows, cols] = auw[:, :GDN_HEAD_DIM]
            qe_s[rows, cols] = (p["qh"].astype(F32) * p["egc"] - auw[:, GDN_HEAD_DIM:]).astype(BF16)
            egl_s[pl.ds(p["i"] * SUBLANES + hd, 1), :] = jnp.broadcast_to(jnp.exp(p["g_last"]), (1, LANES))
        return carry

    lax.fori_loop(0, g // unroll, intra, 0)

    def inter(i, carry):
        rows = pl.ds(pl.multiple_of(i * c, c), c)
        srows = pl.ds(pl.multiple_of(i * GDN_HEAD_DIM, GDN_HEAD_DIM), GDN_HEAD_DIM)
        ss = [s_scr[hd] for hd in heads]
        sbs = [s.astype(BF16) for s in ss]
        upd = [_dot(kw_s[srows, hcols[hd]], sbs[hd]) for hd in heads]
        for hd in heads:
            s_scr[hd] = ss[hd] * egl_s[pl.ds(i * SUBLANES + hd, 1), :] + (ku_s[srows, hcols[hd]] - upd[hd])
        os_ = [_dot(qe_s[rows, hcols[hd]], sbs[hd]) + au_s[rows, hcols[hd]] for hd in heads]
        for hd in heads:
            o = os_[hd]
            gate = _silu(z_ref[0, rows, hcols[hd]].astype(F32))
            on = o * lax.rsqrt(jnp.mean(o * o, axis=-1, keepdims=True) + RMS_EPS) * gnw * gate
            o_ref[0, rows, hcols[hd]] = on.astype(BF16)
        return carry

    lax.fori_loop(0, g, inter, 0)

    @pl.when(cg == pl.num_programs(1) - 1)
    def _():
        sout_ref[0] = s_scr[...]


def _gdn(q, k, v, z, bgc, bgr, s0, gnw, c, g):
    b, l, _ = q.shape
    ncg = l // (c * g)
    tok = lambda w: pl.BlockSpec((1, c * g, w), lambda i, j: (i, j, 0))
    st = pl.BlockSpec((1, GDN_HEADS, GDN_HEAD_DIM, GDN_HEAD_DIM), lambda i, j: (i, 0, 0, 0))
    tok_buf = lambda dt: pltpu.VMEM((c * g, GDN_WIDTH), dt)
    state_buf = lambda dt: pltpu.VMEM((g * GDN_HEAD_DIM, GDN_WIDTH), dt)
    return pl.pallas_call(
        functools.partial(_gdn_body, c=c, g=g, unroll=min(g, 4)),
        grid=(b, ncg),
        in_specs=[tok(GDN_WIDTH), tok(GDN_WIDTH), tok(GDN_WIDTH), tok(GDN_WIDTH), tok(LANES),
                  pl.BlockSpec((1, g, SUBLANES, c), lambda i, j: (i, j, 0, 0)), st,
                  pl.BlockSpec((1, GDN_HEAD_DIM), lambda i, j: (0, 0))],
        out_specs=[tok(GDN_WIDTH), st],
        out_shape=[jax.ShapeDtypeStruct((b, l, GDN_WIDTH), BF16), jax.ShapeDtypeStruct(s0.shape, F32)],
        scratch_shapes=[pltpu.VMEM((GDN_HEADS, GDN_HEAD_DIM, GDN_HEAD_DIM), F32),
                        state_buf(F32), state_buf(BF16), tok_buf(F32), tok_buf(BF16),
                        pltpu.VMEM((g * SUBLANES, LANES), F32)],
        compiler_params=_cparams(("arbitrary", "arbitrary")),
        name="gdn",
    )(q, k, v, z, bgc, bgr, s0, gnw)


HALF = D_MODEL // 2
HI_MASK = 0xFFFF0000


def _pack_bf16_pairs(x):
    bits = pltpu.bitcast(x.astype(BF16).astype(F32), U32)
    return (bits[:, :HALF] >> 16) | (bits[:, HALF:] & jnp.uint32(HI_MASK))


def _unpack_bf16_pairs(p):
    return pltpu.bitcast(p << 16, F32), pltpu.bitcast(p & jnp.uint32(HI_MASK), F32)


def _mid_body(o_ref, uc_ref, x_ref, gt1_ref, sh2_ref, sc2_ref, gt2_ref,
              wout_ref, g1_ref, b1_ref, wsgu_ref, wsd_ref, wrh_ref, wrl_ref, pre_ref, h2p_ref, lgt_ref):
    gt1, sh2, sc2, gt2 = gt1_ref[0], sh2_ref[0], sc2_ref[0], gt2_ref[0]
    mix = _dot(jnp.concatenate([o_ref[0], uc_ref[0]], axis=1), wout_ref[...])
    x1 = _layer_norm(ALPHA * x_ref[0] + (1.0 + gt1) * mix) * g1_ref[...] + b1_ref[...]
    h2 = _layer_norm(x1) * (1.0 + sc2) + sh2
    hh, hl = _split_bf16(h2)
    lgt_ref[...] = _dot_nt(wrh_ref[...], hh) + (_dot_nt(wrh_ref[...], hl) + _dot_nt(wrl_ref[...], hh))
    gu = _dot(hh, wsgu_ref[...])
    act = _silu(gu[:, :D_SHARED]) * gu[:, D_SHARED:]
    shared = _dot(act.astype(BF16), wsd_ref[...])
    pre_ref[0] = ALPHA * x1 + (1.0 + gt2) * shared
    h2p_ref[0] = _pack_bf16_pairs(h2)


def _mod_spec(m, tl):
    if m.shape[1] == 1:
        return pl.BlockSpec((1, 1, D_MODEL), lambda i, j: (i, 0, 0))
    return pl.BlockSpec((1, tl, D_MODEL), lambda i, j: (i, j, 0))


def _mid(o, uc, x, mods, prm, tl):
    b, l, _ = x.shape
    nt = l // tl
    full = lambda a: pl.BlockSpec(a.shape, lambda i, j: (0,) * a.ndim)
    consts = [prm["w_out"], prm["ln1_g"], prm["ln1_b"], prm["ws_gu"], prm["ws_down"], prm["wr_hi"], prm["wr_lo"]]
    return pl.pallas_call(
        _mid_body,
        grid=(b, nt),
        in_specs=[pl.BlockSpec((1, tl, GDN_WIDTH), lambda i, j: (i, j, 0)),
                  pl.BlockSpec((1, tl, CONV_WIDTH), lambda i, j: (i, j, 0)),
                  pl.BlockSpec((1, tl, D_MODEL), lambda i, j: (i, j, 0))]
                 + [_mod_spec(m, tl) for m in mods] + [full(a) for a in consts],
        out_specs=[pl.BlockSpec((1, tl, D_MODEL), lambda i, j: (i, j, 0)),
                   pl.BlockSpec((1, tl, HALF), lambda i, j: (i, j, 0)),
                   pl.BlockSpec((N_EXPERTS, tl), lambda i, j: (0, i * nt + j))],
        out_shape=[jax.ShapeDtypeStruct((b, l, D_MODEL), F32), jax.ShapeDtypeStruct((b, l, HALF), U32),
                   jax.ShapeDtypeStruct((N_EXPERTS, b * l), F32)],
        compiler_params=_cparams(("arbitrary", "arbitrary")),
        name="mixer_out",
    )(o, uc, x, *mods, *consts)


def _first_max(x, row, n):
    m = jnp.max(x, axis=0, keepdims=True)
    ix = jnp.min(jnp.where(x == m, row, float(n)), axis=0, keepdims=True)
    return m, ix


def _route_body(lg_ref, bias_ref, idx_ref, w_ref, rank_ref, cnt_ref, cnt_scr, *, tr):
    @pl.when(pl.program_id(0) == 0)
    def _():
        cnt_scr[...] = jnp.zeros_like(cnt_scr)

    neg = -jnp.inf
    scores = _sigmoid(lg_ref[...])
    sel = scores + bias_ref[:, 0:1]
    row_g = lax.broadcasted_iota(I32, (GROUP_SIZE, tr), 0).astype(F32)
    gs = []
    for g in range(N_GROUPS):
        blk = sel[g * GROUP_SIZE:(g + 1) * GROUP_SIZE, :]
        m1, i1 = _first_max(blk, row_g, GROUP_SIZE)
        m2 = jnp.max(jnp.where(row_g == i1, neg, blk), axis=0, keepdims=True)
        gs.append(m1 + m2)
    gs = jnp.concatenate(gs, axis=0)
    row_n = lax.broadcasted_iota(I32, (N_GROUPS, tr), 0).astype(F32)
    chosen = jnp.zeros((N_GROUPS, tr), F32)
    for _ in range(TOPK_GROUPS):
        _, ix = _first_max(gs, row_n, N_GROUPS)
        hit = row_n == ix
        chosen = jnp.where(hit, 1.0, chosen)
        gs = jnp.where(hit, neg, gs)
    selm = jnp.concatenate(
        [jnp.where(chosen[g:g + 1, :] > 0.5, sel[g * GROUP_SIZE:(g + 1) * GROUP_SIZE, :], neg) for g in range(N_GROUPS)],
        axis=0)
    row_e = lax.broadcasted_iota(I32, (N_EXPERTS, tr), 0).astype(F32)
    idxs, ws = [], []
    picked = jnp.zeros((N_EXPERTS, tr), F32)
    for _ in range(TOP_K):
        _, ix = _first_max(selm, row_e, N_EXPERTS)
        hit = row_e == ix
        ws.append(jnp.sum(jnp.where(hit, scores, 0.0), axis=0, keepdims=True))
        idxs.append(ix)
        selm = jnp.where(hit, neg, selm)
        picked = jnp.where(hit, 1.0, picked)
    wsum = ws[0]
    for wk in ws[1:]:
        wsum = wsum + wk
    idx_ref[...] = jnp.concatenate(idxs, axis=0).astype(I32)
    w_ref[...] = jnp.concatenate(ws, axis=0) / wsum * ROUTED_SCALE
    ti = lax.broadcasted_iota(I32, (tr, tr), 0)
    tj = lax.broadcasted_iota(I32, (tr, tr), 1)
    before = _dot(picked.astype(BF16), jnp.where(ti < tj, 1.0, 0.0).astype(BF16)) + cnt_scr[:, 0:1]
    rank_ref[...] = jnp.concatenate(
        [jnp.sum(jnp.where(row_e == ix, before, 0.0), axis=0, keepdims=True) for ix in idxs], axis=0).astype(I32)
    cnt_scr[...] = cnt_scr[...] + jnp.sum(picked, axis=1, keepdims=True)
    cnt_ref[...] = cnt_scr[...]


def _route(logits_t, bias_b, tr):
    t = logits_t.shape[1]
    kt = lambda dt: (pl.BlockSpec((TOP_K, tr), lambda i: (0, i)), jax.ShapeDtypeStruct((TOP_K, t), dt))
    outs = [kt(I32), kt(F32), kt(I32),
            (pl.BlockSpec((N_EXPERTS, LANES), lambda i: (0, 0)), jax.ShapeDtypeStruct((N_EXPERTS, LANES), F32))]
    return pl.pallas_call(
        functools.partial(_route_body, tr=tr),
        grid=(t // tr,),
        in_specs=[pl.BlockSpec((N_EXPERTS, tr), lambda i: (0, i)), pl.BlockSpec((N_EXPERTS, LANES), lambda i: (0, 0))],
        out_specs=[o[0] for o in outs],
        out_shape=[o[1] for o in outs],
        scratch_shapes=[pltpu.VMEM((N_EXPERTS, LANES), F32)],
        compiler_params=_cparams(("arbitrary",)),
        name="route",
    )(logits_t, bias_b)


def _pos_body(idx_ref, rank_ref, cnt_ref, pos_ref, *, tr):
    ei = lax.broadcasted_iota(I32, (N_EXPERTS, N_EXPERTS), 0)
    ej = lax.broadcasted_iota(I32, (N_EXPERTS, N_EXPERTS), 1)
    below = jnp.where(ej < ei, 1.0, 0.0).astype(BF16)
    start = sum(_dot(below, part) for part in _split3_bf16(cnt_ref[...]))[:, 0:1]
    row_e = lax.broadcasted_iota(I32, (N_EXPERTS, tr), 0)
    pos_ref[...] = jnp.concatenate(
        [jnp.sum(jnp.where(row_e == idx_ref[k:k + 1, :], start, 0.0), axis=0, keepdims=True) for k in range(TOP_K)],
        axis=0).astype(I32) + rank_ref[...]


def _positions(idx, rank, cnt, tr):
    t = idx.shape[1]
    kt = pl.BlockSpec((TOP_K, tr), lambda i: (0, i))
    return pl.pallas_call(
        functools.partial(_pos_body, tr=tr),
        grid=(t // tr,),
        in_specs=[kt, kt, pl.BlockSpec((N_EXPERTS, LANES), lambda i: (0, 0))],
        out_specs=kt,
        out_shape=jax.ShapeDtypeStruct((TOP_K, t), I32),
        compiler_params=_cparams(("arbitrary",)),
        name="positions",
    )(idx, rank, cnt)


def _row_copy_wait(src_ref, dst_ref, sem, n):
    for _ in range(n):
        pltpu.make_async_copy(src_ref, dst_ref, sem).wait()


def _dispatch_body(pos_ref, hp_ref, hs_ref, xs_ref, sem, *, tt, n_prompt_tiles):
    i = pl.program_id(0)

    def scatter(src_ref):
        def body(t, carry):
            for k in range(TOP_K):
                pltpu.make_async_copy(src_ref.at[pl.ds(t, 1)], xs_ref.at[pl.ds(pos_ref[k, t], 1)], sem).start()
            return carry
        lax.fori_loop(0, tt, body, 0)
        _row_copy_wait(src_ref, xs_ref.at[pl.ds(0, tt)], sem, TOP_K)

    @pl.when(i < n_prompt_tiles)
    def _():
        scatter(hp_ref)

    @pl.when(i >= n_prompt_tiles)
    def _():
        scatter(hs_ref)


def _dispatch(pos, h_prompt, h_sample, tt):
    tp, ts = h_prompt.shape[0], h_sample.shape[0]
    ntp, nts = tp // tt, ts // tt
    n_rows = TOP_K * (tp + ts)
    return pl.pallas_call(
        functools.partial(_dispatch_body, tt=tt, n_prompt_tiles=ntp),
        grid=(ntp + nts,),
        in_specs=[pl.BlockSpec((TOP_K, tt), lambda i: (0, i), memory_space=pltpu.SMEM),
                  pl.BlockSpec((tt, HALF), lambda i: (jnp.minimum(i, ntp - 1), 0)),
                  pl.BlockSpec((tt, HALF), lambda i: (jnp.maximum(i - ntp, 0), 0))],
        out_specs=pl.BlockSpec(memory_space=pl.ANY),
        out_shape=jax.ShapeDtypeStruct((n_rows, HALF), U32),
        scratch_shapes=[pltpu.SemaphoreType.DMA(())],
        compiler_params=_cparams(("arbitrary",)),
        name="dispatch",
    )(pos, h_prompt, h_sample)


GMM_ROWS = 256


def _gmm_body(ve_ref, vb_ref, vlo_ref, vhi_ref, vfirst_ref, vnew_ref,
              xs_ref, wg_ref, wu_ref, wd_ref, ys_ref, wgu_s, wd_s):
    v = pl.program_id(0)
    lo = vlo_ref[v]
    hi = vhi_ref[v]

    @pl.when(vnew_ref[v] == 1)
    def _():
        wgu_s[:, :D_EXPERT] = wg_ref[0].astype(BF16)
        wgu_s[:, D_EXPERT:] = wu_ref[0].astype(BF16)
        wd_s[...] = wd_ref[0].astype(BF16)

    @pl.when(hi > lo)
    def _():
        x_lo, x_hi = _unpack_bf16_pairs(xs_ref[...])
        gu = _dot(x_lo.astype(BF16), wgu_s[:HALF, :]) + _dot(x_hi.astype(BF16), wgu_s[HALF:, :])
        act = _silu(gu[:, :D_EXPERT]) * gu[:, D_EXPERT:]
        y = _pack_bf16_pairs(_dot(act.astype(BF16), wd_s[...]))
        row = lax.broadcasted_iota(I32, y.shape, 0)
        mine = (row >= lo) & (row < hi)

        @pl.when(vfirst_ref[v] == 1)
        def _():
            ys_ref[...] = jnp.where(mine, y, jnp.uint32(0))

        @pl.when(vfirst_ref[v] == 0)
        def _():
            ys_ref[...] = jnp.where(mine, y, ys_ref[...])


def _gmm_schedule(counts, n_rows):
    nb = n_rows // GMM_ROWS
    n_vis = nb + N_EXPERTS
    ends = jnp.cumsum(counts)
    starts = ends - counts
    first_blk = starts // GMM_ROWS
    last_blk = jnp.maximum(ends - 1, 0) // GMM_ROWS
    per_e = jnp.where(counts > 0, last_blk - first_blk + 1, 0)
    vis_end = jnp.cumsum(per_e)
    total = vis_end[-1]
    v = jnp.minimum(jnp.arange(n_vis, dtype=I32), total - 1)
    e = jnp.minimum(jnp.searchsorted(vis_end, v, side="right"), N_EXPERTS - 1).astype(I32)
    blk = (first_blk[e] + (v - (vis_end[e] - per_e[e]))).astype(I32)
    lo = jnp.maximum(starts[e], blk * GMM_ROWS) - blk * GMM_ROWS
    hi = jnp.minimum(ends[e], (blk + 1) * GMM_ROWS) - blk * GMM_ROWS
    real = jnp.arange(n_vis) < total
    hi = jnp.where(real, hi, lo)
    prev = lambda a: jnp.concatenate([jnp.full((1,), -1, I32), a[:-1]])
    first = (blk != prev(blk)).astype(I32)
    new_e = (e != prev(e)).astype(I32)
    return e, blk, lo.astype(I32), hi.astype(I32), first, new_e


def _gmm(xs, counts, we_gate, we_up, we_down):
    n_rows = xs.shape[0]
    sched = _gmm_schedule(counts, n_rows)
    n_vis = sched[0].shape[0]
    grid_spec = pltpu.PrefetchScalarGridSpec(
        num_scalar_prefetch=len(sched),
        grid=(n_vis,),
        in_specs=[pl.BlockSpec((GMM_ROWS, HALF), lambda v, ve, vb, *_: (vb[v], 0)),
                  pl.BlockSpec((1, D_MODEL, D_EXPERT), lambda v, ve, *_: (ve[v], 0, 0)),
                  pl.BlockSpec((1, D_MODEL, D_EXPERT), lambda v, ve, *_: (ve[v], 0, 0)),
                  pl.BlockSpec((1, D_EXPERT, D_MODEL), lambda v, ve, *_: (ve[v], 0, 0))],
        out_specs=pl.BlockSpec((GMM_ROWS, HALF), lambda v, ve, vb, *_: (vb[v], 0)),
        scratch_shapes=[pltpu.VMEM((D_MODEL, 2 * D_EXPERT), BF16), pltpu.VMEM((D_EXPERT, D_MODEL), BF16)])
    return pl.pallas_call(
        _gmm_body,
        grid_spec=grid_spec,
        out_shape=jax.ShapeDtypeStruct((n_rows, HALF), U32),
        compiler_params=_cparams(("arbitrary",)),
        name="expert_ffn",
    )(*sched, xs, we_gate, we_up, we_down)


def _combine_body(pos_ref, w_ref, pre_ref, gt2_ref, g2_ref, b2_ref, ys_ref, out_ref, gbuf, sem, *, tt):
    def body(t, carry):
        for k in range(TOP_K):
            pltpu.make_async_copy(ys_ref.at[pl.ds(pos_ref[k, t], 1)], gbuf.at[k, pl.ds(t, 1)], sem).start()
        return carry
    lax.fori_loop(0, tt, body, 0)
    for k in range(TOP_K):
        pltpu.make_async_copy(ys_ref.at[pl.ds(0, tt)], gbuf.at[k], sem).wait()
    acc_lo = jnp.zeros((tt, HALF), F32)
    acc_hi = jnp.zeros((tt, HALF), F32)
    for k in range(TOP_K):
        y_lo, y_hi = _unpack_bf16_pairs(gbuf[k])
        wk = w_ref[:, k:k + 1]
        acc_lo = acc_lo + wk * y_lo
        acc_hi = acc_hi + wk * y_hi
    routed = jnp.concatenate([acc_lo, acc_hi], axis=1)
    y = pre_ref[0] + (1.0 + gt2_ref[0]) * routed
    out_ref[0] = _layer_norm(y) * g2_ref[...] + b2_ref[...]


def _combine(pos, w_tk, pre, gt2, ln2_g, ln2_b, ys, tt, tok0):
    b, l, _ = pre.shape
    nt = l // tt
    blk0 = tok0 // tt
    return pl.pallas_call(
        functools.partial(_combine_body, tt=tt),
        grid=(b, nt),
        in_specs=[pl.BlockSpec((TOP_K, tt), lambda i, j: (0, blk0 + i * nt + j), memory_space=pltpu.SMEM),
                  pl.BlockSpec((tt, TOP_K), lambda i, j: (blk0 + i * nt + j, 0)),
                  pl.BlockSpec((1, tt, D_MODEL), lambda i, j: (i, j, 0)),
                  _mod_spec(gt2, tt),
                  pl.BlockSpec((1, D_MODEL), lambda i, j: (0, 0)),
                  pl.BlockSpec((1, D_MODEL), lambda i, j: (0, 0)),
                  pl.BlockSpec(memory_space=pl.ANY)],
        out_specs=pl.BlockSpec((1, tt, D_MODEL), lambda i, j: (i, j, 0)),
        out_shape=jax.ShapeDtypeStruct((b, l, D_MODEL), F32),
        scratch_shapes=[pltpu.VMEM((TOP_K, tt, HALF), U32), pltpu.SemaphoreType.DMA(())],
        compiler_params=_cparams(("arbitrary", "arbitrary")),
        name="combine",
    )(pos, w_tk, pre, gt2, ln2_g, ln2_b, ys)


def _prep_params(w_in, conv_qkv_w, a_log, dt_bias, dw_w, dw_b, cn_g, cn_b):
    z0 = QKV_WIDTH
    b0 = z0 + GDN_WIDTH
    g0 = b0 + 2 * GDN_HEADS
    w_bg = w_in[:, b0:g0]
    w_in_r = jnp.concatenate(
        [w_in[:, :b0], w_in[:, g0:], w_bg, jnp.zeros((D_MODEL, LANES - 2 * GDN_HEADS), w_in.dtype)], axis=1).astype(BF16)
    pad_h = jnp.zeros((GDN_HEADS,), F32)
    al = jnp.concatenate([pad_h, a_log.astype(F32)])
    db = jnp.concatenate([pad_h, dt_bias.astype(F32)])
    gp_c = jnp.zeros((SUBLANES, LANES), F32).at[0, :2 * GDN_HEADS].set(al).at[1, :2 * GDN_HEADS].set(db)
    gp_r = jnp.zeros((SUBLANES, LANES), F32).at[:, 0].set(al).at[:, 1].set(db)
    return dict(
        w_in_r=w_in_r, w_bgt=w_bg.T.astype(BF16), conv_qkv_w=conv_qkv_w.astype(F32),
        dw_w=jnp.concatenate([dw_w, jnp.zeros((1, CONV_WIDTH), dw_w.dtype)], axis=0).astype(F32),
        dw_b=dw_b.reshape(1, -1).astype(F32), cn_g=cn_g.reshape(1, -1).astype(F32), cn_b=cn_b.reshape(1, -1).astype(F32),
        gp_c=gp_c, gp_r=gp_r)


def _prep_mid_params(w_out, ln1_g, ln1_b, w_router, router_bias, ws_gate, ws_up, ws_down, ln2_g, ln2_b, gdn_norm_w):
    row = lambda a: a.reshape(1, -1).astype(F32)
    wr_t = w_router.astype(F32).T
    wr_hi = wr_t.astype(BF16)
    return dict(
        w_out=w_out.astype(BF16), ln1_g=row(ln1_g), ln1_b=row(ln1_b),
        ws_gu=jnp.concatenate([ws_gate, ws_up], axis=1).astype(BF16), ws_down=ws_down.astype(BF16),
        wr_hi=wr_hi, wr_lo=(wr_t - wr_hi.astype(F32)).astype(BF16),
        bias_b=jnp.broadcast_to(router_bias.astype(F32)[:, None], (N_EXPERTS, LANES)),
        ln2_g=row(ln2_g), ln2_b=row(ln2_b), gnw=row(gdn_norm_w))


def _tile(n, pref):
    t = min(pref, n)
    while n % t:
        t //= 2
    return t


def _token_mixer(x, mod, s_gdn, s_qkv, s_dw, prm, mprm):
    b, l, _ = x.shape
    assert l >= DW_CONV - 1 and l % SUBLANES == 0
    tl = _tile(l, 256)
    sq_pad = jnp.pad(s_qkv.astype(F32), ((0, 0), (QKV_TAIL - (GDN_CONV - 1), 0), (0, 0)))
    sd_pad = jnp.pad(s_dw.astype(F32), ((0, 0), (DW_TAIL - (DW_CONV - 1), 0), (0, 0)))
    q, k, v, z, bgc, bgr, uc, nq, nd = _front(x, mod, sq_pad, sd_pad, prm, tl)
    c = min(CHUNK, l)
    n_chunks = l // c
    g = _tile(n_chunks, 8)
    bgr = bgr.reshape(b, SUBLANES, n_chunks, c).transpose(0, 2, 1, 3)
    o, s_new = _gdn(q, k, v, z, bgc, bgr, s_gdn.astype(F32), mprm["gnw"], c, g)
    return o, uc, s_new, nq[:, QKV_TAIL - (GDN_CONV - 1):], nd[:, DW_TAIL - (DW_CONV - 1):]


def kernel(x_prompt, x_sample, state_gdn, state_qkv_conv, state_dw_conv, c_prompt, c_sample, w_ada, b_ada, w_in, conv_qkv_w, a_log, dt_bias, gdn_norm_w, dw_w, dw_b, cn_g, cn_b, w_out, ln1_g, ln1_b, w_router, router_bias, we_gate, we_up, we_down, ws_gate, ws_up, ws_down, ln2_g, ln2_b):
    bp, lp, _ = x_prompt.shape
    bs, ls, _ = x_sample.shape
    tp, ts = bp * lp, bs * ls
    yp, ys = x_prompt, x_sample
    c_all = jnp.concatenate([c_prompt, c_sample], axis=0)
    new_p, new_s = [], []
    for l in range(w_ada.shape[0]):
        prm = _prep_params(w_in[l], conv_qkv_w[l], a_log[l], dt_bias[l], dw_w[l], dw_b[l], cn_g[l], cn_b[l])
        mprm = _prep_mid_params(w_out[l], ln1_g[l], ln1_b[l], w_router[l], router_bias[l], ws_gate[l], ws_up[l],
                                ws_down[l], ln2_g[l], ln2_b[l], gdn_norm_w[l])
        mod = _ada_mod(c_all, w_ada[l], b_ada[l]).reshape(bp + bs, 6, D_MODEL)
        mod_p, mod_s = mod[:bp], mod[bp:]

        zg = jnp.zeros((bp, GDN_HEADS, GDN_HEAD_DIM, GDN_HEAD_DIM), F32)
        zq = jnp.zeros((bp, GDN_CONV - 1, QKV_WIDTH), F32)
        zd = jnp.zeros((bp, DW_CONV - 1, CONV_WIDTH), F32)
        o_p, uc_p, g_p, q_p, d_p = _token_mixer(yp, mod_p, zg, zq, zd, prm, mprm)
        o_s, uc_s, g_s, q_s, d_s = _token_mixer(ys, mod_s, state_gdn[l], state_qkv_conv[l], state_dw_conv[l], prm, mprm)

        mods_p = tuple(mod_p[:, j:j + 1, :] for j in (2, 3, 4, 5))
        mods_s = tuple(jnp.repeat(mod_s[:, j, :], ls, axis=0)[None] for j in (2, 3, 4, 5))
        flat = lambda a: a.reshape(1, ts, a.shape[-1])
        pre_p, h_p, lg_p = _mid(o_p, uc_p, yp, mods_p, mprm, _tile(lp, 256))
        pre_s, h_s, lg_s = _mid(flat(o_s), flat(uc_s), flat(ys), mods_s, mprm, _tile(ts, 256))

        tt = _tile(ts, 512)
        assert tp % tt == 0 and lp % tt == 0 and (TOP_K * (tp + ts)) % GMM_ROWS == 0
        idx, w, rank, cnt = _route(jnp.concatenate([lg_p, lg_s], axis=1), mprm["bias_b"], tt)
        pos = _positions(idx, rank, cnt, tt)
        xs = _dispatch(pos, h_p.reshape(tp, HALF), h_s.reshape(ts, HALF), tt)
        ye = _gmm(xs, cnt[:, 0].astype(I32), we_gate[l], we_up[l], we_down[l])
        w_tk = w.T
        yp = _combine(pos, w_tk, pre_p, mods_p[3], mprm["ln2_g"], mprm["ln2_b"], ye, tt, 0)
        ys = _combine(pos, w_tk, pre_s, mods_s[3], mprm["ln2_g"], mprm["ln2_b"], ye, tt, tp).reshape(bs, ls, D_MODEL)
        new_p.append((g_p.astype(state_gdn.dtype), q_p.astype(x_prompt.dtype), d_p.astype(x_prompt.dtype)))
        new_s.append((g_s.astype(state_gdn.dtype), q_s.astype(state_qkv_conv.dtype), d_s.astype(state_dw_conv.dtype)))
    stack = lambda rows, j: jnp.stack([r[j] for r in rows])
    return (yp, ys, stack(new_p, 0), stack(new_p, 1), stack(new_p, 2), stack(new_s, 0), stack(new_s, 1), stack(new_s, 2))
```

```python
import functools

import jax
import jax.numpy as jnp
from jax import lax
from jax.experimental import pallas as pl
from jax.experimental.pallas import tpu as pltpu

F32 = jnp.float32
BF16 = jnp.bfloat16
I32 = jnp.int32
U32 = jnp.uint32

D_MODEL = 1024
GDN_WIDTH = 512
CONV_WIDTH = 512
GDN_HEAD_DIM = 128
GDN_HEADS = 4
QKV_WIDTH = 3 * GDN_WIDTH
GDN_CONV = 4
DW_CONV = 31
CHUNK = 64
N_EXPERTS = 256
TOP_K = 8
N_GROUPS = 8
GROUP_SIZE = N_EXPERTS // N_GROUPS
TOPK_GROUPS = 4
D_EXPERT = 256
D_SHARED = 256
ROUTED_SCALE = 2.5
LN_EPS = 1e-5
RMS_EPS = 1e-6
L2_EPS = 1e-6
DEPTH = 1
ALPHA = (2.0 * DEPTH) ** 0.25

LANES = 128
SUBLANES = 8
VMEM_LIMIT_BYTES = 56 * 1024 * 1024

COL_Z = QKV_WIDTH
COL_GV = COL_Z + GDN_WIDTH
COL_GG = COL_GV + CONV_WIDTH
COL_BG = COL_GG + CONV_WIDTH
D_PROJ_PAD = COL_BG + LANES
QKV_TAIL = SUBLANES
DW_TAIL = 32
NEG_BIG = -1e30


def _cparams(sem):
    return pltpu.CompilerParams(dimension_semantics=sem, vmem_limit_bytes=VMEM_LIMIT_BYTES)


def _split_bf16(x):
    hi = x.astype(BF16)
    lo = (x - hi.astype(F32)).astype(BF16)
    return hi, lo


def _dot(a, b):
    return jnp.dot(a, b, preferred_element_type=F32)


def _dot_nt(a, b):
    return lax.dot_general(a, b, (((1,), (1,)), ((), ())), preferred_element_type=F32)


def _dot_tn(a, b):
    return lax.dot_general(a, b, (((0,), (0,)), ((), ())), preferred_element_type=F32)


def _dot_hp(a, b):
    ah, al = _split_bf16(a)
    bh, bl = _split_bf16(b)
    return _dot(ah, bh) + (_dot(ah, bl) + _dot(al, bh))


def _sigmoid(x):
    return 1.0 / (1.0 + jnp.exp(-x))


def _silu(x):
    return x * _sigmoid(x)


def _softplus(x):
    return jnp.maximum(x, 0.0) + jnp.log(1.0 + jnp.exp(-jnp.abs(x)))


def _layer_norm(x):
    mu = jnp.mean(x, axis=-1, keepdims=True)
    xc = x - mu
    var = jnp.mean(xc * xc, axis=-1, keepdims=True)
    return xc * lax.rsqrt(var + LN_EPS)


def _ada_body(c_ref, w_ref, b_ref, o_ref):
    o_ref[...] = _dot_hp(_silu(c_ref[...]), w_ref[...]) + b_ref[...]


def _ada_mod(c, w_ada, b_ada):
    bt = c.shape[0]
    n_col = w_ada.shape[1] // D_MODEL
    return pl.pallas_call(
        _ada_body,
        grid=(n_col,),
        in_specs=[
            pl.BlockSpec((bt, D_MODEL), lambda j: (0, 0)),
            pl.BlockSpec((D_MODEL, D_MODEL), lambda j: (0, j)),
            pl.BlockSpec((1, D_MODEL), lambda j: (0, j)),
        ],
        out_specs=pl.BlockSpec((bt, D_MODEL), lambda j: (0, j)),
        out_shape=jax.ShapeDtypeStruct((bt, w_ada.shape[1]), F32),
        compiler_params=_cparams(("arbitrary",)),
        name="ada_mod",
    )(c, w_ada, b_ada.reshape(1, -1))


def _front_body(x_ref, mod_ref, win_ref, wbgt_ref, cw_ref, dww_ref, dwb_ref, cng_ref, cnb_ref,
                gpc_ref, gpr_ref, sq_ref, sd_ref,
                q_ref, k_ref, v_ref, z_ref, bgc_ref, bgr_ref, uc_ref, nq_ref, nd_ref,
                qkv_buf, u_buf, ush_buf, *, tl):
    t = pl.program_id(1)

    @pl.when(t == 0)
    def _():
        qkv_buf[0:QKV_TAIL, :] = sq_ref[0]
        u_buf[0:DW_TAIL, :] = sd_ref[0]

    sh1 = mod_ref[0, 0:1, :]
    sc1 = mod_ref[0, 1:2, :]
    h = _layer_norm(x_ref[0]) * (1.0 + sc1) + sh1
    hb = h.astype(BF16)

    qkv_buf[QKV_TAIL:QKV_TAIL + tl, :] = _dot(hb, win_ref[:, 0:QKV_WIDTH])
    acc = cw_ref[0:1, :] * qkv_buf[QKV_TAIL - 3:QKV_TAIL - 3 + tl, :]
    for j in range(1, GDN_CONV):
        acc = acc + cw_ref[j:j + 1, :] * qkv_buf[QKV_TAIL - 3 + j:QKV_TAIL - 3 + j + tl, :]
    c = _silu(acc)
    for hd in range(GDN_HEADS):
        lo = hd * GDN_HEAD_DIM
        qh = c[:, lo:lo + GDN_HEAD_DIM]
        kh = c[:, GDN_WIDTH + lo:GDN_WIDTH + lo + GDN_HEAD_DIM]
        qn = qh * lax.rsqrt(jnp.sum(qh * qh, axis=-1, keepdims=True) + L2_EPS) * (GDN_HEAD_DIM ** -0.5)
        kn = kh * lax.rsqrt(jnp.sum(kh * kh, axis=-1, keepdims=True) + L2_EPS)
        q_ref[0, :, lo:lo + GDN_HEAD_DIM] = qn.astype(BF16)
        k_ref[0, :, lo:lo + GDN_HEAD_DIM] = kn.astype(BF16)
    v_ref[0] = c[:, 2 * GDN_WIDTH:3 * GDN_WIDTH].astype(BF16)
    z_ref[0] = _dot(hb, win_ref[:, COL_Z:COL_GV]).astype(BF16)

    raw_c = _dot(hb, win_ref[:, COL_BG:D_PROJ_PAD])
    lane = lax.broadcasted_iota(I32, raw_c.shape, 1)
    neg_a_c = -jnp.exp(gpc_ref[0:1, :])
    g_c = neg_a_c * _softplus(raw_c + gpc_ref[1:2, :])
    bgc_ref[0] = jnp.where(lane < GDN_HEADS, _sigmoid(raw_c), g_c)
    raw_r = _dot_nt(wbgt_ref[...], hb)
    row = lax.broadcasted_iota(I32, raw_r.shape, 0)
    neg_a_r = -jnp.exp(gpr_ref[:, 0:1])
    g_r = neg_a_r * _softplus(raw_r + gpr_ref[:, 1:2])
    bgr_ref[0] = jnp.where(row < GDN_HEADS, _sigmoid(raw_r), g_r)

    gv = _dot(hb, win_ref[:, COL_GV:COL_GG])
    gg = _dot(hb, win_ref[:, COL_GG:COL_BG])
    u_buf[DW_TAIL:DW_TAIL + tl, :] = gv * _sigmoid(gg)
    off = DW_TAIL - (DW_CONV - 1)
    dacc = dwb_ref[...]
    for r in range(SUBLANES):
        starts = [s for s in range(off, off + DW_CONV) if s % SUBLANES == r]
        n = starts[-1] - r + tl
        if r:
            ush_buf[0:n, :] = u_buf[r:r + n, :]
        src = ush_buf if r else u_buf
        for s in starts:
            dacc = dacc + dww_ref[s - off:s - off + 1, :] * src[s - r:s - r + tl, :]
    uc_ref[0] = _silu(_layer_norm(dacc) * cng_ref[...] + cnb_ref[...]).astype(BF16)

    nq = qkv_buf[tl:tl + QKV_TAIL, :]
    nd = u_buf[tl:tl + DW_TAIL, :]
    qkv_buf[0:QKV_TAIL, :] = nq
    u_buf[0:DW_TAIL, :] = nd
    nq_ref[0] = nq
    nd_ref[0] = nd


def _front(x, mod, sq_pad, sd_pad, prm, tl):
    b, l, _ = x.shape
    nt = l // tl
    tok = lambda w, dt: (pl.BlockSpec((1, tl, w), lambda i, j: (i, j, 0)), jax.ShapeDtypeStruct((b, l, w), dt))
    full = lambda a: pl.BlockSpec(a.shape, lambda i, j: (0,) * a.ndim)
    outs = [tok(GDN_WIDTH, BF16), tok(GDN_WIDTH, BF16), tok(GDN_WIDTH, BF16), tok(GDN_WIDTH, BF16),
            tok(LANES, F32),
            (pl.BlockSpec((1, SUBLANES, tl), lambda i, j: (i, 0, j)), jax.ShapeDtypeStruct((b, SUBLANES, l), F32)),
            tok(CONV_WIDTH, BF16),
            (pl.BlockSpec((1, QKV_TAIL, QKV_WIDTH), lambda i, j: (i, 0, 0)),
             jax.ShapeDtypeStruct((b, QKV_TAIL, QKV_WIDTH), F32)),
            (pl.BlockSpec((1, DW_TAIL, CONV_WIDTH), lambda i, j: (i, 0, 0)),
             jax.ShapeDtypeStruct((b, DW_TAIL, CONV_WIDTH), F32))]
    consts = [prm["w_in_r"], prm["w_bgt"], prm["conv_qkv_w"], prm["dw_w"], prm["dw_b"], prm["cn_g"], prm["cn_b"],
              prm["gp_c"], prm["gp_r"]]
    return pl.pallas_call(
        functools.partial(_front_body, tl=tl),
        grid=(b, nt),
        in_specs=[pl.BlockSpec((1, tl, D_MODEL), lambda i, j: (i, j, 0)),
                  pl.BlockSpec((1, 6, D_MODEL), lambda i, j: (i, 0, 0))]
                 + [full(a) for a in consts]
                 + [pl.BlockSpec((1, QKV_TAIL, QKV_WIDTH), lambda i, j: (i, 0, 0)),
                    pl.BlockSpec((1, DW_TAIL, CONV_WIDTH), lambda i, j: (i, 0, 0))],
        out_specs=[o[0] for o in outs],
        out_shape=[o[1] for o in outs],
        scratch_shapes=[pltpu.VMEM((QKV_TAIL + tl, QKV_WIDTH), F32), pltpu.VMEM((DW_TAIL + tl, CONV_WIDTH), F32),
                        pltpu.VMEM((DW_TAIL + tl, CONV_WIDTH), F32)],
        compiler_params=_cparams(("arbitrary", "arbitrary")),
        name="mixer_front",
    )(x, mod, *consts, sq_pad, sd_pad)


def _split3_bf16(x):
    hi = x.astype(BF16)
    r1 = x - hi.astype(F32)
    mid = r1.astype(BF16)
    lo = (r1 - mid.astype(F32)).astype(BF16)
    return hi, mid, lo


def _tri_inverse(ms, ri, ci):
    c = ms[0].shape[0]
    eye = jnp.where(ri == ci, 1.0, 0.0)
    pair = (ri >> 1) == (ci >> 1)
    ps = [eye - jnp.where(pair, m, 0.0) for m in ms]
    w = 2
    while w < c:
        s = w.bit_length() - 1
        sel = ((ri >> (s + 1)) == (ci >> (s + 1))) & ((ri >> s) > (ci >> s))
        pbs = [p.astype(BF16) for p in ps]
        xs = [_dot(pb, jnp.where(sel, m, 0.0).astype(BF16)).astype(BF16) for pb, m in zip(pbs, ms)]
        ps = [p - _dot(x, pb) for p, x, pb in zip(ps, xs, pbs)]
        w *= 2
    return ps


def _gdn_body(q_ref, k_ref, v_ref, z_ref, bgc_ref, bgr_ref, s0_ref, gnw_ref, o_ref, sout_ref,
              s_scr, ku_s, kw_s, au_s, qe_s, egl_s, *, c, g, unroll):
    cg = pl.program_id(1)

    @pl.when(cg == 0)
    def _():
        s_scr[...] = s0_ref[0]

    ri = lax.broadcasted_iota(I32, (c, c), 0)
    ci = lax.broadcasted_iota(I32, (c, c), 1)
    causal = ri >= ci
    strict = ri > ci
    lower = jnp.where(causal, 1.0, 0.0).astype(BF16)
    upper = jnp.where(ri <= ci, 1.0, 0.0).astype(BF16)
    gnw = gnw_ref[...]

    heads = range(GDN_HEADS)
    hcols = [slice(hd * GDN_HEAD_DIM, (hd + 1) * GDN_HEAD_DIM) for hd in heads]

    def intra(it, carry):
        prob = []
        for j in range(unroll):
            i = it * unroll + j
            rows = pl.ds(pl.multiple_of(i * c, c), c)
            bgc = bgc_ref[0, rows, :]
            bgr = bgr_ref[0, i]
            gc_c = sum(_dot(lower, part) for part in _split3_bf16(bgc))
            gc_r = sum(_dot(part, upper) for part in _split3_bf16(bgr))
            for hd in heads:
                gcc = gc_c[:, GDN_HEADS + hd:GDN_HEADS + hd + 1]
                gcr = gc_r[GDN_HEADS + hd:GDN_HEADS + hd + 1, :]
                prob.append(dict(i=i, hd=hd, rows=rows, cols=hcols[hd], beta=bgc[:, hd:hd + 1], gcc=gcc,
                                 decay=jnp.exp(jnp.where(causal, gcc - gcr, NEG_BIG))))
        for p in prob:
            p["kh"] = k_ref[0, p["rows"], p["cols"]]
            p["kb"] = p["kh"].astype(F32) * p["beta"]
        ms = [jnp.where(strict, _dot_nt(p["kb"].astype(BF16), p["kh"]) * p["decay"], 0.0) for p in prob]
        t_invs = _tri_inverse(ms, ri, ci)
        uws = []
        for p, t_inv in zip(prob, t_invs):
            p["egc"] = jnp.exp(p["gcc"])
            vf = v_ref[0, p["rows"], p["cols"]].astype(F32)
            rhs = jnp.concatenate([vf * p["beta"], p["kb"] * p["egc"]], axis=1).astype(BF16)
            uws.append(_dot(t_inv.astype(BF16), rhs).astype(BF16))
        for p in prob:
            p["qh"] = q_ref[0, p["rows"], p["cols"]]
            p["a"] = jnp.where(causal, _dot_nt(p["qh"], p["kh"]) * p["decay"], 0.0).astype(BF16)
            p["g_last"] = p["gcc"][c - 1:c, :]
            p["kd"] = (p["kh"].astype(F32) * jnp.exp(p["g_last"] - p["gcc"])).astype(BF16)
        kuws = [_dot_tn(p["kd"], uw) for p, uw in zip(prob, uws)]
        auws = [_dot(p["a"], uw) for p, uw in zip(prob, uws)]
        for p, kuw, auw in zip(prob, kuws, auws):
            rows, cols, hd = p["rows"], p["cols"], p["hd"]
            srows = pl.ds(pl.multiple_of(p["i"] * GDN_HEAD_DIM, GDN_HEAD_DIM), GDN_HEAD_DIM)
            ku_s[srows, cols] = kuw[:, :GDN_HEAD_DIM]
            kw_s[srows, cols] = kuw[:, GDN_HEAD_DIM:].astype(BF16)
            au_s[rows, cols] = auw[:, :GDN_HEAD_DIM]
            qe_s[rows, cols] = (p["qh"].astype(F32) * p["egc"] - auw[:, GDN_HEAD_DIM:]).astype(BF16)
            egl_s[pl.ds(p["i"] * SUBLANES + hd, 1), :] = jnp.broadcast_to(jnp.exp(p["g_last"]), (1, LANES))
        return carry

    lax.fori_loop(0, g // unroll, intra, 0)

    def inter(i, carry):
        rows = pl.ds(pl.multiple_of(i * c, c), c)
        srows = pl.ds(pl.multiple_of(i * GDN_HEAD_DIM, GDN_HEAD_DIM), GDN_HEAD_DIM)
        ss = [s_scr[hd] for hd in heads]
        sbs = [s.astype(BF16) for s in ss]
        upd = [_dot(kw_s[srows, hcols[hd]], sbs[hd]) for hd in heads]
        for hd in heads:
            s_scr[hd] = ss[hd] * egl_s[pl.ds(i * SUBLANES + hd, 1), :] + (ku_s[srows, hcols[hd]] - upd[hd])
        os_ = [_dot(qe_s[rows, hcols[hd]], sbs[hd]) + au_s[rows, hcols[hd]] for hd in heads]
        for hd in heads:
            o = os_[hd]
            gate = _silu(z_ref[0, rows, hcols[hd]].astype(F32))
            on = o * lax.rsqrt(jnp.mean(o * o, axis=-1, keepdims=True) + RMS_EPS) * gnw * gate
            o_ref[0, rows, hcols[hd]] = on.astype(BF16)
        return carry

    lax.fori_loop(0, g, inter, 0)

    @pl.when(cg == pl.num_programs(1) - 1)
    def _():
        sout_ref[0] = s_scr[...]


def _gdn(q, k, v, z, bgc, bgr, s0, gnw, c, g):
    b, l, _ = q.shape
    ncg = l // (c * g)
    tok = lambda w: pl.BlockSpec((1, c * g, w), lambda i, j: (i, j, 0))
    st = pl.BlockSpec((1, GDN_HEADS, GDN_HEAD_DIM, GDN_HEAD_DIM), lambda i, j: (i, 0, 0, 0))
    tok_buf = lambda dt: pltpu.VMEM((c * g, GDN_WIDTH), dt)
    state_buf = lambda dt: pltpu.VMEM((g * GDN_HEAD_DIM, GDN_WIDTH), dt)
    return pl.pallas_call(
        functools.partial(_gdn_body, c=c, g=g, unroll=min(g, 4)),
        grid=(b, ncg),
        in_specs=[tok(GDN_WIDTH), tok(GDN_WIDTH), tok(GDN_WIDTH), tok(GDN_WIDTH), tok(LANES),
                  pl.BlockSpec((1, g, SUBLANES, c), lambda i, j: (i, j, 0, 0)), st,
                  pl.BlockSpec((1, GDN_HEAD_DIM), lambda i, j: (0, 0))],
        out_specs=[tok(GDN_WIDTH), st],
        out_shape=[jax.ShapeDtypeStruct((b, l, GDN_WIDTH), BF16), jax.ShapeDtypeStruct(s0.shape, F32)],
        scratch_shapes=[pltpu.VMEM((GDN_HEADS, GDN_HEAD_DIM, GDN_HEAD_DIM), F32),
                        state_buf(F32), state_buf(BF16), tok_buf(F32), tok_buf(BF16),
                        pltpu.VMEM((g * SUBLANES, LANES), F32)],
        compiler_params=_cparams(("arbitrary", "arbitrary")),
        name="gdn",
    )(q, k, v, z, bgc, bgr, s0, gnw)


HALF = D_MODEL // 2
HI_MASK = 0xFFFF0000


def _pack_bf16_pairs(x):
    bits = pltpu.bitcast(x.astype(BF16).astype(F32), U32)
    return (bits[:, :HALF] >> 16) | (bits[:, HALF:] & jnp.uint32(HI_MASK))


def _unpack_bf16_pairs(p):
    return pltpu.bitcast(p << 16, F32), pltpu.bitcast(p & jnp.uint32(HI_MASK), F32)


ROW_TILES = HALF // LANES


def _row_lines(r):
    return pl.ds(r * ROW_TILES, ROW_TILES)


def _store_rows(ref, lead, r0, packed):
    n = packed.shape[0]
    for q in range(ROW_TILES):
        ref[(*lead, pl.ds(r0 * ROW_TILES + q, n, stride=ROW_TILES), slice(None))] = packed[:, q * LANES:(q + 1) * LANES]


def _load_rows(ref, lead, r0, n):
    return jnp.concatenate(
        [ref[(*lead, pl.ds(r0 * ROW_TILES + q, n, stride=ROW_TILES), slice(None))] for q in range(ROW_TILES)], axis=1)


def _mid_body(o_ref, uc_ref, x_ref, gt1_ref, sh2_ref, sc2_ref, gt2_ref,
              wout_ref, g1_ref, b1_ref, wsgu_ref, wsd_ref, wrh_ref, wrl_ref, pre_ref, h2p_ref, lgt_ref):
    gt1, sh2, sc2, gt2 = gt1_ref[0], sh2_ref[0], sc2_ref[0], gt2_ref[0]
    mix = _dot(jnp.concatenate([o_ref[0], uc_ref[0]], axis=1), wout_ref[...])
    x1 = _layer_norm(ALPHA * x_ref[0] + (1.0 + gt1) * mix) * g1_ref[...] + b1_ref[...]
    h2 = _layer_norm(x1) * (1.0 + sc2) + sh2
    hh, hl = _split_bf16(h2)
    lgt_ref[...] = _dot_nt(wrh_ref[...], hh) + (_dot_nt(wrh_ref[...], hl) + _dot_nt(wrl_ref[...], hh))
    gu = _dot(hh, wsgu_ref[...])
    act = _silu(gu[:, :D_SHARED]) * gu[:, D_SHARED:]
    shared = _dot(act.astype(BF16), wsd_ref[...])
    pre_ref[0] = ALPHA * x1 + (1.0 + gt2) * shared
    _store_rows(h2p_ref, (0,), 0, _pack_bf16_pairs(h2))


def _mod_spec(m, tl):
    if m.shape[1] == 1:
        return pl.BlockSpec((1, 1, D_MODEL), lambda i, j: (i, 0, 0))
    return pl.BlockSpec((1, tl, D_MODEL), lambda i, j: (i, j, 0))


def _mid(o, uc, x, mods, prm, tl):
    b, l, _ = x.shape
    nt = l // tl
    full = lambda a: pl.BlockSpec(a.shape, lambda i, j: (0,) * a.ndim)
    consts = [prm["w_out"], prm["ln1_g"], prm["ln1_b"], prm["ws_gu"], prm["ws_down"], prm["wr_hi"], prm["wr_lo"]]
    return pl.pallas_call(
        _mid_body,
        grid=(b, nt),
        in_specs=[pl.BlockSpec((1, tl, GDN_WIDTH), lambda i, j: (i, j, 0)),
                  pl.BlockSpec((1, tl, CONV_WIDTH), lambda i, j: (i, j, 0)),
                  pl.BlockSpec((1, tl, D_MODEL), lambda i, j: (i, j, 0))]
                 + [_mod_spec(m, tl) for m in mods] + [full(a) for a in consts],
        out_specs=[pl.BlockSpec((1, tl, D_MODEL), lambda i, j: (i, j, 0)),
                   pl.BlockSpec((1, tl * ROW_TILES, LANES), lambda i, j: (i, j, 0)),
                   pl.BlockSpec((N_EXPERTS, tl), lambda i, j: (0, i * nt + j))],
        out_shape=[jax.ShapeDtypeStruct((b, l, D_MODEL), F32), jax.ShapeDtypeStruct((b, l * ROW_TILES, LANES), U32),
                   jax.ShapeDtypeStruct((N_EXPERTS, b * l), F32)],
        compiler_params=_cparams(("arbitrary", "arbitrary")),
        name="mixer_out",
    )(o, uc, x, *mods, *consts)


def _first_max(x, row, n):
    m = jnp.max(x, axis=0, keepdims=True)
    ix = jnp.min(jnp.where(x == m, row, float(n)), axis=0, keepdims=True)
    return m, ix


def _route_body(lg_ref, bias_ref, idx_ref, w_ref, rank_ref, cnt_ref, cnt_scr, *, tr):
    @pl.when(pl.program_id(0) == 0)
    def _():
        cnt_scr[...] = jnp.zeros_like(cnt_scr)

    neg = -jnp.inf
    scores = _sigmoid(lg_ref[...])
    sel = scores + bias_ref[:, 0:1]
    row_g = lax.broadcasted_iota(I32, (GROUP_SIZE, tr), 0).astype(F32)
    gs = []
    for g in range(N_GROUPS):
        blk = sel[g * GROUP_SIZE:(g + 1) * GROUP_SIZE, :]
        m1, i1 = _first_max(blk, row_g, GROUP_SIZE)
        m2 = jnp.max(jnp.where(row_g == i1, neg, blk), axis=0, keepdims=True)
        gs.append(m1 + m2)
    gs = jnp.concatenate(gs, axis=0)
    row_n = lax.broadcasted_iota(I32, (N_GROUPS, tr), 0).astype(F32)
    chosen = jnp.zeros((N_GROUPS, tr), F32)
    for _ in range(TOPK_GROUPS):
        _, ix = _first_max(gs, row_n, N_GROUPS)
        hit = row_n == ix
        chosen = jnp.where(hit, 1.0, chosen)
        gs = jnp.where(hit, neg, gs)
    selm = jnp.concatenate(
        [jnp.where(chosen[g:g + 1, :] > 0.5, sel[g * GROUP_SIZE:(g + 1) * GROUP_SIZE, :], neg) for g in range(N_GROUPS)],
        axis=0)
    row_e = lax.broadcasted_iota(I32, (N_EXPERTS, tr), 0).astype(F32)
    idxs, ws = [], []
    picked = jnp.zeros((N_EXPERTS, tr), F32)
    for _ in range(TOP_K):
        _, ix = _first_max(selm, row_e, N_EXPERTS)
        hit = row_e == ix
        ws.append(jnp.sum(jnp.where(hit, scores, 0.0), axis=0, keepdims=True))
        idxs.append(ix)
        selm = jnp.where(hit, neg, selm)
        picked = jnp.where(hit, 1.0, picked)
    wsum = ws[0]
    for wk in ws[1:]:
        wsum = wsum + wk
    idx_ref[...] = jnp.concatenate(idxs, axis=0).astype(I32)
    w_ref[...] = jnp.concatenate(ws, axis=0) / wsum * ROUTED_SCALE
    ti = lax.broadcasted_iota(I32, (tr, tr), 0)
    tj = lax.broadcasted_iota(I32, (tr, tr), 1)
    before = _dot(picked.astype(BF16), jnp.where(ti < tj, 1.0, 0.0).astype(BF16)) + cnt_scr[:, 0:1]
    rank_ref[...] = jnp.concatenate(
        [jnp.sum(jnp.where(row_e == ix, before, 0.0), axis=0, keepdims=True) for ix in idxs], axis=0).astype(I32)
    cnt_scr[...] = cnt_scr[...] + jnp.sum(picked, axis=1, keepdims=True)
    cnt_ref[...] = cnt_scr[...]


def _route(logits_t, bias_b, tr):
    t = logits_t.shape[1]
    kt = lambda dt: (pl.BlockSpec((TOP_K, tr), lambda i: (0, i)), jax.ShapeDtypeStruct((TOP_K, t), dt))
    outs = [kt(I32), kt(F32), kt(I32),
            (pl.BlockSpec((N_EXPERTS, LANES), lambda i: (0, 0)), jax.ShapeDtypeStruct((N_EXPERTS, LANES), F32))]
    return pl.pallas_call(
        functools.partial(_route_body, tr=tr),
        grid=(t // tr,),
        in_specs=[pl.BlockSpec((N_EXPERTS, tr), lambda i: (0, i)), pl.BlockSpec((N_EXPERTS, LANES), lambda i: (0, 0))],
        out_specs=[o[0] for o in outs],
        out_shape=[o[1] for o in outs],
        scratch_shapes=[pltpu.VMEM((N_EXPERTS, LANES), F32)],
        compiler_params=_cparams(("arbitrary",)),
        name="route",
    )(logits_t, bias_b)


def _pos_body(idx_ref, rank_ref, cnt_ref, pos_ref, *, tr):
    ei = lax.broadcasted_iota(I32, (N_EXPERTS, N_EXPERTS), 0)
    ej = lax.broadcasted_iota(I32, (N_EXPERTS, N_EXPERTS), 1)
    below = jnp.where(ej < ei, 1.0, 0.0).astype(BF16)
    start = sum(_dot(below, part) for part in _split3_bf16(cnt_ref[...]))[:, 0:1]
    row_e = lax.broadcasted_iota(I32, (N_EXPERTS, tr), 0)
    pos_ref[...] = jnp.concatenate(
        [jnp.sum(jnp.where(row_e == idx_ref[k:k + 1, :], start, 0.0), axis=0, keepdims=True) for k in range(TOP_K)],
        axis=0).astype(I32) + rank_ref[...]


def _positions(idx, rank, cnt, tr):
    t = idx.shape[1]
    kt = pl.BlockSpec((TOP_K, tr), lambda i: (0, i))
    return pl.pallas_call(
        functools.partial(_pos_body, tr=tr),
        grid=(t // tr,),
        in_specs=[kt, kt, pl.BlockSpec((N_EXPERTS, LANES), lambda i: (0, 0))],
        out_specs=kt,
        out_shape=jax.ShapeDtypeStruct((TOP_K, t), I32),
        compiler_params=_cparams(("arbitrary",)),
        name="positions",
    )(idx, rank, cnt)


def _dispatch_body(pos_ref, hp_ref, hs_ref, xs_ref, sem, *, tt, n_prompt_tiles):
    i = pl.program_id(0)

    def scatter(src_ref):
        def body(t, carry):
            for k in range(TOP_K):
                pltpu.make_async_copy(src_ref.at[_row_lines(t)], xs_ref.at[_row_lines(pos_ref[k, t])], sem).start(
                    priority=k % 2)
            return carry
        lax.fori_loop(0, tt, body, 0)
        for _ in range(TOP_K):
            pltpu.make_async_copy(src_ref, xs_ref.at[pl.ds(0, tt * ROW_TILES)], sem).wait()

    @pl.when(i < n_prompt_tiles)
    def _():
        scatter(hp_ref)

    @pl.when(i >= n_prompt_tiles)
    def _():
        scatter(hs_ref)


def _dispatch(pos, h_prompt, h_sample, tt):
    tp, ts = h_prompt.shape[0] // ROW_TILES, h_sample.shape[0] // ROW_TILES
    ntp, nts = tp // tt, ts // tt
    n_rows = TOP_K * (tp + ts)
    return pl.pallas_call(
        functools.partial(_dispatch_body, tt=tt, n_prompt_tiles=ntp),
        grid=(ntp + nts,),
        in_specs=[pl.BlockSpec((TOP_K, tt), lambda i: (0, i), memory_space=pltpu.SMEM),
                  pl.BlockSpec((tt * ROW_TILES, LANES), lambda i: (jnp.minimum(i, ntp - 1), 0)),
                  pl.BlockSpec((tt * ROW_TILES, LANES), lambda i: (jnp.maximum(i - ntp, 0), 0))],
        out_specs=pl.BlockSpec(memory_space=pl.ANY),
        out_shape=jax.ShapeDtypeStruct((n_rows * ROW_TILES, LANES), U32),
        scratch_shapes=[pltpu.SemaphoreType.DMA(())],
        compiler_params=_cparams(("arbitrary",)),
        name="dispatch",
    )(pos, h_prompt, h_sample)


GMM_ROWS = 1024
GMM_SUB = 256


def _gmm_body(ve_ref, vb_ref, vlo_ref, vhi_ref, vfirst_ref, vnew_ref,
              xs_ref, wg_ref, wu_ref, wd_ref, ys_ref, wgu_s, wd_s):
    v = pl.program_id(0)
    lo = vlo_ref[v]
    hi = vhi_ref[v]

    @pl.when(vnew_ref[v] == 1)
    def _():
        wgu_s[:, :D_EXPERT] = wg_ref[0].astype(BF16)
        wgu_s[:, D_EXPERT:] = wu_ref[0].astype(BF16)
        wd_s[...] = wd_ref[0].astype(BF16)

    @pl.when(vfirst_ref[v] == 1)
    def _():
        ys_ref[...] = jnp.zeros_like(ys_ref)

    def sub_block(s, carry):
        r0 = pl.multiple_of(s * GMM_SUB, GMM_SUB)
        x_lo, x_hi = _unpack_bf16_pairs(_load_rows(xs_ref, (), r0, GMM_SUB))
        gu = _dot(x_lo.astype(BF16), wgu_s[:HALF, :]) + _dot(x_hi.astype(BF16), wgu_s[HALF:, :])
        act = _silu(gu[:, :D_EXPERT]) * gu[:, D_EXPERT:]
        y = _pack_bf16_pairs(_dot(act.astype(BF16), wd_s[...]))
        row = lax.broadcasted_iota(I32, y.shape, 0) + r0
        _store_rows(ys_ref, (), r0, jnp.where((row >= lo) & (row < hi), y, _load_rows(ys_ref, (), r0, GMM_SUB)))
        return carry

    lax.fori_loop(lo // GMM_SUB, (hi + GMM_SUB - 1) // GMM_SUB, sub_block, 0)


def _gmm_schedule(counts, n_rows):
    nb = n_rows // GMM_ROWS
    n_vis = nb + N_EXPERTS
    ends = jnp.cumsum(counts)
    starts = ends - counts
    first_blk = starts // GMM_ROWS
    last_blk = jnp.maximum(ends - 1, 0) // GMM_ROWS
    per_e = jnp.where(counts > 0, last_blk - first_blk + 1, 0)
    vis_end = jnp.cumsum(per_e)
    total = vis_end[-1]
    v = jnp.minimum(jnp.arange(n_vis, dtype=I32), total - 1)
    e = jnp.minimum(jnp.sum((vis_end[None, :] <= v[:, None]).astype(I32), axis=1), N_EXPERTS - 1)
    table = jnp.stack([first_blk, vis_end - per_e, starts, ends], axis=1).astype(I32)
    pick = e[:, None] == jnp.arange(N_EXPERTS, dtype=I32)[None, :]
    fb, v0, st, en = jnp.moveaxis(jnp.sum(jnp.where(pick[:, :, None], table[None], 0), axis=1), 1, 0)
    blk = fb + (v - v0)
    lo = jnp.maximum(st, blk * GMM_ROWS) - blk * GMM_ROWS
    hi = jnp.minimum(en, (blk + 1) * GMM_ROWS) - blk * GMM_ROWS
    hi = jnp.where(jnp.arange(n_vis) < total, hi, lo)
    prev = lambda a: jnp.concatenate([jnp.full((1,), -1, I32), a[:-1]])
    first = (blk != prev(blk)).astype(I32)
    new_e = (e != prev(e)).astype(I32)
    return e, blk, lo.astype(I32), hi.astype(I32), first, new_e


def _gmm(xs, counts, we_gate, we_up, we_down):
    n_rows = xs.shape[0] // ROW_TILES
    sched = _gmm_schedule(counts, n_rows)
    n_vis = sched[0].shape[0]
    grid_spec = pltpu.PrefetchScalarGridSpec(
        num_scalar_prefetch=len(sched),
        grid=(n_vis,),
        in_specs=[pl.BlockSpec((GMM_ROWS * ROW_TILES, LANES), lambda v, ve, vb, *_: (vb[v], 0)),
                  pl.BlockSpec((1, D_MODEL, D_EXPERT), lambda v, ve, *_: (ve[v], 0, 0)),
                  pl.BlockSpec((1, D_MODEL, D_EXPERT), lambda v, ve, *_: (ve[v], 0, 0)),
                  pl.BlockSpec((1, D_EXPERT, D_MODEL), lambda v, ve, *_: (ve[v], 0, 0))],
        out_specs=pl.BlockSpec((GMM_ROWS * ROW_TILES, LANES), lambda v, ve, vb, *_: (vb[v], 0)),
        scratch_shapes=[pltpu.VMEM((D_MODEL, 2 * D_EXPERT), BF16), pltpu.VMEM((D_EXPERT, D_MODEL), BF16)])
    return pl.pallas_call(
        _gmm_body,
        grid_spec=grid_spec,
        out_shape=jax.ShapeDtypeStruct(xs.shape, U32),
        compiler_params=_cparams(("arbitrary",)),
        name="expert_ffn",
    )(*sched, xs, we_gate, we_up, we_down)


def _combine_body(pos_ref, w_ref, pre_ref, gt2_ref, g2_ref, b2_ref, ys_ref, out_ref, gbuf, sem, *, tt):
    def body(t, carry):
        for k in range(TOP_K):
            pltpu.make_async_copy(ys_ref.at[_row_lines(pos_ref[k, t])], gbuf.at[k, _row_lines(t)], sem).start(
                priority=k % 2)
        return carry
    lax.fori_loop(0, tt, body, 0)
    pltpu.make_async_copy(gbuf, gbuf, sem).wait()
    acc_lo = jnp.zeros((tt, HALF), F32)
    acc_hi = jnp.zeros((tt, HALF), F32)
    for k in range(TOP_K):
        y_lo, y_hi = _unpack_bf16_pairs(_load_rows(gbuf, (k,), 0, tt))
        wk = w_ref[:, k:k + 1]
        acc_lo = acc_lo + wk * y_lo
        acc_hi = acc_hi + wk * y_hi
    routed = jnp.concatenate([acc_lo, acc_hi], axis=1)
    y = pre_ref[0] + (1.0 + gt2_ref[0]) * routed
    out_ref[0] = _layer_norm(y) * g2_ref[...] + b2_ref[...]


def _combine(pos, w_tk, pre, gt2, ln2_g, ln2_b, ys, tt, tok0):
    b, l, _ = pre.shape
    nt = l // tt
    blk0 = tok0 // tt
    return pl.pallas_call(
        functools.partial(_combine_body, tt=tt),
        grid=(b, nt),
        in_specs=[pl.BlockSpec((TOP_K, tt), lambda i, j: (0, blk0 + i * nt + j), memory_space=pltpu.SMEM),
                  pl.BlockSpec((tt, TOP_K), lambda i, j: (blk0 + i * nt + j, 0)),
                  pl.BlockSpec((1, tt, D_MODEL), lambda i, j: (i, j, 0)),
                  _mod_spec(gt2, tt),
                  pl.BlockSpec((1, D_MODEL), lambda i, j: (0, 0)),
                  pl.BlockSpec((1, D_MODEL), lambda i, j: (0, 0)),
                  pl.BlockSpec(memory_space=pl.ANY)],
        out_specs=pl.BlockSpec((1, tt, D_MODEL), lambda i, j: (i, j, 0)),
        out_shape=jax.ShapeDtypeStruct((b, l, D_MODEL), F32),
        scratch_shapes=[pltpu.VMEM((TOP_K, tt * ROW_TILES, LANES), U32), pltpu.SemaphoreType.DMA(())],
        compiler_params=_cparams(("arbitrary", "arbitrary")),
        name="combine",
    )(pos, w_tk, pre, gt2, ln2_g, ln2_b, ys)


def _prep_params(w_in, conv_qkv_w, a_log, dt_bias, dw_w, dw_b, cn_g, cn_b):
    z0 = QKV_WIDTH
    b0 = z0 + GDN_WIDTH
    g0 = b0 + 2 * GDN_HEADS
    w_bg = w_in[:, b0:g0]
    w_in_r = jnp.concatenate(
        [w_in[:, :b0], w_in[:, g0:], w_bg, jnp.zeros((D_MODEL, LANES - 2 * GDN_HEADS), w_in.dtype)], axis=1).astype(BF16)
    pad_h = jnp.zeros((GDN_HEADS,), F32)
    al = jnp.concatenate([pad_h, a_log.astype(F32)])
    db = jnp.concatenate([pad_h, dt_bias.astype(F32)])
    gp_c = jnp.zeros((SUBLANES, LANES), F32).at[0, :2 * GDN_HEADS].set(al).at[1, :2 * GDN_HEADS].set(db)
    gp_r = jnp.zeros((SUBLANES, LANES), F32).at[:, 0].set(al).at[:, 1].set(db)
    return dict(
        w_in_r=w_in_r, w_bgt=w_bg.T.astype(BF16), conv_qkv_w=conv_qkv_w.astype(F32),
        dw_w=jnp.concatenate([dw_w, jnp.zeros((1, CONV_WIDTH), dw_w.dtype)], axis=0).astype(F32),
        dw_b=dw_b.reshape(1, -1).astype(F32), cn_g=cn_g.reshape(1, -1).astype(F32), cn_b=cn_b.reshape(1, -1).astype(F32),
        gp_c=gp_c, gp_r=gp_r)


def _prep_mid_params(w_out, ln1_g, ln1_b, w_router, router_bias, ws_gate, ws_up, ws_down, ln2_g, ln2_b, gdn_norm_w):
    row = lambda a: a.reshape(1, -1).astype(F32)
    wr_t = w_router.astype(F32).T
    wr_hi = wr_t.astype(BF16)
    return dict(
        w_out=w_out.astype(BF16), ln1_g=row(ln1_g), ln1_b=row(ln1_b),
        ws_gu=jnp.concatenate([ws_gate, ws_up], axis=1).astype(BF16), ws_down=ws_down.astype(BF16),
        wr_hi=wr_hi, wr_lo=(wr_t - wr_hi.astype(F32)).astype(BF16),
        bias_b=jnp.broadcast_to(router_bias.astype(F32)[:, None], (N_EXPERTS, LANES)),
        ln2_g=row(ln2_g), ln2_b=row(ln2_b), gnw=row(gdn_norm_w))


def _tile(n, pref):
    t = min(pref, n)
    while n % t:
        t //= 2
    return t


def _token_mixer(x, mod, s_gdn, s_qkv, s_dw, prm, mprm):
    b, l, _ = x.shape
    assert l >= DW_CONV - 1 and l % SUBLANES == 0
    tl = _tile(l, 256)
    sq_pad = jnp.pad(s_qkv.astype(F32), ((0, 0), (QKV_TAIL - (GDN_CONV - 1), 0), (0, 0)))
    sd_pad = jnp.pad(s_dw.astype(F32), ((0, 0), (DW_TAIL - (DW_CONV - 1), 0), (0, 0)))
    q, k, v, z, bgc, bgr, uc, nq, nd = _front(x, mod, sq_pad, sd_pad, prm, tl)
    c = min(CHUNK, l)
    n_chunks = l // c
    g = _tile(n_chunks, 8)
    bgr = bgr.reshape(b, SUBLANES, n_chunks, c).transpose(0, 2, 1, 3)
    o, s_new = _gdn(q, k, v, z, bgc, bgr, s_gdn.astype(F32), mprm["gnw"], c, g)
    return o, uc, s_new, nq[:, QKV_TAIL - (GDN_CONV - 1):], nd[:, DW_TAIL - (DW_CONV - 1):]


def kernel(x_prompt, x_sample, state_gdn, state_qkv_conv, state_dw_conv, c_prompt, c_sample, w_ada, b_ada, w_in, conv_qkv_w, a_log, dt_bias, gdn_norm_w, dw_w, dw_b, cn_g, cn_b, w_out, ln1_g, ln1_b, w_router, router_bias, we_gate, we_up, we_down, ws_gate, ws_up, ws_down, ln2_g, ln2_b):
    bp, lp, _ = x_prompt.shape
    bs, ls, _ = x_sample.shape
    tp, ts = bp * lp, bs * ls
    yp, ys = x_prompt, x_sample
    c_all = jnp.concatenate([c_prompt, c_sample], axis=0)
    new_p, new_s = [], []
    for l in range(w_ada.shape[0]):
        prm = _prep_params(w_in[l], conv_qkv_w[l], a_log[l], dt_bias[l], dw_w[l], dw_b[l], cn_g[l], cn_b[l])
        mprm = _prep_mid_params(w_out[l], ln1_g[l], ln1_b[l], w_router[l], router_bias[l], ws_gate[l], ws_up[l],
                                ws_down[l], ln2_g[l], ln2_b[l], gdn_norm_w[l])
        mod = _ada_mod(c_all, w_ada[l], b_ada[l]).reshape(bp + bs, 6, D_MODEL)
        mod_p, mod_s = mod[:bp], mod[bp:]

        zg = jnp.zeros((bp, GDN_HEADS, GDN_HEAD_DIM, GDN_HEAD_DIM), F32)
        zq = jnp.zeros((bp, GDN_CONV - 1, QKV_WIDTH), F32)
        zd = jnp.zeros((bp, DW_CONV - 1, CONV_WIDTH), F32)
        o_p, uc_p, g_p, q_p, d_p = _token_mixer(yp, mod_p, zg, zq, zd, prm, mprm)
        o_s, uc_s, g_s, q_s, d_s = _token_mixer(ys, mod_s, state_gdn[l], state_qkv_conv[l], state_dw_conv[l], prm, mprm)

        mods_p = tuple(mod_p[:, j:j + 1, :] for j in (2, 3, 4, 5))
        mods_s = tuple(jnp.repeat(mod_s[:, j, :], ls, axis=0)[None] for j in (2, 3, 4, 5))
        flat = lambda a: a.reshape(1, ts, a.shape[-1])
        pre_p, h_p, lg_p = _mid(o_p, uc_p, yp, mods_p, mprm, _tile(lp, 256))
        pre_s, h_s, lg_s = _mid(flat(o_s), flat(uc_s), flat(ys), mods_s, mprm, _tile(ts, 256))

        tt = _tile(ts, 512)
        assert tp % tt == 0 and lp % tt == 0 and (TOP_K * (tp + ts)) % GMM_ROWS == 0
        idx, w, rank, cnt = _route(jnp.concatenate([lg_p, lg_s], axis=1), mprm["bias_b"], tt)
        pos = _positions(idx, rank, cnt, tt)
        xs = _dispatch(pos, h_p.reshape(tp * ROW_TILES, LANES), h_s.reshape(ts * ROW_TILES, LANES), tt)
        ye = _gmm(xs, cnt[:, 0].astype(I32), we_gate[l], we_up[l], we_down[l])
        w_tk = w.T
        yp = _combine(pos, w_tk, pre_p, mods_p[3], mprm["ln2_g"], mprm["ln2_b"], ye, tt, 0)
        ys = _combine(pos, w_tk, pre_s, mods_s[3], mprm["ln2_g"], mprm["ln2_b"], ye, tt, tp).reshape(bs, ls, D_MODEL)
        new_p.append((g_p.astype(state_gdn.dtype), q_p.astype(x_prompt.dtype), d_p.astype(x_prompt.dtype)))
        new_s.append((g_s.astype(state_gdn.dtype), q_s.astype(state_qkv_conv.dtype), d_s.astype(state_dw_conv.dtype)))
    stack = lambda rows, j: jnp.stack([r[j] for r in rows])
    return (yp, ys, stack(new_p, 0), stack(new_p, 1), stack(new_p, 2), stack(new_s, 0), stack(new_s, 1), stack(new_s, 2))
```

```python
import functools

import jax
import jax.numpy as jnp
from jax import lax
from jax.experimental import pallas as pl
from jax.experimental.pallas import tpu as pltpu

F32 = jnp.float32
BF16 = jnp.bfloat16
I32 = jnp.int32
U32 = jnp.uint32

D_MODEL = 1024
GDN_WIDTH = 512
CONV_WIDTH = 512
GDN_HEAD_DIM = 128
GDN_HEADS = 4
QKV_WIDTH = 3 * GDN_WIDTH
GDN_CONV = 4
DW_CONV = 31
CHUNK = 64
N_EXPERTS = 256
TOP_K = 8
N_GROUPS = 8
GROUP_SIZE = N_EXPERTS // N_GROUPS
TOPK_GROUPS = 4
D_EXPERT = 256
D_SHARED = 256
ROUTED_SCALE = 2.5
LN_EPS = 1e-5
RMS_EPS = 1e-6
L2_EPS = 1e-6
DEPTH = 1
ALPHA = (2.0 * DEPTH) ** 0.25

LANES = 128
SUBLANES = 8
VMEM_LIMIT_BYTES = 56 * 1024 * 1024

COL_Z = QKV_WIDTH
COL_GV = COL_Z + GDN_WIDTH
COL_GG = COL_GV + CONV_WIDTH
COL_BG = COL_GG + CONV_WIDTH
D_PROJ_PAD = COL_BG + LANES
QKV_TAIL = SUBLANES
DW_TAIL = 32
NEG_BIG = -1e30


def _cparams(sem):
    return pltpu.CompilerParams(dimension_semantics=sem, vmem_limit_bytes=VMEM_LIMIT_BYTES)


def _split_bf16(x):
    hi = x.astype(BF16)
    lo = (x - hi.astype(F32)).astype(BF16)
    return hi, lo


def _dot(a, b):
    return jnp.dot(a, b, preferred_element_type=F32)


def _dot_nt(a, b):
    return lax.dot_general(a, b, (((1,), (1,)), ((), ())), preferred_element_type=F32)


def _dot_tn(a, b):
    return lax.dot_general(a, b, (((0,), (0,)), ((), ())), preferred_element_type=F32)


def _dot_hp(a, b):
    ah, al = _split_bf16(a)
    bh, bl = _split_bf16(b)
    return _dot(ah, bh) + (_dot(ah, bl) + _dot(al, bh))


def _sigmoid(x):
    return 1.0 / (1.0 + jnp.exp(-x))


def _silu(x):
    return x * _sigmoid(x)


def _softplus(x):
    return jnp.maximum(x, 0.0) + jnp.log(1.0 + jnp.exp(-jnp.abs(x)))


def _layer_norm(x):
    mu = jnp.mean(x, axis=-1, keepdims=True)
    xc = x - mu
    var = jnp.mean(xc * xc, axis=-1, keepdims=True)
    return xc * lax.rsqrt(var + LN_EPS)


def _ada_body(c_ref, w_ref, b_ref, o_ref):
    o_ref[...] = _dot_hp(_silu(c_ref[...]), w_ref[...]) + b_ref[...]


def _ada_mod(c, w_ada, b_ada):
    bt = c.shape[0]
    n_col = w_ada.shape[1] // D_MODEL
    return pl.pallas_call(
        _ada_body,
        grid=(n_col,),
        in_specs=[
            pl.BlockSpec((bt, D_MODEL), lambda j: (0, 0)),
            pl.BlockSpec((D_MODEL, D_MODEL), lambda j: (0, j)),
            pl.BlockSpec((1, D_MODEL), lambda j: (0, j)),
        ],
        out_specs=pl.BlockSpec((bt, D_MODEL), lambda j: (0, j)),
        out_shape=jax.ShapeDtypeStruct((bt, w_ada.shape[1]), F32),
        compiler_params=_cparams(("arbitrary",)),
        name="ada_mod",
    )(c, w_ada, b_ada.reshape(1, -1))


def _front_body(x_ref, mod_ref, win_ref, wbgt_ref, cw_ref, dww_ref, dwb_ref, cng_ref, cnb_ref,
                gpc_ref, gpr_ref, sq_ref, sd_ref,
                q_ref, k_ref, v_ref, z_ref, bgc_ref, bgr_ref, uc_ref, nq_ref, nd_ref,
                qkv_buf, u_buf, *, tl):
    t = pl.program_id(1)
    rq = QKV_TAIL + tl
    ru = DW_TAIL + tl
    n_qkv = QKV_WIDTH // LANES
    n_u = CONV_WIDTH // LANES
    lanes = lambda c: slice(c * LANES, (c + 1) * LANES)

    @pl.when(t == 0)
    def _():
        for c in range(n_qkv):
            qkv_buf[c * rq:c * rq + QKV_TAIL, :] = sq_ref[0, :, lanes(c)]
        for c in range(n_u):
            u_buf[c * ru:c * ru + DW_TAIL, :] = sd_ref[0, :, lanes(c)]

    sh1 = mod_ref[0, 0:1, :]
    sc1 = mod_ref[0, 1:2, :]
    h = _layer_norm(x_ref[0]) * (1.0 + sc1) + sh1
    hb = h.astype(BF16)

    qkv = _dot(hb, win_ref[:, 0:QKV_WIDTH])
    for c in range(n_qkv):
        qkv_buf[c * rq + QKV_TAIL:(c + 1) * rq, :] = qkv[:, lanes(c)]
    off_q = QKV_TAIL - (GDN_CONV - 1)
    for c in range(n_qkv):
        acc = cw_ref[0:1, lanes(c)] * qkv_buf[c * rq + off_q:c * rq + off_q + tl, :]
        for j in range(1, GDN_CONV):
            acc = acc + cw_ref[j:j + 1, lanes(c)] * qkv_buf[c * rq + off_q + j:c * rq + off_q + j + tl, :]
        a = _silu(acc)
        hd = c % GDN_HEADS
        if c < GDN_HEADS:
            a = a * lax.rsqrt(jnp.sum(a * a, axis=-1, keepdims=True) + L2_EPS) * (GDN_HEAD_DIM ** -0.5)
            q_ref[0, :, lanes(hd)] = a.astype(BF16)
        elif c < 2 * GDN_HEADS:
            a = a * lax.rsqrt(jnp.sum(a * a, axis=-1, keepdims=True) + L2_EPS)
            k_ref[0, :, lanes(hd)] = a.astype(BF16)
        else:
            v_ref[0, :, lanes(hd)] = a.astype(BF16)
    z_ref[0] = _dot(hb, win_ref[:, COL_Z:COL_GV]).astype(BF16)

    raw_c = _dot(hb, win_ref[:, COL_BG:D_PROJ_PAD])
    lane = lax.broadcasted_iota(I32, raw_c.shape, 1)
    neg_a_c = -jnp.exp(gpc_ref[0:1, :])
    g_c = neg_a_c * _softplus(raw_c + gpc_ref[1:2, :])
    bgc_ref[0] = jnp.where(lane < GDN_HEADS, _sigmoid(raw_c), g_c)
    raw_r = _dot_nt(wbgt_ref[...], hb)
    row = lax.broadcasted_iota(I32, raw_r.shape, 0)
    neg_a_r = -jnp.exp(gpr_ref[:, 0:1])
    g_r = neg_a_r * _softplus(raw_r + gpr_ref[:, 1:2])
    bgr_ref[0] = jnp.where(row < GDN_HEADS, _sigmoid(raw_r), g_r)

    gv = _dot(hb, win_ref[:, COL_GV:COL_GG])
    gg = _dot(hb, win_ref[:, COL_GG:COL_BG])
    u = gv * _sigmoid(gg)
    for c in range(n_u):
        u_buf[c * ru + DW_TAIL:(c + 1) * ru, :] = u[:, lanes(c)]
    off_u = DW_TAIL - (DW_CONV - 1)
    daccs = []
    for c in range(n_u):
        dacc = dwb_ref[:, lanes(c)] + dww_ref[0:1, lanes(c)] * u_buf[c * ru + off_u:c * ru + off_u + tl, :]
        for j in range(1, DW_CONV):
            dacc = dacc + dww_ref[j:j + 1, lanes(c)] * u_buf[c * ru + off_u + j:c * ru + off_u + j + tl, :]
        daccs.append(dacc)
    dacc = jnp.concatenate(daccs, axis=1)
    uc_ref[0] = _silu(_layer_norm(dacc) * cng_ref[...] + cnb_ref[...]).astype(BF16)

    for c in range(n_qkv):
        nq = qkv_buf[c * rq + tl:(c + 1) * rq, :]
        qkv_buf[c * rq:c * rq + QKV_TAIL, :] = nq
        nq_ref[0, :, lanes(c)] = nq
    for c in range(n_u):
        nd = u_buf[c * ru + tl:(c + 1) * ru, :]
        u_buf[c * ru:c * ru + DW_TAIL, :] = nd
        nd_ref[0, :, lanes(c)] = nd


def _front(x, mod, sq_pad, sd_pad, prm, tl):
    b, l, _ = x.shape
    nt = l // tl
    tok = lambda w, dt: (pl.BlockSpec((1, tl, w), lambda i, j: (i, j, 0)), jax.ShapeDtypeStruct((b, l, w), dt))
    full = lambda a: pl.BlockSpec(a.shape, lambda i, j: (0,) * a.ndim)
    outs = [tok(GDN_WIDTH, BF16), tok(GDN_WIDTH, BF16), tok(GDN_WIDTH, BF16), tok(GDN_WIDTH, BF16),
            tok(LANES, F32),
            (pl.BlockSpec((1, SUBLANES, tl), lambda i, j: (i, 0, j)), jax.ShapeDtypeStruct((b, SUBLANES, l), F32)),
            tok(CONV_WIDTH, BF16),
            (pl.BlockSpec((1, QKV_TAIL, QKV_WIDTH), lambda i, j: (i, 0, 0)),
             jax.ShapeDtypeStruct((b, QKV_TAIL, QKV_WIDTH), F32)),
            (pl.BlockSpec((1, DW_TAIL, CONV_WIDTH), lambda i, j: (i, 0, 0)),
             jax.ShapeDtypeStruct((b, DW_TAIL, CONV_WIDTH), F32))]
    consts = [prm["w_in_r"], prm["w_bgt"], prm["conv_qkv_w"], prm["dw_w"], prm["dw_b"], prm["cn_g"], prm["cn_b"],
              prm["gp_c"], prm["gp_r"]]
    return pl.pallas_call(
        functools.partial(_front_body, tl=tl),
        grid=(b, nt),
        in_specs=[pl.BlockSpec((1, tl, D_MODEL), lambda i, j: (i, j, 0)),
                  pl.BlockSpec((1, 6, D_MODEL), lambda i, j: (i, 0, 0))]
                 + [full(a) for a in consts]
                 + [pl.BlockSpec((1, QKV_TAIL, QKV_WIDTH), lambda i, j: (i, 0, 0)),
                    pl.BlockSpec((1, DW_TAIL, CONV_WIDTH), lambda i, j: (i, 0, 0))],
        out_specs=[o[0] for o in outs],
        out_shape=[o[1] for o in outs],
        scratch_shapes=[pltpu.VMEM((QKV_WIDTH // LANES * (QKV_TAIL + tl), LANES), F32),
                        pltpu.VMEM((CONV_WIDTH // LANES * (DW_TAIL + tl), LANES), F32)],
        compiler_params=_cparams(("arbitrary", "arbitrary")),
        name="mixer_front",
    )(x, mod, *consts, sq_pad, sd_pad)


def _split3_bf16(x):
    hi = x.astype(BF16)
    r1 = x - hi.astype(F32)
    mid = r1.astype(BF16)
    lo = (r1 - mid.astype(F32)).astype(BF16)
    return hi, mid, lo


def _tri_inverse(ms, ri, ci):
    c = ms[0].shape[0]
    eye = jnp.where(ri == ci, 1.0, 0.0)
    pair = (ri >> 1) == (ci >> 1)
    ps = [eye - jnp.where(pair, m, 0.0) for m in ms]
    w = 2
    while w < c:
        s = w.bit_length() - 1
        sel = ((ri >> (s + 1)) == (ci >> (s + 1))) & ((ri >> s) > (ci >> s))
        pbs = [p.astype(BF16) for p in ps]
        xs = [_dot(pb, jnp.where(sel, m, 0.0).astype(BF16)).astype(BF16) for pb, m in zip(pbs, ms)]
        ps = [p - _dot(x, pb) for p, x, pb in zip(ps, xs, pbs)]
        w *= 2
    return ps


def _gdn_body(q_ref, k_ref, v_ref, z_ref, bgc_ref, bgr_ref, s0_ref, gnw_ref, o_ref, sout_ref,
              s_scr, ku_s, kw_s, au_s, qe_s, egl_s, *, c, g, unroll):
    cg = pl.program_id(1)

    @pl.when(cg == 0)
    def _():
        s_scr[...] = s0_ref[0]

    ri = lax.broadcasted_iota(I32, (c, c), 0)
    ci = lax.broadcasted_iota(I32, (c, c), 1)
    causal = ri >= ci
    strict = ri > ci
    lower = jnp.where(causal, 1.0, 0.0).astype(BF16)
    upper = jnp.where(ri <= ci, 1.0, 0.0).astype(BF16)
    gnw = gnw_ref[...]

    heads = range(GDN_HEADS)
    hcols = [slice(hd * GDN_HEAD_DIM, (hd + 1) * GDN_HEAD_DIM) for hd in heads]

    def intra(it, carry):
        prob = []
        for j in range(unroll):
            i = it * unroll + j
            rows = pl.ds(pl.multiple_of(i * c, c), c)
            bgc = bgc_ref[0, rows, :]
            bgr = bgr_ref[0, i]
            gc_c = sum(_dot(lower, part) for part in _split3_bf16(bgc))
            gc_r = sum(_dot(part, upper) for part in _split3_bf16(bgr))
            for hd in heads:
                gcc = gc_c[:, GDN_HEADS + hd:GDN_HEADS + hd + 1]
                gcr = gc_r[GDN_HEADS + hd:GDN_HEADS + hd + 1, :]
                prob.append(dict(i=i, hd=hd, rows=rows, cols=hcols[hd], beta=bgc[:, hd:hd + 1], gcc=gcc,
                                 decay=jnp.exp(jnp.where(causal, gcc - gcr, NEG_BIG))))
        for p in prob:
            p["kh"] = k_ref[0, p["rows"], p["cols"]]
            p["kb"] = p["kh"].astype(F32) * p["beta"]
        ms = [jnp.where(strict, _dot_nt(p["kb"].astype(BF16), p["kh"]) * p["decay"], 0.0) for p in prob]
        t_invs = _tri_inverse(ms, ri, ci)
        uws = []
        for p, t_inv in zip(prob, t_invs):
            p["egc"] = jnp.exp(p["gcc"])
            vf = v_ref[0, p["rows"], p["cols"]].astype(F32)
            rhs = jnp.concatenate([vf * p["beta"], p["kb"] * p["egc"]], axis=1).astype(BF16)
            uws.append(_dot(t_inv.astype(BF16), rhs).astype(BF16))
        for p in prob:
            p["qh"] = q_ref[0, p["rows"], p["cols"]]
            p["a"] = jnp.where(causal, _dot_nt(p["qh"], p["kh"]) * p["decay"], 0.0).astype(BF16)
            p["g_last"] = p["gcc"][c - 1:c, :]
            p["kd"] = (p["kh"].astype(F32) * jnp.exp(p["g_last"] - p["gcc"])).astype(BF16)
        kuws = [_dot_tn(p["kd"], uw) for p, uw in zip(prob, uws)]
        auws = [_dot(p["a"], uw) for p, uw in zip(prob, uws)]
        for p, kuw, auw in zip(prob, kuws, auws):
            rows, cols, hd = p["rows"], p["cols"], p["hd"]
            srows = pl.ds(pl.multiple_of(p["i"] * GDN_HEAD_DIM, GDN_HEAD_DIM), GDN_HEAD_DIM)
            ku_s[srows, cols] = kuw[:, :GDN_HEAD_DIM]
            kw_s[srows, cols] = kuw[:, GDN_HEAD_DIM:].astype(BF16)
            au_s[rows, cols] = auw[:, :GDN_HEAD_DIM]
            qe_s[rows, cols] = (p["qh"].astype(F32) * p["egc"] - auw[:, GDN_HEAD_DIM:]).astype(BF16)
            egl_s[pl.ds(p["i"] * SUBLANES + hd, 1), :] = jnp.broadcast_to(jnp.exp(p["g_last"]), (1, LANES))
        return carry

    lax.fori_loop(0, g // unroll, intra, 0)

    def inter(i, carry):
        rows = pl.ds(pl.multiple_of(i * c, c), c)
        srows = pl.ds(pl.multiple_of(i * GDN_HEAD_DIM, GDN_HEAD_DIM), GDN_HEAD_DIM)
        ss = [s_scr[hd] for hd in heads]
        sbs = [s.astype(BF16) for s in ss]
        upd = [_dot(kw_s[srows, hcols[hd]], sbs[hd]) for hd in heads]
        for hd in heads:
            s_scr[hd] = ss[hd] * egl_s[pl.ds(i * SUBLANES + hd, 1), :] + (ku_s[srows, hcols[hd]] - upd[hd])
        os_ = [_dot(qe_s[rows, hcols[hd]], sbs[hd]) + au_s[rows, hcols[hd]] for hd in heads]
        for hd in heads:
            o = os_[hd]
            gate = _silu(z_ref[0, rows, hcols[hd]].astype(F32))
            on = o * lax.rsqrt(jnp.mean(o * o, axis=-1, keepdims=True) + RMS_EPS) * gnw * gate
            o_ref[0, rows, hcols[hd]] = on.astype(BF16)
        return carry

    lax.fori_loop(0, g, inter, 0)

    @pl.when(cg == pl.num_programs(1) - 1)
    def _():
        sout_ref[0] = s_scr[...]


def _gdn(q, k, v, z, bgc, bgr, s0, gnw, c, g):
    b, l, _ = q.shape
    ncg = l // (c * g)
    tok = lambda w: pl.BlockSpec((1, c * g, w), lambda i, j: (i, j, 0))
    st = pl.BlockSpec((1, GDN_HEADS, GDN_HEAD_DIM, GDN_HEAD_DIM), lambda i, j: (i, 0, 0, 0))
    tok_buf = lambda dt: pltpu.VMEM((c * g, GDN_WIDTH), dt)
    state_buf = lambda dt: pltpu.VMEM((g * GDN_HEAD_DIM, GDN_WIDTH), dt)
    return pl.pallas_call(
        functools.partial(_gdn_body, c=c, g=g, unroll=min(g, 8)),
        grid=(b, ncg),
        in_specs=[tok(GDN_WIDTH), tok(GDN_WIDTH), tok(GDN_WIDTH), tok(GDN_WIDTH), tok(LANES),
                  pl.BlockSpec((1, g, SUBLANES, c), lambda i, j: (i, j, 0, 0)), st,
                  pl.BlockSpec((1, GDN_HEAD_DIM), lambda i, j: (0, 0))],
        out_specs=[tok(GDN_WIDTH), st],
        out_shape=[jax.ShapeDtypeStruct((b, l, GDN_WIDTH), BF16), jax.ShapeDtypeStruct(s0.shape, F32)],
        scratch_shapes=[pltpu.VMEM((GDN_HEADS, GDN_HEAD_DIM, GDN_HEAD_DIM), F32),
                        state_buf(F32), state_buf(BF16), tok_buf(F32), tok_buf(BF16),
                        pltpu.VMEM((g * SUBLANES, LANES), F32)],
        compiler_params=_cparams(("arbitrary", "arbitrary")),
        name="gdn",
    )(q, k, v, z, bgc, bgr, s0, gnw)


HALF = D_MODEL // 2
HI_MASK = 0xFFFF0000


def _pack_bf16_pairs(x):
    bits = pltpu.bitcast(x.astype(BF16).astype(F32), U32)
    return (bits[:, :HALF] >> 16) | (bits[:, HALF:] & jnp.uint32(HI_MASK))


def _unpack_bf16_pairs(p):
    return pltpu.bitcast(p << 16, F32), pltpu.bitcast(p & jnp.uint32(HI_MASK), F32)


ROW_TILES = HALF // LANES


def _row_lines(r):
    return pl.ds(r * ROW_TILES, ROW_TILES)


def _store_rows(ref, lead, r0, packed):
    n = packed.shape[0]
    for q in range(ROW_TILES):
        ref[(*lead, pl.ds(r0 * ROW_TILES + q, n, stride=ROW_TILES), slice(None))] = packed[:, q * LANES:(q + 1) * LANES]


def _load_rows(ref, lead, r0, n):
    return jnp.concatenate(
        [ref[(*lead, pl.ds(r0 * ROW_TILES + q, n, stride=ROW_TILES), slice(None))] for q in range(ROW_TILES)], axis=1)


def _mid_body(o_ref, uc_ref, x_ref, gt1_ref, sh2_ref, sc2_ref, gt2_ref,
              wout_ref, g1_ref, b1_ref, wsgu_ref, wsd_ref, wrh_ref, wrl_ref, pre_ref, h2p_ref, lgt_ref):
    gt1, sh2, sc2, gt2 = gt1_ref[0], sh2_ref[0], sc2_ref[0], gt2_ref[0]
    mix = _dot(jnp.concatenate([o_ref[0], uc_ref[0]], axis=1), wout_ref[...])
    x1 = _layer_norm(ALPHA * x_ref[0] + (1.0 + gt1) * mix) * g1_ref[...] + b1_ref[...]
    h2 = _layer_norm(x1) * (1.0 + sc2) + sh2
    hh, hl = _split_bf16(h2)
    lgt_ref[...] = _dot_nt(wrh_ref[...], hh) + (_dot_nt(wrh_ref[...], hl) + _dot_nt(wrl_ref[...], hh))
    gu = _dot(hh, wsgu_ref[...])
    act = _silu(gu[:, :D_SHARED]) * gu[:, D_SHARED:]
    shared = _dot(act.astype(BF16), wsd_ref[...])
    pre_ref[0] = ALPHA * x1 + (1.0 + gt2) * shared
    _store_rows(h2p_ref, (0,), 0, _pack_bf16_pairs(h2))


def _mod_spec(m, tl):
    if m.shape[1] == 1:
        return pl.BlockSpec((1, 1, D_MODEL), lambda i, j: (i, 0, 0))
    return pl.BlockSpec((1, tl, D_MODEL), lambda i, j: (i, j, 0))


def _mid(o, uc, x, mods, prm, tl):
    b, l, _ = x.shape
    nt = l // tl
    full = lambda a: pl.BlockSpec(a.shape, lambda i, j: (0,) * a.ndim)
    consts = [prm["w_out"], prm["ln1_g"], prm["ln1_b"], prm["ws_gu"], prm["ws_down"], prm["wr_hi"], prm["wr_lo"]]
    return pl.pallas_call(
        _mid_body,
        grid=(b, nt),
        in_specs=[pl.BlockSpec((1, tl, GDN_WIDTH), lambda i, j: (i, j, 0)),
                  pl.BlockSpec((1, tl, CONV_WIDTH), lambda i, j: (i, j, 0)),
                  pl.BlockSpec((1, tl, D_MODEL), lambda i, j: (i, j, 0))]
                 + [_mod_spec(m, tl) for m in mods] + [full(a) for a in consts],
        out_specs=[pl.BlockSpec((1, tl, D_MODEL), lambda i, j: (i, j, 0)),
                   pl.BlockSpec((1, tl * ROW_TILES, LANES), lambda i, j: (i, j, 0)),
                   pl.BlockSpec((N_EXPERTS, tl), lambda i, j: (0, i * nt + j))],
        out_shape=[jax.ShapeDtypeStruct((b, l, D_MODEL), F32), jax.ShapeDtypeStruct((b, l * ROW_TILES, LANES), U32),
                   jax.ShapeDtypeStruct((N_EXPERTS, b * l), F32)],
        compiler_params=_cparams(("arbitrary", "arbitrary")),
        name="mixer_out",
    )(o, uc, x, *mods, *consts)


def _first_max(x, row, n):
    m = jnp.max(x, axis=0, keepdims=True)
    ix = jnp.min(jnp.where(x == m, row, float(n)), axis=0, keepdims=True)
    return m, ix


def _route_body(lg_ref, bias_ref, idx_ref, w_ref, rank_ref, cnt_ref, cnt_scr, *, tr):
    @pl.when(pl.program_id(0) == 0)
    def _():
        cnt_scr[...] = jnp.zeros_like(cnt_scr)

    neg = -jnp.inf
    scores = _sigmoid(lg_ref[...])
    sel = scores + bias_ref[:, 0:1]
    row_g = lax.broadcasted_iota(I32, (GROUP_SIZE, tr), 0).astype(F32)
    gs = []
    for g in range(N_GROUPS):
        blk = sel[g * GROUP_SIZE:(g + 1) * GROUP_SIZE, :]
        m1, i1 = _first_max(blk, row_g, GROUP_SIZE)
        m2 = jnp.max(jnp.where(row_g == i1, neg, blk), axis=0, keepdims=True)
        gs.append(m1 + m2)
    gs = jnp.concatenate(gs, axis=0)
    row_n = lax.broadcasted_iota(I32, (N_GROUPS, tr), 0).astype(F32)
    chosen = jnp.zeros((N_GROUPS, tr), F32)
    for _ in range(TOPK_GROUPS):
        _, ix = _first_max(gs, row_n, N_GROUPS)
        hit = row_n == ix
        chosen = jnp.where(hit, 1.0, chosen)
        gs = jnp.where(hit, neg, gs)
    selm = jnp.concatenate(
        [jnp.where(chosen[g:g + 1, :] > 0.5, sel[g * GROUP_SIZE:(g + 1) * GROUP_SIZE, :], neg) for g in range(N_GROUPS)],
        axis=0)
    row_e = lax.broadcasted_iota(I32, (N_EXPERTS, tr), 0).astype(F32)
    idxs, ws = [], []
    picked = jnp.zeros((N_EXPERTS, tr), F32)
    for _ in range(TOP_K):
        _, ix = _first_max(selm, row_e, N_EXPERTS)
        hit = row_e == ix
        ws.append(jnp.sum(jnp.where(hit, scores, 0.0), axis=0, keepdims=True))
        idxs.append(ix)
        selm = jnp.where(hit, neg, selm)
        picked = jnp.where(hit, 1.0, picked)
    wsum = ws[0]
    for wk in ws[1:]:
        wsum = wsum + wk
    idx_ref[...] = jnp.concatenate(idxs, axis=0).astype(I32)
    w_ref[...] = jnp.concatenate(ws, axis=0) / wsum * ROUTED_SCALE
    ti = lax.broadcasted_iota(I32, (tr, tr), 0)
    tj = lax.broadcasted_iota(I32, (tr, tr), 1)
    before = _dot(picked.astype(BF16), jnp.where(ti < tj, 1.0, 0.0).astype(BF16)) + cnt_scr[:, 0:1]
    rank_ref[...] = jnp.concatenate(
        [jnp.sum(jnp.where(row_e == ix, before, 0.0), axis=0, keepdims=True) for ix in idxs], axis=0).astype(I32)
    cnt_scr[...] = cnt_scr[...] + jnp.sum(picked, axis=1, keepdims=True)
    cnt_ref[...] = cnt_scr[...]


def _route(logits_t, bias_b, tr):
    t = logits_t.shape[1]
    kt = lambda dt: (pl.BlockSpec((TOP_K, tr), lambda i: (0, i)), jax.ShapeDtypeStruct((TOP_K, t), dt))
    outs = [kt(I32), kt(F32), kt(I32),
            (pl.BlockSpec((N_EXPERTS, LANES), lambda i: (0, 0)), jax.ShapeDtypeStruct((N_EXPERTS, LANES), F32))]
    return pl.pallas_call(
        functools.partial(_route_body, tr=tr),
        grid=(t // tr,),
        in_specs=[pl.BlockSpec((N_EXPERTS, tr), lambda i: (0, i)), pl.BlockSpec((N_EXPERTS, LANES), lambda i: (0, 0))],
        out_specs=[o[0] for o in outs],
        out_shape=[o[1] for o in outs],
        scratch_shapes=[pltpu.VMEM((N_EXPERTS, LANES), F32)],
        compiler_params=_cparams(("arbitrary",)),
        name="route",
    )(logits_t, bias_b)


def _pos_body(idx_ref, rank_ref, cnt_ref, pos_ref, *, tr):
    ei = lax.broadcasted_iota(I32, (N_EXPERTS, N_EXPERTS), 0)
    ej = lax.broadcasted_iota(I32, (N_EXPERTS, N_EXPERTS), 1)
    below = jnp.where(ej < ei, 1.0, 0.0).astype(BF16)
    start = sum(_dot(below, part) for part in _split3_bf16(cnt_ref[...]))[:, 0:1]
    row_e = lax.broadcasted_iota(I32, (N_EXPERTS, tr), 0)
    pos_ref[...] = jnp.concatenate(
        [jnp.sum(jnp.where(row_e == idx_ref[k:k + 1, :], start, 0.0), axis=0, keepdims=True) for k in range(TOP_K)],
        axis=0).astype(I32) + rank_ref[...]


def _positions(idx, rank, cnt, tr):
    t = idx.shape[1]
    kt = pl.BlockSpec((TOP_K, tr), lambda i: (0, i))
    return pl.pallas_call(
        functools.partial(_pos_body, tr=tr),
        grid=(t // tr,),
        in_specs=[kt, kt, pl.BlockSpec((N_EXPERTS, LANES), lambda i: (0, 0))],
        out_specs=kt,
        out_shape=jax.ShapeDtypeStruct((TOP_K, t), I32),
        compiler_params=_cparams(("arbitrary",)),
        name="positions",
    )(idx, rank, cnt)


def _dispatch_body(pos_ref, hp_ref, hs_ref, xs_ref, sem, *, tt, n_prompt_tiles):
    i = pl.program_id(0)

    def scatter(src_ref):
        def body(t, carry):
            for k in range(TOP_K):
                pltpu.make_async_copy(src_ref.at[_row_lines(t)], xs_ref.at[_row_lines(pos_ref[0, 0, t * TOP_K + k])],
                                      sem).start(priority=k % 2)
            return carry
        lax.fori_loop(0, tt, body, 0)
        for _ in range(TOP_K):
            pltpu.make_async_copy(src_ref, xs_ref.at[pl.ds(0, tt * ROW_TILES)], sem).wait()

    @pl.when(i < n_prompt_tiles)
    def _():
        scatter(hp_ref)

    @pl.when(i >= n_prompt_tiles)
    def _():
        scatter(hs_ref)


def _dispatch(pos_tiles, h_prompt, h_sample, tt):
    tp, ts = h_prompt.shape[0] // ROW_TILES, h_sample.shape[0] // ROW_TILES
    ntp, nts = tp // tt, ts // tt
    n_rows = TOP_K * (tp + ts)
    return pl.pallas_call(
        functools.partial(_dispatch_body, tt=tt, n_prompt_tiles=ntp),
        grid=(ntp + nts,),
        in_specs=[pl.BlockSpec((1, 1, tt * TOP_K), lambda i: (i, 0, 0), memory_space=pltpu.SMEM),
                  pl.BlockSpec((tt * ROW_TILES, LANES), lambda i: (jnp.minimum(i, ntp - 1), 0)),
                  pl.BlockSpec((tt * ROW_TILES, LANES), lambda i: (jnp.maximum(i - ntp, 0), 0))],
        out_specs=pl.BlockSpec(memory_space=pl.ANY),
        out_shape=jax.ShapeDtypeStruct((n_rows * ROW_TILES, LANES), U32),
        scratch_shapes=[pltpu.SemaphoreType.DMA(())],
        compiler_params=_cparams(("arbitrary",)),
        name="dispatch",
    )(pos_tiles, h_prompt, h_sample)


GMM_ROWS = 1024
GMM_SUB = 512


def _gmm_body(ve_ref, vb_ref, vlo_ref, vhi_ref, vfirst_ref, vnew_ref,
              xs_ref, wg_ref, wu_ref, wd_ref, ys_ref, wgu_s, wd_s):
    v = pl.program_id(0)
    lo = vlo_ref[v]
    hi = vhi_ref[v]

    @pl.when(vnew_ref[v] == 1)
    def _():
        wgu_s[:, :D_EXPERT] = wg_ref[0].astype(BF16)
        wgu_s[:, D_EXPERT:] = wu_ref[0].astype(BF16)
        wd_s[...] = wd_ref[0].astype(BF16)

    @pl.when(vfirst_ref[v] == 1)
    def _():
        ys_ref[...] = jnp.zeros_like(ys_ref)

    def sub_block(s, carry):
        r0 = pl.multiple_of(s * GMM_SUB, GMM_SUB)
        x_lo, x_hi = _unpack_bf16_pairs(_load_rows(xs_ref, (), r0, GMM_SUB))
        gu = _dot(x_lo.astype(BF16), wgu_s[:HALF, :]) + _dot(x_hi.astype(BF16), wgu_s[HALF:, :])
        act = _silu(gu[:, :D_EXPERT]) * gu[:, D_EXPERT:]
        y = _pack_bf16_pairs(_dot(act.astype(BF16), wd_s[...]))
        row = lax.broadcasted_iota(I32, y.shape, 0) + r0
        _store_rows(ys_ref, (), r0, jnp.where((row >= lo) & (row < hi), y, _load_rows(ys_ref, (), r0, GMM_SUB)))
        return carry

    lax.fori_loop(lo // GMM_SUB, (hi + GMM_SUB - 1) // GMM_SUB, sub_block, 0)


def _gmm_schedule(counts, n_rows):
    nb = n_rows // GMM_ROWS
    n_vis = nb + N_EXPERTS
    ends = jnp.cumsum(counts)
    starts = ends - counts
    first_blk = starts // GMM_ROWS
    last_blk = jnp.maximum(ends - 1, 0) // GMM_ROWS
    per_e = jnp.where(counts > 0, last_blk - first_blk + 1, 0)
    vis_end = jnp.cumsum(per_e)
    total = vis_end[-1]
    v = jnp.minimum(jnp.arange(n_vis, dtype=I32), total - 1)
    e = jnp.minimum(jnp.sum((vis_end[None, :] <= v[:, None]).astype(I32), axis=1), N_EXPERTS - 1)
    table = jnp.stack([first_blk, vis_end - per_e, starts, ends], axis=1).astype(I32)
    pick = e[:, None] == jnp.arange(N_EXPERTS, dtype=I32)[None, :]
    fb, v0, st, en = jnp.moveaxis(jnp.sum(jnp.where(pick[:, :, None], table[None], 0), axis=1), 1, 0)
    blk = fb + (v - v0)
    lo = jnp.maximum(st, blk * GMM_ROWS) - blk * GMM_ROWS
    hi = jnp.minimum(en, (blk + 1) * GMM_ROWS) - blk * GMM_ROWS
    hi = jnp.where(jnp.arange(n_vis) < total, hi, lo)
    prev = lambda a: jnp.concatenate([jnp.full((1,), -1, I32), a[:-1]])
    first = (blk != prev(blk)).astype(I32)
    new_e = (e != prev(e)).astype(I32)
    return e, blk, lo.astype(I32), hi.astype(I32), first, new_e


def _gmm(xs, counts, we_gate, we_up, we_down):
    n_rows = xs.shape[0] // ROW_TILES
    sched = _gmm_schedule(counts, n_rows)
    n_vis = sched[0].shape[0]
    grid_spec = pltpu.PrefetchScalarGridSpec(
        num_scalar_prefetch=len(sched),
        grid=(n_vis,),
        in_specs=[pl.BlockSpec((GMM_ROWS * ROW_TILES, LANES), lambda v, ve, vb, *_: (vb[v], 0)),
                  pl.BlockSpec((1, D_MODEL, D_EXPERT), lambda v, ve, *_: (ve[v], 0, 0)),
                  pl.BlockSpec((1, D_MODEL, D_EXPERT), lambda v, ve, *_: (ve[v], 0, 0)),
                  pl.BlockSpec((1, D_EXPERT, D_MODEL), lambda v, ve, *_: (ve[v], 0, 0))],
        out_specs=pl.BlockSpec((GMM_ROWS * ROW_TILES, LANES), lambda v, ve, vb, *_: (vb[v], 0)),
        scratch_shapes=[pltpu.VMEM((D_MODEL, 2 * D_EXPERT), BF16), pltpu.VMEM((D_EXPERT, D_MODEL), BF16)])
    return pl.pallas_call(
        _gmm_body,
        grid_spec=grid_spec,
        out_shape=jax.ShapeDtypeStruct(xs.shape, U32),
        compiler_params=_cparams(("arbitrary",)),
        name="expert_ffn",
    )(*sched, xs, we_gate, we_up, we_down)


def _combine_body(pos_ref, posn_ref, w_ref, pre_ref, gt2_ref, g2_ref, b2_ref, ys_ref, out_ref, gbuf, sem,
                  *, tt, n_steps):
    step = pl.program_id(0) * pl.num_programs(1) + pl.program_id(1)
    slot = step % 2

    def gather(p_ref, into):
        def body(t, carry):
            for k in range(TOP_K):
                pltpu.make_async_copy(ys_ref.at[_row_lines(p_ref[0, 0, t * TOP_K + k])],
                                      gbuf.at[into, k, _row_lines(t)], sem.at[into]).start(priority=k % 2)
            return carry
        lax.fori_loop(0, tt, body, 0)

    @pl.when(step == 0)
    def _():
        gather(pos_ref, 0)

    @pl.when(step + 1 < n_steps)
    def _():
        gather(posn_ref, 1 - slot)

    pltpu.make_async_copy(gbuf.at[slot], gbuf.at[slot], sem.at[slot]).wait()
    acc_lo = jnp.zeros((tt, HALF), F32)
    acc_hi = jnp.zeros((tt, HALF), F32)
    for k in range(TOP_K):
        y_lo, y_hi = _unpack_bf16_pairs(_load_rows(gbuf, (slot, k), 0, tt))
        wk = w_ref[:, k:k + 1]
        acc_lo = acc_lo + wk * y_lo
        acc_hi = acc_hi + wk * y_hi
    routed = jnp.concatenate([acc_lo, acc_hi], axis=1)
    y = pre_ref[0] + (1.0 + gt2_ref[0]) * routed
    out_ref[0] = _layer_norm(y) * g2_ref[...] + b2_ref[...]


def _combine(pos_tiles, w_tk, pre, gt2, ln2_g, ln2_b, ys, tt, tok0):
    b, l, _ = pre.shape
    nt = l // tt
    blk0 = tok0 // tt
    last = blk0 + b * nt - 1
    pos_spec = lambda nxt: pl.BlockSpec((1, 1, tt * TOP_K),
                                        lambda i, j: (jnp.minimum(blk0 + i * nt + j + nxt, last), 0, 0),
                                        memory_space=pltpu.SMEM)
    return pl.pallas_call(
        functools.partial(_combine_body, tt=tt, n_steps=b * nt),
        grid=(b, nt),
        in_specs=[pos_spec(0), pos_spec(1),
                  pl.BlockSpec((tt, TOP_K), lambda i, j: (blk0 + i * nt + j, 0)),
                  pl.BlockSpec((1, tt, D_MODEL), lambda i, j: (i, j, 0)),
                  _mod_spec(gt2, tt),
                  pl.BlockSpec((1, D_MODEL), lambda i, j: (0, 0)),
                  pl.BlockSpec((1, D_MODEL), lambda i, j: (0, 0)),
                  pl.BlockSpec(memory_space=pl.ANY)],
        out_specs=pl.BlockSpec((1, tt, D_MODEL), lambda i, j: (i, j, 0)),
        out_shape=jax.ShapeDtypeStruct((b, l, D_MODEL), F32),
        scratch_shapes=[pltpu.VMEM((2, TOP_K, tt * ROW_TILES, LANES), U32), pltpu.SemaphoreType.DMA((2,))],
        compiler_params=_cparams(("arbitrary", "arbitrary")),
        name="combine",
    )(pos_tiles, pos_tiles, w_tk, pre, gt2, ln2_g, ln2_b, ys)


def _prep_params(w_in, conv_qkv_w, a_log, dt_bias, dw_w, dw_b, cn_g, cn_b):
    z0 = QKV_WIDTH
    b0 = z0 + GDN_WIDTH
    g0 = b0 + 2 * GDN_HEADS
    w_bg = w_in[:, b0:g0]
    w_in_r = jnp.concatenate(
        [w_in[:, :b0], w_in[:, g0:], w_bg, jnp.zeros((D_MODEL, LANES - 2 * GDN_HEADS), w_in.dtype)], axis=1).astype(BF16)
    pad_h = jnp.zeros((GDN_HEADS,), F32)
    al = jnp.concatenate([pad_h, a_log.astype(F32)])
    db = jnp.concatenate([pad_h, dt_bias.astype(F32)])
    gp_c = jnp.zeros((SUBLANES, LANES), F32).at[0, :2 * GDN_HEADS].set(al).at[1, :2 * GDN_HEADS].set(db)
    gp_r = jnp.zeros((SUBLANES, LANES), F32).at[:, 0].set(al).at[:, 1].set(db)
    return dict(
        w_in_r=w_in_r, w_bgt=w_bg.T.astype(BF16), conv_qkv_w=conv_qkv_w.astype(F32),
        dw_w=jnp.concatenate([dw_w, jnp.zeros((1, CONV_WIDTH), dw_w.dtype)], axis=0).astype(F32),
        dw_b=dw_b.reshape(1, -1).astype(F32), cn_g=cn_g.reshape(1, -1).astype(F32), cn_b=cn_b.reshape(1, -1).astype(F32),
        gp_c=gp_c, gp_r=gp_r)


def _prep_mid_params(w_out, ln1_g, ln1_b, w_router, router_bias, ws_gate, ws_up, ws_down, ln2_g, ln2_b, gdn_norm_w):
    row = lambda a: a.reshape(1, -1).astype(F32)
    wr_t = w_router.astype(F32).T
    wr_hi = wr_t.astype(BF16)
    return dict(
        w_out=w_out.astype(BF16), ln1_g=row(ln1_g), ln1_b=row(ln1_b),
        ws_gu=jnp.concatenate([ws_gate, ws_up], axis=1).astype(BF16), ws_down=ws_down.astype(BF16),
        wr_hi=wr_hi, wr_lo=(wr_t - wr_hi.astype(F32)).astype(BF16),
        bias_b=jnp.broadcast_to(router_bias.astype(F32)[:, None], (N_EXPERTS, LANES)),
        ln2_g=row(ln2_g), ln2_b=row(ln2_b), gnw=row(gdn_norm_w))


def _tile(n, pref):
    t = min(pref, n)
    while n % t:
        t //= 2
    return t


def _token_mixer(x, mod, s_gdn, s_qkv, s_dw, prm, mprm):
    b, l, _ = x.shape
    assert l >= DW_CONV - 1 and l % SUBLANES == 0
    tl = _tile(l, 256)
    sq_pad = jnp.pad(s_qkv.astype(F32), ((0, 0), (QKV_TAIL - (GDN_CONV - 1), 0), (0, 0)))
    sd_pad = jnp.pad(s_dw.astype(F32), ((0, 0), (DW_TAIL - (DW_CONV - 1), 0), (0, 0)))
    q, k, v, z, bgc, bgr, uc, nq, nd = _front(x, mod, sq_pad, sd_pad, prm, tl)
    c = min(CHUNK, l)
    n_chunks = l // c
    g = _tile(n_chunks, 8)
    bgr = bgr.reshape(b, SUBLANES, n_chunks, c).transpose(0, 2, 1, 3)
    o, s_new = _gdn(q, k, v, z, bgc, bgr, s_gdn.astype(F32), mprm["gnw"], c, g)
    return o, uc, s_new, nq[:, QKV_TAIL - (GDN_CONV - 1):], nd[:, DW_TAIL - (DW_CONV - 1):]


def kernel(x_prompt, x_sample, state_gdn, state_qkv_conv, state_dw_conv, c_prompt, c_sample, w_ada, b_ada, w_in, conv_qkv_w, a_log, dt_bias, gdn_norm_w, dw_w, dw_b, cn_g, cn_b, w_out, ln1_g, ln1_b, w_router, router_bias, we_gate, we_up, we_down, ws_gate, ws_up, ws_down, ln2_g, ln2_b):
    bp, lp, _ = x_prompt.shape
    bs, ls, _ = x_sample.shape
    tp, ts = bp * lp, bs * ls
    yp, ys = x_prompt, x_sample
    c_all = jnp.concatenate([c_prompt, c_sample], axis=0)
    new_p, new_s = [], []
    for l in range(w_ada.shape[0]):
        prm = _prep_params(w_in[l], conv_qkv_w[l], a_log[l], dt_bias[l], dw_w[l], dw_b[l], cn_g[l], cn_b[l])
        mprm = _prep_mid_params(w_out[l], ln1_g[l], ln1_b[l], w_router[l], router_bias[l], ws_gate[l], ws_up[l],
                                ws_down[l], ln2_g[l], ln2_b[l], gdn_norm_w[l])
        mod = _ada_mod(c_all, w_ada[l], b_ada[l]).reshape(bp + bs, 6, D_MODEL)
        mod_p, mod_s = mod[:bp], mod[bp:]

        zg = jnp.zeros((bp, GDN_HEADS, GDN_HEAD_DIM, GDN_HEAD_DIM), F32)
        zq = jnp.zeros((bp, GDN_CONV - 1, QKV_WIDTH), F32)
        zd = jnp.zeros((bp, DW_CONV - 1, CONV_WIDTH), F32)
        o_p, uc_p, g_p, q_p, d_p = _token_mixer(yp, mod_p, zg, zq, zd, prm, mprm)
        o_s, uc_s, g_s, q_s, d_s = _token_mixer(ys, mod_s, state_gdn[l], state_qkv_conv[l], state_dw_conv[l], prm, mprm)

        mods_p = tuple(mod_p[:, j:j + 1, :] for j in (2, 3, 4, 5))
        mods_s = tuple(jnp.repeat(mod_s[:, j, :], ls, axis=0)[None] for j in (2, 3, 4, 5))
        flat = lambda a: a.reshape(1, ts, a.shape[-1])
        pre_p, h_p, lg_p = _mid(o_p, uc_p, yp, mods_p, mprm, _tile(lp, 512))
        pre_s, h_s, lg_s = _mid(flat(o_s), flat(uc_s), flat(ys), mods_s, mprm, _tile(ts, 256))

        tt = _tile(ts, 512)
        assert tp % tt == 0 and lp % tt == 0 and (TOP_K * (tp + ts)) % GMM_ROWS == 0
        idx, w, rank, cnt = _route(jnp.concatenate([lg_p, lg_s], axis=1), mprm["bias_b"], tt)
        pos = _positions(idx, rank, cnt, tt)
        pos_tiles = pos.T.reshape((tp + ts) // tt, 1, tt * TOP_K)
        xs = _dispatch(pos_tiles, h_p.reshape(tp * ROW_TILES, LANES), h_s.reshape(ts * ROW_TILES, LANES), tt)
        ye = _gmm(xs, cnt[:, 0].astype(I32), we_gate[l], we_up[l], we_down[l])
        w_tk = w.T
        yp = _combine(pos_tiles, w_tk, pre_p, mods_p[3], mprm["ln2_g"], mprm["ln2_b"], ye, tt, 0)
        ys = _combine(pos_tiles, w_tk, pre_s, mods_s[3], mprm["ln2_g"], mprm["ln2_b"], ye, tt, tp
                      ).reshape(bs, ls, D_MODEL)
        new_p.append((g_p.astype(state_gdn.dtype), q_p.astype(x_prompt.dtype), d_p.astype(x_prompt.dtype)))
        new_s.append((g_s.astype(state_gdn.dtype), q_s.astype(state_qkv_conv.dtype), d_s.astype(state_dw_conv.dtype)))
    stack = lambda rows, j: jnp.stack([r[j] for r in rows])
    return (yp, ys, stack(new_p, 0), stack(new_p, 1), stack(new_p, 2), stack(new_s, 0), stack(new_s, 1), stack(new_s, 2))
```

```python
import functools

import jax
import jax.numpy as jnp
from jax import lax
from jax.experimental import pallas as pl
from jax.experimental.pallas import tpu as pltpu

F32 = jnp.float32
BF16 = jnp.bfloat16
I32 = jnp.int32
U32 = jnp.uint32

D_MODEL = 1024
GDN_WIDTH = 512
CONV_WIDTH = 512
GDN_HEAD_DIM = 128
GDN_HEADS = 4
QKV_WIDTH = 3 * GDN_WIDTH
GDN_CONV = 4
DW_CONV = 31
CHUNK = 64
N_EXPERTS = 256
TOP_K = 8
N_GROUPS = 8
GROUP_SIZE = N_EXPERTS // N_GROUPS
TOPK_GROUPS = 4
D_EXPERT = 256
D_SHARED = 256
ROUTED_SCALE = 2.5
LN_EPS = 1e-5
RMS_EPS = 1e-6
L2_EPS = 1e-6
DEPTH = 1
ALPHA = (2.0 * DEPTH) ** 0.25

LANES = 128
SUBLANES = 8
VMEM_LIMIT_BYTES = 56 * 1024 * 1024

COL_Z = QKV_WIDTH
COL_GV = COL_Z + GDN_WIDTH
COL_GG = COL_GV + CONV_WIDTH
COL_BG = COL_GG + CONV_WIDTH
D_PROJ_PAD = COL_BG + LANES
QKV_TAIL = SUBLANES
DW_TAIL = 32
NEG_BIG = -1e30


def _cparams(sem):
    return pltpu.CompilerParams(dimension_semantics=sem, vmem_limit_bytes=VMEM_LIMIT_BYTES)


def _split_bf16(x):
    hi = x.astype(BF16)
    lo = (x - hi.astype(F32)).astype(BF16)
    return hi, lo


def _dot(a, b):
    return jnp.dot(a, b, preferred_element_type=F32)


def _dot_nt(a, b):
    return lax.dot_general(a, b, (((1,), (1,)), ((), ())), preferred_element_type=F32)


def _dot_tn(a, b):
    return lax.dot_general(a, b, (((0,), (0,)), ((), ())), preferred_element_type=F32)


def _dot_hp(a, b):
    ah, al = _split_bf16(a)
    bh, bl = _split_bf16(b)
    return _dot(ah, bh) + (_dot(ah, bl) + _dot(al, bh))


def _sigmoid(x):
    return 1.0 / (1.0 + jnp.exp(-x))


def _silu(x):
    return x * _sigmoid(x)


def _softplus(x):
    return jnp.maximum(x, 0.0) + jnp.log(1.0 + jnp.exp(-jnp.abs(x)))


def _layer_norm(x):
    mu = jnp.mean(x, axis=-1, keepdims=True)
    xc = x - mu
    var = jnp.mean(xc * xc, axis=-1, keepdims=True)
    return xc * lax.rsqrt(var + LN_EPS)


def _ada_body(c_ref, w_ref, b_ref, o_ref):
    o_ref[...] = _dot_hp(_silu(c_ref[...]), w_ref[...]) + b_ref[...]


def _ada_mod(c, w_ada, b_ada):
    bt = c.shape[0]
    n_col = w_ada.shape[1] // D_MODEL
    return pl.pallas_call(
        _ada_body,
        grid=(n_col,),
        in_specs=[
            pl.BlockSpec((bt, D_MODEL), lambda j: (0, 0)),
            pl.BlockSpec((D_MODEL, D_MODEL), lambda j: (0, j)),
            pl.BlockSpec((1, D_MODEL), lambda j: (0, j)),
        ],
        out_specs=pl.BlockSpec((bt, D_MODEL), lambda j: (0, j)),
        out_shape=jax.ShapeDtypeStruct((bt, w_ada.shape[1]), F32),
        compiler_params=_cparams(("arbitrary",)),
        name="ada_mod",
    )(c, w_ada, b_ada.reshape(1, -1))


def _front_body(x_ref, mod_ref, win_ref, wbgt_ref, cw_ref, dww_ref, dwb_ref, cng_ref, cnb_ref,
                gpc_ref, gpr_ref, sq_ref, sd_ref,
                q_ref, k_ref, v_ref, z_ref, bgc_ref, bgr_ref, uc_ref, nq_ref, nd_ref,
                qkv_buf, u_buf, *, tl):
    t = pl.program_id(1)
    rq = QKV_TAIL + tl
    ru = DW_TAIL + tl
    n_qkv = QKV_WIDTH // LANES
    n_u = CONV_WIDTH // LANES
    lanes = lambda c: slice(c * LANES, (c + 1) * LANES)

    @pl.when(t == 0)
    def _():
        for c in range(n_qkv):
            qkv_buf[c * rq:c * rq + QKV_TAIL, :] = sq_ref[0, :, lanes(c)]
        for c in range(n_u):
            u_buf[c * ru:c * ru + DW_TAIL, :] = sd_ref[0, :, lanes(c)]

    sh1 = mod_ref[0, 0:1, :]
    sc1 = mod_ref[0, 1:2, :]
    h = _layer_norm(x_ref[0]) * (1.0 + sc1) + sh1
    hb = h.astype(BF16)

    qkv = _dot(hb, win_ref[:, 0:QKV_WIDTH])
    for c in range(n_qkv):
        qkv_buf[c * rq + QKV_TAIL:(c + 1) * rq, :] = qkv[:, lanes(c)]
    off_q = QKV_TAIL - (GDN_CONV - 1)
    for c in range(n_qkv):
        acc = cw_ref[0:1, lanes(c)] * qkv_buf[c * rq + off_q:c * rq + off_q + tl, :]
        for j in range(1, GDN_CONV):
            acc = acc + cw_ref[j:j + 1, lanes(c)] * qkv_buf[c * rq + off_q + j:c * rq + off_q + j + tl, :]
        a = _silu(acc)
        hd = c % GDN_HEADS
        if c < GDN_HEADS:
            a = a * lax.rsqrt(jnp.sum(a * a, axis=-1, keepdims=True) + L2_EPS) * (GDN_HEAD_DIM ** -0.5)
            q_ref[0, :, lanes(hd)] = a.astype(BF16)
        elif c < 2 * GDN_HEADS:
            a = a * lax.rsqrt(jnp.sum(a * a, axis=-1, keepdims=True) + L2_EPS)
            k_ref[0, :, lanes(hd)] = a.astype(BF16)
        else:
            v_ref[0, :, lanes(hd)] = a.astype(BF16)
    z_ref[0] = _dot(hb, win_ref[:, COL_Z:COL_GV]).astype(BF16)

    raw_c = _dot(hb, win_ref[:, COL_BG:D_PROJ_PAD])
    lane = lax.broadcasted_iota(I32, raw_c.shape, 1)
    neg_a_c = -jnp.exp(gpc_ref[0:1, :])
    g_c = neg_a_c * _softplus(raw_c + gpc_ref[1:2, :])
    bgc_ref[0] = jnp.where(lane < GDN_HEADS, _sigmoid(raw_c), g_c)
    raw_r = _dot_nt(wbgt_ref[...], hb)
    row = lax.broadcasted_iota(I32, raw_r.shape, 0)
    neg_a_r = -jnp.exp(gpr_ref[:, 0:1])
    g_r = neg_a_r * _softplus(raw_r + gpr_ref[:, 1:2])
    bgr_ref[0] = jnp.where(row < GDN_HEADS, _sigmoid(raw_r), g_r)

    gv = _dot(hb, win_ref[:, COL_GV:COL_GG])
    gg = _dot(hb, win_ref[:, COL_GG:COL_BG])
    u = gv * _sigmoid(gg)
    for c in range(n_u):
        u_buf[c * ru + DW_TAIL:(c + 1) * ru, :] = u[:, lanes(c)]
    off_u = DW_TAIL - (DW_CONV - 1)
    daccs = []
    for c in range(n_u):
        dacc = dwb_ref[:, lanes(c)] + dww_ref[0:1, lanes(c)] * u_buf[c * ru + off_u:c * ru + off_u + tl, :]
        for j in range(1, DW_CONV):
            dacc = dacc + dww_ref[j:j + 1, lanes(c)] * u_buf[c * ru + off_u + j:c * ru + off_u + j + tl, :]
        daccs.append(dacc)
    dacc = jnp.concatenate(daccs, axis=1)
    uc_ref[0] = _silu(_layer_norm(dacc) * cng_ref[...] + cnb_ref[...]).astype(BF16)

    for c in range(n_qkv):
        nq = qkv_buf[c * rq + tl:(c + 1) * rq, :]
        qkv_buf[c * rq:c * rq + QKV_TAIL, :] = nq
        nq_ref[0, :, lanes(c)] = nq
    for c in range(n_u):
        nd = u_buf[c * ru + tl:(c + 1) * ru, :]
        u_buf[c * ru:c * ru + DW_TAIL, :] = nd
        nd_ref[0, :, lanes(c)] = nd


def _front(x, mod, sq_pad, sd_pad, prm, tl):
    b, l, _ = x.shape
    nt = l // tl
    tok = lambda w, dt: (pl.BlockSpec((1, tl, w), lambda i, j: (i, j, 0)), jax.ShapeDtypeStruct((b, l, w), dt))
    full = lambda a: pl.BlockSpec(a.shape, lambda i, j: (0,) * a.ndim)
    outs = [tok(GDN_WIDTH, BF16), tok(GDN_WIDTH, BF16), tok(GDN_WIDTH, BF16), tok(GDN_WIDTH, BF16),
            tok(LANES, F32),
            (pl.BlockSpec((1, SUBLANES, tl), lambda i, j: (i, 0, j)), jax.ShapeDtypeStruct((b, SUBLANES, l), F32)),
            tok(CONV_WIDTH, BF16),
            (pl.BlockSpec((1, QKV_TAIL, QKV_WIDTH), lambda i, j: (i, 0, 0)),
             jax.ShapeDtypeStruct((b, QKV_TAIL, QKV_WIDTH), F32)),
            (pl.BlockSpec((1, DW_TAIL, CONV_WIDTH), lambda i, j: (i, 0, 0)),
             jax.ShapeDtypeStruct((b, DW_TAIL, CONV_WIDTH), F32))]
    consts = [prm["w_in_r"], prm["w_bgt"], prm["conv_qkv_w"], prm["dw_w"], prm["dw_b"], prm["cn_g"], prm["cn_b"],
              prm["gp_c"], prm["gp_r"]]
    return pl.pallas_call(
        functools.partial(_front_body, tl=tl),
        grid=(b, nt),
        in_specs=[pl.BlockSpec((1, tl, D_MODEL), lambda i, j: (i, j, 0)),
                  pl.BlockSpec((1, 6, D_MODEL), lambda i, j: (i, 0, 0))]
                 + [full(a) for a in consts]
                 + [pl.BlockSpec((1, QKV_TAIL, QKV_WIDTH), lambda i, j: (i, 0, 0)),
                    pl.BlockSpec((1, DW_TAIL, CONV_WIDTH), lambda i, j: (i, 0, 0))],
        out_specs=[o[0] for o in outs],
        out_shape=[o[1] for o in outs],
        scratch_shapes=[pltpu.VMEM((QKV_WIDTH // LANES * (QKV_TAIL + tl), LANES), F32),
                        pltpu.VMEM((CONV_WIDTH // LANES * (DW_TAIL + tl), LANES), F32)],
        compiler_params=_cparams(("arbitrary", "arbitrary")),
        name="mixer_front",
    )(x, mod, *consts, sq_pad, sd_pad)


def _split3_bf16(x):
    hi = x.astype(BF16)
    r1 = x - hi.astype(F32)
    mid = r1.astype(BF16)
    lo = (r1 - mid.astype(F32)).astype(BF16)
    return hi, mid, lo


def _tri_inverse_stages(ms, ri, ci, out):
    c = ms[0].shape[0]
    eye = jnp.where(ri == ci, 1.0, 0.0)
    pair = (ri >> 1) == (ci >> 1)
    ps = [eye - jnp.where(pair, m, 0.0) for m in ms]
    w = 2
    while w < c:
        s = w.bit_length() - 1
        sel = ((ri >> (s + 1)) == (ci >> (s + 1))) & ((ri >> s) > (ci >> s))
        pbs = [p.astype(BF16) for p in ps]
        xs = [_dot(pb, jnp.where(sel, m, 0.0).astype(BF16)).astype(BF16) for pb, m in zip(pbs, ms)]
        yield
        ps = [p - _dot(x, pb) for p, x, pb in zip(ps, xs, pbs)]
        yield
        w *= 2
    out.extend(ps)


def _gdn_body(q_ref, k_ref, v_ref, z_ref, bgc_ref, bgr_ref, s0_ref, gnw_ref, o_ref, sout_ref,
              s_scr, ku_s, kw_s, au_s, qe_s, egl_s, *, c, g):
    cg = pl.program_id(1)
    ncg = pl.num_programs(1) - 1
    wset = cg % 2
    rset = 1 - wset

    ri = lax.broadcasted_iota(I32, (c, c), 0)
    ci = lax.broadcasted_iota(I32, (c, c), 1)
    causal = ri >= ci
    strict = ri > ci
    lower = jnp.where(causal, 1.0, 0.0).astype(BF16)
    upper = jnp.where(ri <= ci, 1.0, 0.0).astype(BF16)
    gnw = gnw_ref[...]

    heads = range(GDN_HEADS)
    hcols = [slice(hd * GDN_HEAD_DIM, (hd + 1) * GDN_HEAD_DIM) for hd in heads]

    def intra_stages():
        prob = []
        for i in range(g):
            rows = pl.ds(i * c, c)
            bgc = bgc_ref[0, rows, :]
            bgr = bgr_ref[0, i]
            gc_c = sum(_dot(lower, part) for part in _split3_bf16(bgc))
            gc_r = sum(_dot(part, upper) for part in _split3_bf16(bgr))
            for hd in heads:
                gcc = gc_c[:, GDN_HEADS + hd:GDN_HEADS + hd + 1]
                gcr = gc_r[GDN_HEADS + hd:GDN_HEADS + hd + 1, :]
                prob.append(dict(i=i, hd=hd, rows=rows, cols=hcols[hd], beta=bgc[:, hd:hd + 1], gcc=gcc,
                                 decay=jnp.exp(jnp.where(causal, gcc - gcr, NEG_BIG))))
        yield
        for p in prob:
            p["kh"] = k_ref[0, p["rows"], p["cols"]]
            p["kb"] = p["kh"].astype(F32) * p["beta"]
        ms = [jnp.where(strict, _dot_nt(p["kb"].astype(BF16), p["kh"]) * p["decay"], 0.0) for p in prob]
        yield
        t_invs = []
        yield from _tri_inverse_stages(ms, ri, ci, t_invs)
        uws = []
        for p, t_inv in zip(prob, t_invs):
            p["egc"] = jnp.exp(p["gcc"])
            vf = v_ref[0, p["rows"], p["cols"]].astype(F32)
            rhs = jnp.concatenate([vf * p["beta"], p["kb"] * p["egc"]], axis=1).astype(BF16)
            uws.append(_dot(t_inv.astype(BF16), rhs).astype(BF16))
        yield
        for p in prob:
            p["qh"] = q_ref[0, p["rows"], p["cols"]]
            p["a"] = jnp.where(causal, _dot_nt(p["qh"], p["kh"]) * p["decay"], 0.0).astype(BF16)
            p["g_last"] = p["gcc"][c - 1:c, :]
            p["kd"] = (p["kh"].astype(F32) * jnp.exp(p["g_last"] - p["gcc"])).astype(BF16)
        yield
        kuws = [_dot_tn(p["kd"], uw) for p, uw in zip(prob, uws)]
        yield
        auws = [_dot(p["a"], uw) for p, uw in zip(prob, uws)]
        yield
        for p, kuw, auw in zip(prob, kuws, auws):
            rows, cols, hd = p["rows"], p["cols"], p["hd"]
            srows = pl.ds(p["i"] * GDN_HEAD_DIM, GDN_HEAD_DIM)
            ku_s[wset, srows, cols] = kuw[:, :GDN_HEAD_DIM]
            kw_s[wset, srows, cols] = kuw[:, GDN_HEAD_DIM:].astype(BF16)
            au_s[wset, rows, cols] = auw[:, :GDN_HEAD_DIM]
            qe_s[wset, rows, cols] = (p["qh"].astype(F32) * p["egc"] - auw[:, GDN_HEAD_DIM:]).astype(BF16)
            egl_s[wset, pl.ds(p["i"] * SUBLANES + hd, 1), :] = jnp.broadcast_to(jnp.exp(p["g_last"]), (1, LANES))

    def inter(i):
        rows = pl.ds(i * c, c)
        srows = pl.ds(i * GDN_HEAD_DIM, GDN_HEAD_DIM)
        ss = [s_scr[hd] for hd in heads]
        sbs = [s.astype(BF16) for s in ss]
        upd = [_dot(kw_s[rset, srows, hcols[hd]], sbs[hd]) for hd in heads]
        for hd in heads:
            s_scr[hd] = ss[hd] * egl_s[rset, pl.ds(i * SUBLANES + hd, 1), :] + (ku_s[rset, srows, hcols[hd]] - upd[hd])
        os_ = [_dot(qe_s[rset, rows, hcols[hd]], sbs[hd]) + au_s[rset, rows, hcols[hd]] for hd in heads]
        for hd in heads:
            o = os_[hd]
            gate = _silu(z_ref[0, rows, hcols[hd]].astype(F32))
            on = o * lax.rsqrt(jnp.mean(o * o, axis=-1, keepdims=True) + RMS_EPS) * gnw * gate
            o_ref[0, rows, hcols[hd]] = on.astype(BF16)

    def run(do_intra, do_inter):
        stages = intra_stages() if do_intra else iter(())
        todo = list(range(g)) if do_inter else []
        n_stages = 6 + 2 * (c.bit_length() - 2)
        every = max(1, n_stages // max(1, len(todo)))
        for n, _ in enumerate(stages):
            if todo and n % every == 0:
                inter(todo.pop(0))
        for i in todo:
            inter(i)

    @pl.when(cg == 0)
    def _():
        s_scr[...] = s0_ref[0]
        run(True, False)

    @pl.when((cg > 0) & (cg < ncg))
    def _():
        run(True, True)

    @pl.when(cg == ncg)
    def _():
        run(False, True)
        sout_ref[0] = s_scr[...]


def _gdn(q, k, v, z, bgc, bgr, s0, gnw, c, g):
    b, l, _ = q.shape
    ncg = l // (c * g)
    cur = lambda w: pl.BlockSpec((1, c * g, w), lambda i, j: (i, jnp.minimum(j, ncg - 1), 0))
    prev = lambda w: pl.BlockSpec((1, c * g, w), lambda i, j: (i, jnp.maximum(j - 1, 0), 0))
    st = pl.BlockSpec((1, GDN_HEADS, GDN_HEAD_DIM, GDN_HEAD_DIM), lambda i, j: (i, 0, 0, 0))
    tok_buf = lambda dt: pltpu.VMEM((2, c * g, GDN_WIDTH), dt)
    state_buf = lambda dt: pltpu.VMEM((2, g * GDN_HEAD_DIM, GDN_WIDTH), dt)
    return pl.pallas_call(
        functools.partial(_gdn_body, c=c, g=g),
        grid=(b, ncg + 1),
        in_specs=[cur(GDN_WIDTH), cur(GDN_WIDTH), cur(GDN_WIDTH), prev(GDN_WIDTH), cur(LANES),
                  pl.BlockSpec((1, g, SUBLANES, c), lambda i, j: (i, jnp.minimum(j, ncg - 1), 0, 0)), st,
                  pl.BlockSpec((1, GDN_HEAD_DIM), lambda i, j: (0, 0))],
        out_specs=[prev(GDN_WIDTH), st],
        out_shape=[jax.ShapeDtypeStruct((b, l, GDN_WIDTH), BF16), jax.ShapeDtypeStruct(s0.shape, F32)],
        scratch_shapes=[pltpu.VMEM((GDN_HEADS, GDN_HEAD_DIM, GDN_HEAD_DIM), F32),
                        state_buf(F32), state_buf(BF16), tok_buf(F32), tok_buf(BF16),
                        pltpu.VMEM((2, g * SUBLANES, LANES), F32)],
        compiler_params=_cparams(("arbitrary", "arbitrary")),
        name="gdn",
    )(q, k, v, z, bgc, bgr, s0, gnw)


HALF = D_MODEL // 2
HI_MASK = 0xFFFF0000


def _pack_bf16_pairs(x):
    bits = pltpu.bitcast(x.astype(BF16).astype(F32), U32)
    return (bits[:, :HALF] >> 16) | (bits[:, HALF:] & jnp.uint32(HI_MASK))


def _unpack_bf16_pairs(p):
    return pltpu.bitcast(p << 16, F32), pltpu.bitcast(p & jnp.uint32(HI_MASK), F32)


ROW_TILES = HALF // LANES


def _row_lines(r):
    return pl.ds(r * ROW_TILES, ROW_TILES)


def _store_rows(ref, lead, r0, packed):
    n = packed.shape[0]
    for q in range(ROW_TILES):
        ref[(*lead, pl.ds(r0 * ROW_TILES + q, n, stride=ROW_TILES), slice(None))] = packed[:, q * LANES:(q + 1) * LANES]


def _load_rows(ref, lead, r0, n):
    return jnp.concatenate(
        [ref[(*lead, pl.ds(r0 * ROW_TILES + q, n, stride=ROW_TILES), slice(None))] for q in range(ROW_TILES)], axis=1)


def _mid_body(o_ref, uc_ref, x_ref, gt1_ref, sh2_ref, sc2_ref, gt2_ref,
              wout_ref, g1_ref, b1_ref, wsgu_ref, wsd_ref, wrh_ref, wrl_ref, pre_ref, h2p_ref, lgt_ref):
    gt1, sh2, sc2, gt2 = gt1_ref[0], sh2_ref[0], sc2_ref[0], gt2_ref[0]
    mix = _dot(jnp.concatenate([o_ref[0], uc_ref[0]], axis=1), wout_ref[...])
    x1 = _layer_norm(ALPHA * x_ref[0] + (1.0 + gt1) * mix) * g1_ref[...] + b1_ref[...]
    h2 = _layer_norm(x1) * (1.0 + sc2) + sh2
    hh, hl = _split_bf16(h2)
    lgt_ref[...] = _dot_nt(wrh_ref[...], hh) + (_dot_nt(wrh_ref[...], hl) + _dot_nt(wrl_ref[...], hh))
    gu = _dot(hh, wsgu_ref[...])
    act = _silu(gu[:, :D_SHARED]) * gu[:, D_SHARED:]
    shared = _dot(act.astype(BF16), wsd_ref[...])
    pre_ref[0] = ALPHA * x1 + (1.0 + gt2) * shared
    _store_rows(h2p_ref, (0,), 0, _pack_bf16_pairs(h2))


def _mod_spec(m, tl):
    if m.shape[1] == 1:
        return pl.BlockSpec((1, 1, D_MODEL), lambda i, j: (i, 0, 0))
    return pl.BlockSpec((1, tl, D_MODEL), lambda i, j: (i, j, 0))


def _mid(o, uc, x, mods, prm, tl):
    b, l, _ = x.shape
    nt = l // tl
    full = lambda a: pl.BlockSpec(a.shape, lambda i, j: (0,) * a.ndim)
    consts = [prm["w_out"], prm["ln1_g"], prm["ln1_b"], prm["ws_gu"], prm["ws_down"], prm["wr_hi"], prm["wr_lo"]]
    return pl.pallas_call(
        _mid_body,
        grid=(b, nt),
        in_specs=[pl.BlockSpec((1, tl, GDN_WIDTH), lambda i, j: (i, j, 0)),
                  pl.BlockSpec((1, tl, CONV_WIDTH), lambda i, j: (i, j, 0)),
                  pl.BlockSpec((1, tl, D_MODEL), lambda i, j: (i, j, 0))]
                 + [_mod_spec(m, tl) for m in mods] + [full(a) for a in consts],
        out_specs=[pl.BlockSpec((1, tl, D_MODEL), lambda i, j: (i, j, 0)),
                   pl.BlockSpec((1, tl * ROW_TILES, LANES), lambda i, j: (i, j, 0)),
                   pl.BlockSpec((N_EXPERTS, tl), lambda i, j: (0, i * nt + j))],
        out_shape=[jax.ShapeDtypeStruct((b, l, D_MODEL), F32), jax.ShapeDtypeStruct((b, l * ROW_TILES, LANES), U32),
                   jax.ShapeDtypeStruct((N_EXPERTS, b * l), F32)],
        compiler_params=_cparams(("arbitrary", "arbitrary")),
        name="mixer_out",
    )(o, uc, x, *mods, *consts)


def _first_max(x, row, n):
    m = jnp.max(x, axis=0, keepdims=True)
    ix = jnp.min(jnp.where(x == m, row, float(n)), axis=0, keepdims=True)
    return m, ix


def _route_body(lgp_ref, lgs_ref, bias_ref, idx_ref, w_ref, rank_ref, cnt_ref, cnt_scr, *, tr, n_prompt_tiles):
    @pl.when(pl.program_id(0) == 0)
    def _():
        cnt_scr[...] = jnp.zeros_like(cnt_scr)

    neg = -jnp.inf
    logits = jnp.where(pl.program_id(0) < n_prompt_tiles, lgp_ref[...], lgs_ref[...])
    scores = _sigmoid(logits)
    sel = scores + bias_ref[:, 0:1]
    row_g = lax.broadcasted_iota(I32, (GROUP_SIZE, tr), 0).astype(F32)
    gs = []
    for g in range(N_GROUPS):
        blk = sel[g * GROUP_SIZE:(g + 1) * GROUP_SIZE, :]
        m1, i1 = _first_max(blk, row_g, GROUP_SIZE)
        m2 = jnp.max(jnp.where(row_g == i1, neg, blk), axis=0, keepdims=True)
        gs.append(m1 + m2)
    gs = jnp.concatenate(gs, axis=0)
    row_n = lax.broadcasted_iota(I32, (N_GROUPS, tr), 0).astype(F32)
    chosen = jnp.zeros((N_GROUPS, tr), F32)
    for _ in range(TOPK_GROUPS):
        _, ix = _first_max(gs, row_n, N_GROUPS)
        hit = row_n == ix
        chosen = jnp.where(hit, 1.0, chosen)
        gs = jnp.where(hit, neg, gs)
    selm = jnp.concatenate(
        [jnp.where(chosen[g:g + 1, :] > 0.5, sel[g * GROUP_SIZE:(g + 1) * GROUP_SIZE, :], neg) for g in range(N_GROUPS)],
        axis=0)
    row_e = lax.broadcasted_iota(I32, (N_EXPERTS, tr), 0).astype(F32)
    idxs, ws = [], []
    picked = jnp.zeros((N_EXPERTS, tr), F32)
    for _ in range(TOP_K):
        _, ix = _first_max(selm, row_e, N_EXPERTS)
        hit = row_e == ix
        ws.append(jnp.sum(jnp.where(hit, scores, 0.0), axis=0, keepdims=True))
        idxs.append(ix)
        selm = jnp.where(hit, neg, selm)
        picked = jnp.where(hit, 1.0, picked)
    wsum = ws[0]
    for wk in ws[1:]:
        wsum = wsum + wk
    idx_ref[...] = jnp.concatenate(idxs, axis=0).astype(I32)
    w_ref[...] = jnp.concatenate(ws, axis=0) / wsum * ROUTED_SCALE
    ti = lax.broadcasted_iota(I32, (tr, tr), 0)
    tj = lax.broadcasted_iota(I32, (tr, tr), 1)
    before = _dot(picked.astype(BF16), jnp.where(ti < tj, 1.0, 0.0).astype(BF16)) + cnt_scr[:, 0:1]
    rank_ref[...] = jnp.concatenate(
        [jnp.sum(jnp.where(row_e == ix, before, 0.0), axis=0, keepdims=True) for ix in idxs], axis=0).astype(I32)
    cnt_scr[...] = cnt_scr[...] + jnp.sum(picked, axis=1, keepdims=True)
    cnt_ref[...] = cnt_scr[...]


def _route(logits_p, logits_s, bias_b, tr):
    ntp, nts = logits_p.shape[1] // tr, logits_s.shape[1] // tr
    t = (ntp + nts) * tr
    kt = lambda dt: (pl.BlockSpec((TOP_K, tr), lambda i: (0, i)), jax.ShapeDtypeStruct((TOP_K, t), dt))
    outs = [kt(I32), kt(F32), kt(I32),
            (pl.BlockSpec((N_EXPERTS, LANES), lambda i: (0, 0)), jax.ShapeDtypeStruct((N_EXPERTS, LANES), F32))]
    return pl.pallas_call(
        functools.partial(_route_body, tr=tr, n_prompt_tiles=ntp),
        grid=(ntp + nts,),
        in_specs=[pl.BlockSpec((N_EXPERTS, tr), lambda i: (0, jnp.minimum(i, ntp - 1))),
                  pl.BlockSpec((N_EXPERTS, tr), lambda i: (0, jnp.maximum(i - ntp, 0))),
                  pl.BlockSpec((N_EXPERTS, LANES), lambda i: (0, 0))],
        out_specs=[o[0] for o in outs],
        out_shape=[o[1] for o in outs],
        scratch_shapes=[pltpu.VMEM((N_EXPERTS, LANES), F32)],
        compiler_params=_cparams(("arbitrary",)),
        name="route",
    )(logits_p, logits_s, bias_b)


def _pos_body(idx_ref, rank_ref, cnt_ref, pos_ref, *, tr):
    ei = lax.broadcasted_iota(I32, (N_EXPERTS, N_EXPERTS), 0)
    ej = lax.broadcasted_iota(I32, (N_EXPERTS, N_EXPERTS), 1)
    below = jnp.where(ej < ei, 1.0, 0.0).astype(BF16)
    start = sum(_dot(below, part) for part in _split3_bf16(cnt_ref[...]))[:, 0:1]
    row_e = lax.broadcasted_iota(I32, (N_EXPERTS, tr), 0)
    pos_ref[...] = jnp.concatenate(
        [jnp.sum(jnp.where(row_e == idx_ref[k:k + 1, :], start, 0.0), axis=0, keepdims=True) for k in range(TOP_K)],
        axis=0).astype(I32) + rank_ref[...]


def _positions(idx, rank, cnt, tr):
    t = idx.shape[1]
    kt = pl.BlockSpec((TOP_K, tr), lambda i: (0, i))
    return pl.pallas_call(
        functools.partial(_pos_body, tr=tr),
        grid=(t // tr,),
        in_specs=[kt, kt, pl.BlockSpec((N_EXPERTS, LANES), lambda i: (0, 0))],
        out_specs=kt,
        out_shape=jax.ShapeDtypeStruct((TOP_K, t), I32),
        compiler_params=_cparams(("arbitrary",)),
        name="positions",
    )(idx, rank, cnt)


def _dispatch_body(pos_ref, hp_ref, hs_ref, xs_ref, sem, *, tt, n_prompt_tiles):
    i = pl.program_id(0)

    def scatter(src_ref):
        def body(t, carry):
            for k in range(TOP_K):
                pltpu.make_async_copy(src_ref.at[_row_lines(t)], xs_ref.at[_row_lines(pos_ref[0, 0, t * TOP_K + k])],
                                      sem).start(priority=k % 2)
            return carry
        lax.fori_loop(0, tt, body, 0)
        for _ in range(TOP_K):
            pltpu.make_async_copy(src_ref, xs_ref.at[pl.ds(0, tt * ROW_TILES)], sem).wait()

    @pl.when(i < n_prompt_tiles)
    def _():
        scatter(hp_ref)

    @pl.when(i >= n_prompt_tiles)
    def _():
        scatter(hs_ref)


def _dispatch(pos_tiles, h_prompt, h_sample, tt):
    tp, ts = h_prompt.shape[0] // ROW_TILES, h_sample.shape[0] // ROW_TILES
    ntp, nts = tp // tt, ts // tt
    n_rows = TOP_K * (tp + ts)
    return pl.pallas_call(
        functools.partial(_dispatch_body, tt=tt, n_prompt_tiles=ntp),
        grid=(ntp + nts,),
        in_specs=[pl.BlockSpec((1, 1, tt * TOP_K), lambda i: (i, 0, 0), memory_space=pltpu.SMEM),
                  pl.BlockSpec((tt * ROW_TILES, LANES), lambda i: (jnp.minimum(i, ntp - 1), 0)),
                  pl.BlockSpec((tt * ROW_TILES, LANES), lambda i: (jnp.maximum(i - ntp, 0), 0))],
        out_specs=pl.BlockSpec(memory_space=pl.ANY),
        out_shape=jax.ShapeDtypeStruct((n_rows * ROW_TILES, LANES), U32),
        scratch_shapes=[pltpu.SemaphoreType.DMA(())],
        compiler_params=_cparams(("arbitrary",)),
        name="dispatch",
    )(pos_tiles, h_prompt, h_sample)


GMM_ROWS = 1024
GMM_SUB = 512


def _gmm_body(ve_ref, vb_ref, vlo_ref, vhi_ref, vfirst_ref, vnew_ref,
              xs_ref, wg_ref, wu_ref, wd_ref, ys_ref, wgu_s, wd_s):
    v = pl.program_id(0)
    lo = vlo_ref[v]
    hi = vhi_ref[v]

    @pl.when(vnew_ref[v] == 1)
    def _():
        wgu_s[:, :D_EXPERT] = wg_ref[0].astype(BF16)
        wgu_s[:, D_EXPERT:] = wu_ref[0].astype(BF16)
        wd_s[...] = wd_ref[0].astype(BF16)

    @pl.when(vfirst_ref[v] == 1)
    def _():
        ys_ref[...] = jnp.zeros_like(ys_ref)

    def sub_block(s, carry):
        r0 = pl.multiple_of(s * GMM_SUB, GMM_SUB)
        x_lo, x_hi = _unpack_bf16_pairs(_load_rows(xs_ref, (), r0, GMM_SUB))
        gu = _dot(x_lo.astype(BF16), wgu_s[:HALF, :]) + _dot(x_hi.astype(BF16), wgu_s[HALF:, :])
        act = _silu(gu[:, :D_EXPERT]) * gu[:, D_EXPERT:]
        y = _pack_bf16_pairs(_dot(act.astype(BF16), wd_s[...]))
        row = lax.broadcasted_iota(I32, y.shape, 0) + r0
        _store_rows(ys_ref, (), r0, jnp.where((row >= lo) & (row < hi), y, _load_rows(ys_ref, (), r0, GMM_SUB)))
        return carry

    lax.fori_loop(lo // GMM_SUB, (hi + GMM_SUB - 1) // GMM_SUB, sub_block, 0)


def _gmm_schedule(counts, n_rows):
    nb = n_rows // GMM_ROWS
    n_vis = nb + N_EXPERTS
    ends = jnp.cumsum(counts)
    starts = ends - counts
    first_blk = starts // GMM_ROWS
    last_blk = jnp.maximum(ends - 1, 0) // GMM_ROWS
    per_e = jnp.where(counts > 0, last_blk - first_blk + 1, 0)
    vis_end = jnp.cumsum(per_e)
    total = vis_end[-1]
    v = jnp.minimum(jnp.arange(n_vis, dtype=I32), total - 1)
    e = jnp.minimum(jnp.sum((vis_end[None, :] <= v[:, None]).astype(I32), axis=1), N_EXPERTS - 1)
    table = jnp.stack([first_blk, vis_end - per_e, starts, ends], axis=1).astype(I32)
    pick = e[:, None] == jnp.arange(N_EXPERTS, dtype=I32)[None, :]
    fb, v0, st, en = jnp.moveaxis(jnp.sum(jnp.where(pick[:, :, None], table[None], 0), axis=1), 1, 0)
    blk = fb + (v - v0)
    lo = jnp.maximum(st, blk * GMM_ROWS) - blk * GMM_ROWS
    hi = jnp.minimum(en, (blk + 1) * GMM_ROWS) - blk * GMM_ROWS
    hi = jnp.where(jnp.arange(n_vis) < total, hi, lo)
    prev = lambda a: jnp.concatenate([jnp.full((1,), -1, I32), a[:-1]])
    first = (blk != prev(blk)).astype(I32)
    new_e = (e != prev(e)).astype(I32)
    return e, blk, lo.astype(I32), hi.astype(I32), first, new_e


def _gmm(xs, counts, we_gate, we_up, we_down):
    n_rows = xs.shape[0] // ROW_TILES
    sched = _gmm_schedule(counts, n_rows)
    n_vis = sched[0].shape[0]
    grid_spec = pltpu.PrefetchScalarGridSpec(
        num_scalar_prefetch=len(sched),
        grid=(n_vis,),
        in_specs=[pl.BlockSpec((GMM_ROWS * ROW_TILES, LANES), lambda v, ve, vb, *_: (vb[v], 0)),
                  pl.BlockSpec((1, D_MODEL, D_EXPERT), lambda v, ve, *_: (ve[v], 0, 0)),
                  pl.BlockSpec((1, D_MODEL, D_EXPERT), lambda v, ve, *_: (ve[v], 0, 0)),
                  pl.BlockSpec((1, D_EXPERT, D_MODEL), lambda v, ve, *_: (ve[v], 0, 0))],
        out_specs=pl.BlockSpec((GMM_ROWS * ROW_TILES, LANES), lambda v, ve, vb, *_: (vb[v], 0)),
        scratch_shapes=[pltpu.VMEM((D_MODEL, 2 * D_EXPERT), BF16), pltpu.VMEM((D_EXPERT, D_MODEL), BF16)])
    return pl.pallas_call(
        _gmm_body,
        grid_spec=grid_spec,
        out_shape=jax.ShapeDtypeStruct(xs.shape, U32),
        compiler_params=_cparams(("arbitrary",)),
        name="expert_ffn",
    )(*sched, xs, we_gate, we_up, we_down)


def _combine_body(pos_ref, posn_ref, w_ref, pre_ref, gt2_ref, g2_ref, b2_ref, ys_ref, out_ref, gbuf, sem,
                  *, tt, n_steps):
    step = pl.program_id(0) * pl.num_programs(1) + pl.program_id(1)
    slot = step % 2

    def gather(p_ref, into):
        def body(t, carry):
            for k in range(TOP_K):
                pltpu.make_async_copy(ys_ref.at[_row_lines(p_ref[0, 0, t * TOP_K + k])],
                                      gbuf.at[into, k, _row_lines(t)], sem.at[into]).start(priority=k % 2)
            return carry
        lax.fori_loop(0, tt, body, 0)

    @pl.when(step == 0)
    def _():
        gather(pos_ref, 0)

    @pl.when(step + 1 < n_steps)
    def _():
        gather(posn_ref, 1 - slot)

    pltpu.make_async_copy(gbuf.at[slot], gbuf.at[slot], sem.at[slot]).wait()
    acc_lo = jnp.zeros((tt, HALF), F32)
    acc_hi = jnp.zeros((tt, HALF), F32)
    for k in range(TOP_K):
        y_lo, y_hi = _unpack_bf16_pairs(_load_rows(gbuf, (slot, k), 0, tt))
        wk = w_ref[:, k:k + 1]
        acc_lo = acc_lo + wk * y_lo
        acc_hi = acc_hi + wk * y_hi
    routed = jnp.concatenate([acc_lo, acc_hi], axis=1)
    y = pre_ref[0] + (1.0 + gt2_ref[0]) * routed
    out_ref[0] = _layer_norm(y) * g2_ref[...] + b2_ref[...]


def _combine(pos_tiles, w_tk, pre, gt2, ln2_g, ln2_b, ys, tt, tok0):
    b, l, _ = pre.shape
    nt = l // tt
    blk0 = tok0 // tt
    last = blk0 + b * nt - 1
    pos_spec = lambda nxt: pl.BlockSpec((1, 1, tt * TOP_K),
                                        lambda i, j: (jnp.minimum(blk0 + i * nt + j + nxt, last), 0, 0),
                                        memory_space=pltpu.SMEM)
    return pl.pallas_call(
        functools.partial(_combine_body, tt=tt, n_steps=b * nt),
        grid=(b, nt),
        in_specs=[pos_spec(0), pos_spec(1),
                  pl.BlockSpec((tt, TOP_K), lambda i, j: (blk0 + i * nt + j, 0)),
                  pl.BlockSpec((1, tt, D_MODEL), lambda i, j: (i, j, 0)),
                  _mod_spec(gt2, tt),
                  pl.BlockSpec((1, D_MODEL), lambda i, j: (0, 0)),
                  pl.BlockSpec((1, D_MODEL), lambda i, j: (0, 0)),
                  pl.BlockSpec(memory_space=pl.ANY)],
        out_specs=pl.BlockSpec((1, tt, D_MODEL), lambda i, j: (i, j, 0)),
        out_shape=jax.ShapeDtypeStruct((b, l, D_MODEL), F32),
        scratch_shapes=[pltpu.VMEM((2, TOP_K, tt * ROW_TILES, LANES), U32), pltpu.SemaphoreType.DMA((2,))],
        compiler_params=_cparams(("arbitrary", "arbitrary")),
        name="combine",
    )(pos_tiles, pos_tiles, w_tk, pre, gt2, ln2_g, ln2_b, ys)


def _prep_params(w_in, conv_qkv_w, a_log, dt_bias, dw_w, dw_b, cn_g, cn_b):
    z0 = QKV_WIDTH
    b0 = z0 + GDN_WIDTH
    g0 = b0 + 2 * GDN_HEADS
    w_bg = w_in[:, b0:g0]
    w_in_r = jnp.concatenate(
        [w_in[:, :b0], w_in[:, g0:], w_bg, jnp.zeros((D_MODEL, LANES - 2 * GDN_HEADS), w_in.dtype)], axis=1).astype(BF16)
    pad_h = jnp.zeros((GDN_HEADS,), F32)
    al = jnp.concatenate([pad_h, a_log.astype(F32)])
    db = jnp.concatenate([pad_h, dt_bias.astype(F32)])
    gp_c = jnp.zeros((SUBLANES, LANES), F32).at[0, :2 * GDN_HEADS].set(al).at[1, :2 * GDN_HEADS].set(db)
    gp_r = jnp.zeros((SUBLANES, LANES), F32).at[:, 0].set(al).at[:, 1].set(db)
    return dict(
        w_in_r=w_in_r, w_bgt=w_bg.T.astype(BF16), conv_qkv_w=conv_qkv_w.astype(F32),
        dw_w=jnp.concatenate([dw_w, jnp.zeros((1, CONV_WIDTH), dw_w.dtype)], axis=0).astype(F32),
        dw_b=dw_b.reshape(1, -1).astype(F32), cn_g=cn_g.reshape(1, -1).astype(F32), cn_b=cn_b.reshape(1, -1).astype(F32),
        gp_c=gp_c, gp_r=gp_r)


def _prep_mid_params(w_out, ln1_g, ln1_b, w_router, router_bias, ws_gate, ws_up, ws_down, ln2_g, ln2_b, gdn_norm_w):
    row = lambda a: a.reshape(1, -1).astype(F32)
    wr_t = w_router.astype(F32).T
    wr_hi = wr_t.astype(BF16)
    return dict(
        w_out=w_out.astype(BF16), ln1_g=row(ln1_g), ln1_b=row(ln1_b),
        ws_gu=jnp.concatenate([ws_gate, ws_up], axis=1).astype(BF16), ws_down=ws_down.astype(BF16),
        wr_hi=wr_hi, wr_lo=(wr_t - wr_hi.astype(F32)).astype(BF16),
        bias_b=jnp.broadcast_to(router_bias.astype(F32)[:, None], (N_EXPERTS, LANES)),
        ln2_g=row(ln2_g), ln2_b=row(ln2_b), gnw=row(gdn_norm_w))


def _tile(n, pref):
    t = min(pref, n)
    while n % t:
        t //= 2
    return t


def _token_mixer(x, mod, s_gdn, s_qkv, s_dw, prm, mprm):
    b, l, _ = x.shape
    assert l >= DW_CONV - 1 and l % SUBLANES == 0
    tl = _tile(l, 256)
    sq_pad = jnp.pad(s_qkv.astype(F32), ((0, 0), (QKV_TAIL - (GDN_CONV - 1), 0), (0, 0)))
    sd_pad = jnp.pad(s_dw.astype(F32), ((0, 0), (DW_TAIL - (DW_CONV - 1), 0), (0, 0)))
    q, k, v, z, bgc, bgr, uc, nq, nd = _front(x, mod, sq_pad, sd_pad, prm, tl)
    c = min(CHUNK, l)
    n_chunks = l // c
    g = _tile(n_chunks, 8)
    bgr = bgr.reshape(b, SUBLANES, n_chunks, c).transpose(0, 2, 1, 3)
    o, s_new = _gdn(q, k, v, z, bgc, bgr, s_gdn.astype(F32), mprm["gnw"], c, g)
    return o, uc, s_new, nq[:, QKV_TAIL - (GDN_CONV - 1):], nd[:, DW_TAIL - (DW_CONV - 1):]


def kernel(x_prompt, x_sample, state_gdn, state_qkv_conv, state_dw_conv, c_prompt, c_sample, w_ada, b_ada, w_in, conv_qkv_w, a_log, dt_bias, gdn_norm_w, dw_w, dw_b, cn_g, cn_b, w_out, ln1_g, ln1_b, w_router, router_bias, we_gate, we_up, we_down, ws_gate, ws_up, ws_down, ln2_g, ln2_b):
    bp, lp, _ = x_prompt.shape
    bs, ls, _ = x_sample.shape
    tp, ts = bp * lp, bs * ls
    yp, ys = x_prompt, x_sample
    c_all = jnp.concatenate([c_prompt, c_sample], axis=0)
    new_p, new_s = [], []
    for l in range(w_ada.shape[0]):
        prm = _prep_params(w_in[l], conv_qkv_w[l], a_log[l], dt_bias[l], dw_w[l], dw_b[l], cn_g[l], cn_b[l])
        mprm = _prep_mid_params(w_out[l], ln1_g[l], ln1_b[l], w_router[l], router_bias[l], ws_gate[l], ws_up[l],
                                ws_down[l], ln2_g[l], ln2_b[l], gdn_norm_w[l])
        mod = _ada_mod(c_all, w_ada[l], b_ada[l]).reshape(bp + bs, 6, D_MODEL)
        mod_p, mod_s = mod[:bp], mod[bp:]

        zg = jnp.zeros((bp, GDN_HEADS, GDN_HEAD_DIM, GDN_HEAD_DIM), F32)
        zq = jnp.zeros((bp, GDN_CONV - 1, QKV_WIDTH), F32)
        zd = jnp.zeros((bp, DW_CONV - 1, CONV_WIDTH), F32)
        o_p, uc_p, g_p, q_p, d_p = _token_mixer(yp, mod_p, zg, zq, zd, prm, mprm)
        o_s, uc_s, g_s, q_s, d_s = _token_mixer(ys, mod_s, state_gdn[l], state_qkv_conv[l], state_dw_conv[l], prm, mprm)

        mods_p = tuple(mod_p[:, j:j + 1, :] for j in (2, 3, 4, 5))
        mods_s = tuple(jnp.repeat(mod_s[:, j, :], ls, axis=0)[None] for j in (2, 3, 4, 5))
        flat = lambda a: a.reshape(1, ts, a.shape[-1])
        pre_p, h_p, lg_p = _mid(o_p, uc_p, yp, mods_p, mprm, _tile(lp, 512))
        pre_s, h_s, lg_s = _mid(flat(o_s), flat(uc_s), flat(ys), mods_s, mprm, _tile(ts, 256))

        tt = _tile(ts, 512)
        assert tp % tt == 0 and lp % tt == 0 and (TOP_K * (tp + ts)) % GMM_ROWS == 0
        idx, w, rank, cnt = _route(lg_p, lg_s, mprm["bias_b"], tt)
        pos = _positions(idx, rank, cnt, tt)
        pos_tiles = pos.T.reshape((tp + ts) // tt, 1, tt * TOP_K)
        xs = _dispatch(pos_tiles, h_p.reshape(tp * ROW_TILES, LANES), h_s.reshape(ts * ROW_TILES, LANES), tt)
        ye = _gmm(xs, cnt[:, 0].astype(I32), we_gate[l], we_up[l], we_down[l])
        w_tk = w.T
        yp = _combine(pos_tiles, w_tk, pre_p, mods_p[3], mprm["ln2_g"], mprm["ln2_b"], ye, tt, 0)
        ys = _combine(pos_tiles, w_tk, pre_s, mods_s[3], mprm["ln2_g"], mprm["ln2_b"], ye, tt, tp
                      ).reshape(bs, ls, D_MODEL)
        new_p.append((g_p.astype(state_gdn.dtype), q_p.astype(x_prompt.dtype), d_p.astype(x_prompt.dtype)))
        new_s.append((g_s.astype(state_gdn.dtype), q_s.astype(state_qkv_conv.dtype), d_s.astype(state_dw_conv.dtype)))
    stack = lambda rows, j: jnp.stack([r[j] for r in rows])
    return (yp, ys, stack(new_p, 0), stack(new_p, 1), stack(new_p, 2), stack(new_s, 0), stack(new_s, 1), stack(new_s, 2))
```

```python
import functools

import jax
import jax.numpy as jnp
from jax import lax
from jax.experimental import pallas as pl
from jax.experimental.pallas import tpu as pltpu

F32 = jnp.float32
BF16 = jnp.bfloat16
I32 = jnp.int32
U32 = jnp.uint32

D_MODEL = 1024
GDN_WIDTH = 512
CONV_WIDTH = 512
GDN_HEAD_DIM = 128
GDN_HEADS = 4
QKV_WIDTH = 3 * GDN_WIDTH
GDN_CONV = 4
DW_CONV = 31
CHUNK = 64
N_EXPERTS = 256
TOP_K = 8
N_GROUPS = 8
GROUP_SIZE = N_EXPERTS // N_GROUPS
TOPK_GROUPS = 4
D_EXPERT = 256
D_SHARED = 256
ROUTED_SCALE = 2.5
LN_EPS = 1e-5
RMS_EPS = 1e-6
L2_EPS = 1e-6
DEPTH = 1
ALPHA = (2.0 * DEPTH) ** 0.25

LANES = 128
SUBLANES = 8
VMEM_LIMIT_BYTES = 56 * 1024 * 1024

COL_Z = QKV_WIDTH
COL_GV = COL_Z + GDN_WIDTH
COL_GG = COL_GV + CONV_WIDTH
COL_BG = COL_GG + CONV_WIDTH
D_PROJ_PAD = COL_BG + LANES
QKV_TAIL = SUBLANES
DW_TAIL = 32
NEG_BIG = -1e30


def _cparams(sem):
    return pltpu.CompilerParams(dimension_semantics=sem, vmem_limit_bytes=VMEM_LIMIT_BYTES)


def _split_bf16(x):
    hi = x.astype(BF16)
    lo = (x - hi.astype(F32)).astype(BF16)
    return hi, lo


def _dot(a, b):
    return jnp.dot(a, b, preferred_element_type=F32)


def _dot_nt(a, b):
    return lax.dot_general(a, b, (((1,), (1,)), ((), ())), preferred_element_type=F32)


def _dot_tn(a, b):
    return lax.dot_general(a, b, (((0,), (0,)), ((), ())), preferred_element_type=F32)


def _dot_hp(a, b):
    ah, al = _split_bf16(a)
    bh, bl = _split_bf16(b)
    return _dot(ah, bh) + (_dot(ah, bl) + _dot(al, bh))


def _sigmoid(x):
    return 1.0 / (1.0 + jnp.exp(-x))


def _silu(x):
    return x * _sigmoid(x)


def _softplus(x):
    return jnp.maximum(x, 0.0) + jnp.log(1.0 + jnp.exp(-jnp.abs(x)))


def _layer_norm(x):
    mu = jnp.mean(x, axis=-1, keepdims=True)
    xc = x - mu
    var = jnp.mean(xc * xc, axis=-1, keepdims=True)
    return xc * lax.rsqrt(var + LN_EPS)


def _ada_body(c_ref, w_ref, b_ref, o_ref):
    o_ref[...] = _dot_hp(_silu(c_ref[...]), w_ref[...]) + b_ref[...]


def _ada_mod(c, w_ada, b_ada):
    bt = c.shape[0]
    n_col = w_ada.shape[1] // D_MODEL
    return pl.pallas_call(
        _ada_body,
        grid=(n_col,),
        in_specs=[
            pl.BlockSpec((bt, D_MODEL), lambda j: (0, 0)),
            pl.BlockSpec((D_MODEL, D_MODEL), lambda j: (0, j)),
            pl.BlockSpec((1, D_MODEL), lambda j: (0, j)),
        ],
        out_specs=pl.BlockSpec((bt, D_MODEL), lambda j: (0, j)),
        out_shape=jax.ShapeDtypeStruct((bt, w_ada.shape[1]), F32),
        compiler_params=_cparams(("arbitrary",)),
        name="ada_mod",
    )(c, w_ada, b_ada.reshape(1, -1))


def _front_body(x_ref, mod_ref, win_ref, wbgt_ref, cw_ref, dww_ref, dwb_ref, cng_ref, cnb_ref,
                gpc_ref, gpr_ref, sq_ref, sd_ref,
                q_ref, k_ref, v_ref, z_ref, bgc_ref, bgr_ref, uc_ref, nq_ref, nd_ref,
                qkv_buf, u_buf, *, tl):
    t = pl.program_id(1)
    rq = QKV_TAIL + tl
    ru = DW_TAIL + tl
    n_qkv = QKV_WIDTH // LANES
    n_u = CONV_WIDTH // LANES
    lanes = lambda c: slice(c * LANES, (c + 1) * LANES)

    @pl.when(t == 0)
    def _():
        for c in range(n_qkv):
            qkv_buf[c * rq:c * rq + QKV_TAIL, :] = sq_ref[0, :, lanes(c)]
        for c in range(n_u):
            u_buf[c * ru:c * ru + DW_TAIL, :] = sd_ref[0, :, lanes(c)]

    sh1 = mod_ref[0, 0:1, :]
    sc1 = mod_ref[0, 1:2, :]
    h = _layer_norm(x_ref[0]) * (1.0 + sc1) + sh1
    hb = h.astype(BF16)

    qkv = _dot(hb, win_ref[:, 0:QKV_WIDTH])
    for c in range(n_qkv):
        qkv_buf[c * rq + QKV_TAIL:(c + 1) * rq, :] = qkv[:, lanes(c)]
    off_q = QKV_TAIL - (GDN_CONV - 1)
    for c in range(n_qkv):
        acc = cw_ref[0:1, lanes(c)] * qkv_buf[c * rq + off_q:c * rq + off_q + tl, :]
        for j in range(1, GDN_CONV):
            acc = acc + cw_ref[j:j + 1, lanes(c)] * qkv_buf[c * rq + off_q + j:c * rq + off_q + j + tl, :]
        a = _silu(acc)
        hd = c % GDN_HEADS
        if c < GDN_HEADS:
            a = a * lax.rsqrt(jnp.sum(a * a, axis=-1, keepdims=True) + L2_EPS) * (GDN_HEAD_DIM ** -0.5)
            q_ref[0, :, lanes(hd)] = a.astype(BF16)
        elif c < 2 * GDN_HEADS:
            a = a * lax.rsqrt(jnp.sum(a * a, axis=-1, keepdims=True) + L2_EPS)
            k_ref[0, :, lanes(hd)] = a.astype(BF16)
        else:
            v_ref[0, :, lanes(hd)] = a.astype(BF16)
    z_ref[0] = _dot(hb, win_ref[:, COL_Z:COL_GV]).astype(BF16)

    raw_c = _dot(hb, win_ref[:, COL_BG:D_PROJ_PAD])
    lane = lax.broadcasted_iota(I32, raw_c.shape, 1)
    neg_a_c = -jnp.exp(gpc_ref[0:1, :])
    g_c = neg_a_c * _softplus(raw_c + gpc_ref[1:2, :])
    bgc_ref[0] = jnp.where(lane < GDN_HEADS, _sigmoid(raw_c), g_c)
    raw_r = _dot_nt(wbgt_ref[...], hb)
    row = lax.broadcasted_iota(I32, raw_r.shape, 0)
    neg_a_r = -jnp.exp(gpr_ref[:, 0:1])
    g_r = neg_a_r * _softplus(raw_r + gpr_ref[:, 1:2])
    bgr_ref[0] = jnp.where(row < GDN_HEADS, _sigmoid(raw_r), g_r)

    gv = _dot(hb, win_ref[:, COL_GV:COL_GG])
    gg = _dot(hb, win_ref[:, COL_GG:COL_BG])
    u = gv * _sigmoid(gg)
    for c in range(n_u):
        u_buf[c * ru + DW_TAIL:(c + 1) * ru, :] = u[:, lanes(c)]
    off_u = DW_TAIL - (DW_CONV - 1)
    daccs = []
    for c in range(n_u):
        dacc = dwb_ref[:, lanes(c)] + dww_ref[0:1, lanes(c)] * u_buf[c * ru + off_u:c * ru + off_u + tl, :]
        for j in range(1, DW_CONV):
            dacc = dacc + dww_ref[j:j + 1, lanes(c)] * u_buf[c * ru + off_u + j:c * ru + off_u + j + tl, :]
        daccs.append(dacc)
    dacc = jnp.concatenate(daccs, axis=1)
    uc_ref[0] = _silu(_layer_norm(dacc) * cng_ref[...] + cnb_ref[...]).astype(BF16)

    for c in range(n_qkv):
        nq = qkv_buf[c * rq + tl:(c + 1) * rq, :]
        qkv_buf[c * rq:c * rq + QKV_TAIL, :] = nq
        nq_ref[0, :, lanes(c)] = nq
    for c in range(n_u):
        nd = u_buf[c * ru + tl:(c + 1) * ru, :]
        u_buf[c * ru:c * ru + DW_TAIL, :] = nd
        nd_ref[0, :, lanes(c)] = nd


def _front(x, mod, sq_pad, sd_pad, prm, tl):
    b, l, _ = x.shape
    nt = l // tl
    tok = lambda w, dt: (pl.BlockSpec((1, tl, w), lambda i, j: (i, j, 0)), jax.ShapeDtypeStruct((b, l, w), dt))
    full = lambda a: pl.BlockSpec(a.shape, lambda i, j: (0,) * a.ndim)
    outs = [tok(GDN_WIDTH, BF16), tok(GDN_WIDTH, BF16), tok(GDN_WIDTH, BF16), tok(GDN_WIDTH, BF16),
            tok(LANES, F32),
            (pl.BlockSpec((1, SUBLANES, tl), lambda i, j: (i, 0, j)), jax.ShapeDtypeStruct((b, SUBLANES, l), F32)),
            tok(CONV_WIDTH, BF16),
            (pl.BlockSpec((1, QKV_TAIL, QKV_WIDTH), lambda i, j: (i, 0, 0)),
             jax.ShapeDtypeStruct((b, QKV_TAIL, QKV_WIDTH), F32)),
            (pl.BlockSpec((1, DW_TAIL, CONV_WIDTH), lambda i, j: (i, 0, 0)),
             jax.ShapeDtypeStruct((b, DW_TAIL, CONV_WIDTH), F32))]
    consts = [prm["w_in_r"], prm["w_bgt"], prm["conv_qkv_w"], prm["dw_w"], prm["dw_b"], prm["cn_g"], prm["cn_b"],
              prm["gp_c"], prm["gp_r"]]
    return pl.pallas_call(
        functools.partial(_front_body, tl=tl),
        grid=(b, nt),
        in_specs=[pl.BlockSpec((1, tl, D_MODEL), lambda i, j: (i, j, 0)),
                  pl.BlockSpec((1, 6, D_MODEL), lambda i, j: (i, 0, 0))]
                 + [full(a) for a in consts]
                 + [pl.BlockSpec((1, QKV_TAIL, QKV_WIDTH), lambda i, j: (i, 0, 0)),
                    pl.BlockSpec((1, DW_TAIL, CONV_WIDTH), lambda i, j: (i, 0, 0))],
        out_specs=[o[0] for o in outs],
        out_shape=[o[1] for o in outs],
        scratch_shapes=[pltpu.VMEM((QKV_WIDTH // LANES * (QKV_TAIL + tl), LANES), F32),
                        pltpu.VMEM((CONV_WIDTH // LANES * (DW_TAIL + tl), LANES), F32)],
        compiler_params=_cparams(("arbitrary", "arbitrary")),
        name="mixer_front",
    )(x, mod, *consts, sq_pad, sd_pad)


def _split3_bf16(x):
    hi = x.astype(BF16)
    r1 = x - hi.astype(F32)
    mid = r1.astype(BF16)
    lo = (r1 - mid.astype(F32)).astype(BF16)
    return hi, mid, lo


def _tri_inverse_stages(ms, ri, ci, out):
    c = ms[0].shape[0]
    eye = jnp.where(ri == ci, 1.0, 0.0)
    pair = (ri >> 1) == (ci >> 1)
    ps = [eye - jnp.where(pair, m, 0.0) for m in ms]
    w = 2
    while w < c:
        s = w.bit_length() - 1
        sel = ((ri >> (s + 1)) == (ci >> (s + 1))) & ((ri >> s) > (ci >> s))
        pbs = [p.astype(BF16) for p in ps]
        xs = [_dot(pb, jnp.where(sel, m, 0.0).astype(BF16)).astype(BF16) for pb, m in zip(pbs, ms)]
        yield
        ps = [p - _dot(x, pb) for p, x, pb in zip(ps, xs, pbs)]
        yield
        w *= 2
    out.extend(ps)


def _gdn_body(q_ref, k_ref, v_ref, z_ref, bgc_ref, bgr_ref, s0_ref, gnw_ref, o_ref, sout_ref,
              s_scr, ku_s, kw_s, au_s, qe_s, egl_s, *, c, g, ncg):
    cg = pl.program_id(1)
    wset = cg % 2 if ncg > 1 else 0
    rset = 1 - wset if ncg > 1 else 0

    ri = lax.broadcasted_iota(I32, (c, c), 0)
    ci = lax.broadcasted_iota(I32, (c, c), 1)
    causal = ri >= ci
    strict = ri > ci
    lower = jnp.where(causal, 1.0, 0.0).astype(BF16)
    upper = jnp.where(ri <= ci, 1.0, 0.0).astype(BF16)
    gnw = gnw_ref[...]

    heads = range(GDN_HEADS)
    hcols = [slice(hd * GDN_HEAD_DIM, (hd + 1) * GDN_HEAD_DIM) for hd in heads]

    def intra_stages():
        prob = []
        for i in range(g):
            rows = pl.ds(i * c, c)
            bgc = bgc_ref[0, rows, :]
            bgr = bgr_ref[0, i]
            gc_c = sum(_dot(lower, part) for part in _split3_bf16(bgc))
            gc_r = sum(_dot(part, upper) for part in _split3_bf16(bgr))
            for hd in heads:
                gcc = gc_c[:, GDN_HEADS + hd:GDN_HEADS + hd + 1]
                gcr = gc_r[GDN_HEADS + hd:GDN_HEADS + hd + 1, :]
                prob.append(dict(i=i, hd=hd, rows=rows, cols=hcols[hd], beta=bgc[:, hd:hd + 1], gcc=gcc,
                                 decay=jnp.exp(jnp.where(causal, gcc - gcr, NEG_BIG))))
        yield
        for p in prob:
            p["kh"] = k_ref[0, p["rows"], p["cols"]]
            p["kb"] = p["kh"].astype(F32) * p["beta"]
        ms = [jnp.where(strict, _dot_nt(p["kb"].astype(BF16), p["kh"]) * p["decay"], 0.0) for p in prob]
        yield
        t_invs = []
        yield from _tri_inverse_stages(ms, ri, ci, t_invs)
        uws = []
        for p, t_inv in zip(prob, t_invs):
            p["egc"] = jnp.exp(p["gcc"])
            vf = v_ref[0, p["rows"], p["cols"]].astype(F32)
            rhs = jnp.concatenate([vf * p["beta"], p["kb"] * p["egc"]], axis=1).astype(BF16)
            uws.append(_dot(t_inv.astype(BF16), rhs).astype(BF16))
        yield
        for p in prob:
            p["qh"] = q_ref[0, p["rows"], p["cols"]]
            p["a"] = jnp.where(causal, _dot_nt(p["qh"], p["kh"]) * p["decay"], 0.0).astype(BF16)
            p["g_last"] = p["gcc"][c - 1:c, :]
            p["kd"] = (p["kh"].astype(F32) * jnp.exp(p["g_last"] - p["gcc"])).astype(BF16)
        yield
        kuws = [_dot_tn(p["kd"], uw) for p, uw in zip(prob, uws)]
        yield
        auws = [_dot(p["a"], uw) for p, uw in zip(prob, uws)]
        yield
        for p, kuw, auw in zip(prob, kuws, auws):
            rows, cols, hd = p["rows"], p["cols"], p["hd"]
            srows = pl.ds(p["i"] * GDN_HEAD_DIM, GDN_HEAD_DIM)
            ku_s[wset, srows, cols] = kuw[:, :GDN_HEAD_DIM]
            kw_s[wset, srows, cols] = kuw[:, GDN_HEAD_DIM:].astype(BF16)
            au_s[wset, rows, cols] = auw[:, :GDN_HEAD_DIM]
            qe_s[wset, rows, cols] = (p["qh"].astype(F32) * p["egc"] - auw[:, GDN_HEAD_DIM:]).astype(BF16)
            egl_s[wset, pl.ds(p["i"] * SUBLANES + hd, 1), :] = jnp.broadcast_to(jnp.exp(p["g_last"]), (1, LANES))

    def inter(i):
        rows = pl.ds(i * c, c)
        srows = pl.ds(i * GDN_HEAD_DIM, GDN_HEAD_DIM)
        ss = [s_scr[hd] for hd in heads]
        sbs = [s.astype(BF16) for s in ss]
        upd = [_dot(kw_s[rset, srows, hcols[hd]], sbs[hd]) for hd in heads]
        for hd in heads:
            s_scr[hd] = ss[hd] * egl_s[rset, pl.ds(i * SUBLANES + hd, 1), :] + (ku_s[rset, srows, hcols[hd]] - upd[hd])
        os_ = [_dot(qe_s[rset, rows, hcols[hd]], sbs[hd]) + au_s[rset, rows, hcols[hd]] for hd in heads]
        for hd in heads:
            o = os_[hd]
            gate = _silu(z_ref[0, rows, hcols[hd]].astype(F32))
            on = o * lax.rsqrt(jnp.mean(o * o, axis=-1, keepdims=True) + RMS_EPS) * gnw * gate
            o_ref[0, rows, hcols[hd]] = on.astype(BF16)

    def run(do_intra, do_inter):
        stages = intra_stages() if do_intra else iter(())
        todo = list(range(g)) if do_inter else []
        n_stages = 6 + 2 * (c.bit_length() - 2)
        every = max(1, n_stages // max(1, len(todo)))
        for n, _ in enumerate(stages):
            if todo and n % every == 0:
                inter(todo.pop(0))
        for i in todo:
            inter(i)

    if ncg == 1:
        s_scr[...] = s0_ref[0]
        run(True, False)
        run(False, True)
        sout_ref[0] = s_scr[...]
        return

    @pl.when(cg == 0)
    def _():
        s_scr[...] = s0_ref[0]
        run(True, False)

    @pl.when((cg > 0) & (cg < ncg))
    def _():
        run(True, True)

    @pl.when(cg == ncg)
    def _():
        run(False, True)
        sout_ref[0] = s_scr[...]


def _gdn(q, k, v, z, bgc, bgr, s0, gnw, c, g):
    b, l, _ = q.shape
    ncg = l // (c * g)
    cur = lambda w: pl.BlockSpec((1, c * g, w), lambda i, j: (i, jnp.minimum(j, ncg - 1), 0))
    prev = lambda w: pl.BlockSpec((1, c * g, w), lambda i, j: (i, jnp.maximum(j - 1, 0), 0))
    st = pl.BlockSpec((1, GDN_HEADS, GDN_HEAD_DIM, GDN_HEAD_DIM), lambda i, j: (i, 0, 0, 0))
    tok_buf = lambda dt: pltpu.VMEM((2, c * g, GDN_WIDTH), dt)
    state_buf = lambda dt: pltpu.VMEM((2, g * GDN_HEAD_DIM, GDN_WIDTH), dt)
    return pl.pallas_call(
        functools.partial(_gdn_body, c=c, g=g, ncg=ncg),
        grid=(b, ncg + 1 if ncg > 1 else 1),
        in_specs=[cur(GDN_WIDTH), cur(GDN_WIDTH), cur(GDN_WIDTH), prev(GDN_WIDTH), cur(LANES),
                  pl.BlockSpec((1, g, SUBLANES, c), lambda i, j: (i, jnp.minimum(j, ncg - 1), 0, 0)), st,
                  pl.BlockSpec((1, GDN_HEAD_DIM), lambda i, j: (0, 0))],
        out_specs=[prev(GDN_WIDTH), st],
        out_shape=[jax.ShapeDtypeStruct((b, l, GDN_WIDTH), BF16), jax.ShapeDtypeStruct(s0.shape, F32)],
        scratch_shapes=[pltpu.VMEM((GDN_HEADS, GDN_HEAD_DIM, GDN_HEAD_DIM), F32),
                        state_buf(F32), state_buf(BF16), tok_buf(F32), tok_buf(BF16),
                        pltpu.VMEM((2, g * SUBLANES, LANES), F32)],
        compiler_params=_cparams(("arbitrary", "arbitrary")),
        name="gdn",
    )(q, k, v, z, bgc, bgr, s0, gnw)


HALF = D_MODEL // 2
HI_MASK = 0xFFFF0000


def _pack_bf16_pairs(x):
    bits = pltpu.bitcast(x.astype(BF16).astype(F32), U32)
    return (bits[:, :HALF] >> 16) | (bits[:, HALF:] & jnp.uint32(HI_MASK))


def _unpack_bf16_pairs(p):
    return pltpu.bitcast(p << 16, F32), pltpu.bitcast(p & jnp.uint32(HI_MASK), F32)


ROW_TILES = HALF // LANES


def _row_lines(r):
    return pl.ds(r * ROW_TILES, ROW_TILES)


def _store_rows(ref, lead, r0, packed):
    n = packed.shape[0]
    for q in range(ROW_TILES):
        ref[(*lead, pl.ds(r0 * ROW_TILES + q, n, stride=ROW_TILES), slice(None))] = packed[:, q * LANES:(q + 1) * LANES]


def _load_rows(ref, lead, r0, n):
    return jnp.concatenate(
        [ref[(*lead, pl.ds(r0 * ROW_TILES + q, n, stride=ROW_TILES), slice(None))] for q in range(ROW_TILES)], axis=1)


def _mid_body(o_ref, uc_ref, x_ref, gt1_ref, sh2_ref, sc2_ref, gt2_ref,
              wout_ref, g1_ref, b1_ref, wsgu_ref, wsd_ref, wrh_ref, wrl_ref, pre_ref, h2p_ref, lgt_ref):
    gt1, sh2, sc2, gt2 = gt1_ref[0], sh2_ref[0], sc2_ref[0], gt2_ref[0]
    mix = _dot(jnp.concatenate([o_ref[0], uc_ref[0]], axis=1), wout_ref[...])
    x1 = _layer_norm(ALPHA * x_ref[0] + (1.0 + gt1) * mix) * g1_ref[...] + b1_ref[...]
    h2 = _layer_norm(x1) * (1.0 + sc2) + sh2
    hh, hl = _split_bf16(h2)
    lgt_ref[...] = _dot_nt(wrh_ref[...], hh) + (_dot_nt(wrh_ref[...], hl) + _dot_nt(wrl_ref[...], hh))
    gu = _dot(hh, wsgu_ref[...])
    act = _silu(gu[:, :D_SHARED]) * gu[:, D_SHARED:]
    shared = _dot(act.astype(BF16), wsd_ref[...])
    pre_ref[0] = ALPHA * x1 + (1.0 + gt2) * shared
    _store_rows(h2p_ref, (0,), 0, _pack_bf16_pairs(h2))


def _mod_spec(m, tl):
    if m.shape[1] == 1:
        return pl.BlockSpec((1, 1, D_MODEL), lambda i, j: (i, 0, 0))
    return pl.BlockSpec((1, tl, D_MODEL), lambda i, j: (i, j, 0))


def _mid(o, uc, x, mods, prm, tl):
    b, l, _ = x.shape
    nt = l // tl
    full = lambda a: pl.BlockSpec(a.shape, lambda i, j: (0,) * a.ndim)
    consts = [prm["w_out"], prm["ln1_g"], prm["ln1_b"], prm["ws_gu"], prm["ws_down"], prm["wr_hi"], prm["wr_lo"]]
    return pl.pallas_call(
        _mid_body,
        grid=(b, nt),
        in_specs=[pl.BlockSpec((1, tl, GDN_WIDTH), lambda i, j: (i, j, 0)),
                  pl.BlockSpec((1, tl, CONV_WIDTH), lambda i, j: (i, j, 0)),
                  pl.BlockSpec((1, tl, D_MODEL), lambda i, j: (i, j, 0))]
                 + [_mod_spec(m, tl) for m in mods] + [full(a) for a in consts],
        out_specs=[pl.BlockSpec((1, tl, D_MODEL), lambda i, j: (i, j, 0)),
                   pl.BlockSpec((1, tl * ROW_TILES, LANES), lambda i, j: (i, j, 0)),
                   pl.BlockSpec((N_EXPERTS, tl), lambda i, j: (0, i * nt + j))],
        out_shape=[jax.ShapeDtypeStruct((b, l, D_MODEL), F32), jax.ShapeDtypeStruct((b, l * ROW_TILES, LANES), U32),
                   jax.ShapeDtypeStruct((N_EXPERTS, b * l), F32)],
        compiler_params=_cparams(("arbitrary", "arbitrary")),
        name="mixer_out",
    )(o, uc, x, *mods, *consts)


def _first_max(x, row, n):
    m = jnp.max(x, axis=0, keepdims=True)
    ix = jnp.min(jnp.where(x == m, row, float(n)), axis=0, keepdims=True)
    return m, ix


def _route_body(lgp_ref, lgs_ref, bias_ref, idx_ref, w_ref, rank_ref, cnt_ref, cnt_scr, *, tr, n_prompt_tiles):
    @pl.when(pl.program_id(0) == 0)
    def _():
        cnt_scr[...] = jnp.zeros_like(cnt_scr)

    neg = -jnp.inf
    logits = jnp.where(pl.program_id(0) < n_prompt_tiles, lgp_ref[...], lgs_ref[...])
    scores = _sigmoid(logits)
    sel = scores + bias_ref[:, 0:1]
    row_g = lax.broadcasted_iota(I32, (GROUP_SIZE, tr), 0).astype(F32)
    gs = []
    for g in range(N_GROUPS):
        blk = sel[g * GROUP_SIZE:(g + 1) * GROUP_SIZE, :]
        m1, i1 = _first_max(blk, row_g, GROUP_SIZE)
        m2 = jnp.max(jnp.where(row_g == i1, neg, blk), axis=0, keepdims=True)
        gs.append(m1 + m2)
    gs = jnp.concatenate(gs, axis=0)
    row_n = lax.broadcasted_iota(I32, (N_GROUPS, tr), 0).astype(F32)
    chosen = jnp.zeros((N_GROUPS, tr), F32)
    for _ in range(TOPK_GROUPS):
        _, ix = _first_max(gs, row_n, N_GROUPS)
        hit = row_n == ix
        chosen = jnp.where(hit, 1.0, chosen)
        gs = jnp.where(hit, neg, gs)
    selm = jnp.concatenate(
        [jnp.where(chosen[g:g + 1, :] > 0.5, sel[g * GROUP_SIZE:(g + 1) * GROUP_SIZE, :], neg) for g in range(N_GROUPS)],
        axis=0)
    row_e = lax.broadcasted_iota(I32, (N_EXPERTS, tr), 0).astype(F32)
    idxs, ws = [], []
    picked = jnp.zeros((N_EXPERTS, tr), F32)
    for _ in range(TOP_K):
        _, ix = _first_max(selm, row_e, N_EXPERTS)
        hit = row_e == ix
        ws.append(jnp.sum(jnp.where(hit, scores, 0.0), axis=0, keepdims=True))
        idxs.append(ix)
        selm = jnp.where(hit, neg, selm)
        picked = jnp.where(hit, 1.0, picked)
    wsum = ws[0]
    for wk in ws[1:]:
        wsum = wsum + wk
    idx_ref[...] = jnp.concatenate(idxs, axis=0).astype(I32)
    w_ref[...] = jnp.concatenate(ws, axis=0) / wsum * ROUTED_SCALE
    ti = lax.broadcasted_iota(I32, (tr, tr), 0)
    tj = lax.broadcasted_iota(I32, (tr, tr), 1)
    before = _dot(picked.astype(BF16), jnp.where(ti < tj, 1.0, 0.0).astype(BF16)) + cnt_scr[:, 0:1]
    rank_ref[...] = jnp.concatenate(
        [jnp.sum(jnp.where(row_e == ix, before, 0.0), axis=0, keepdims=True) for ix in idxs], axis=0).astype(I32)
    cnt_scr[...] = cnt_scr[...] + jnp.sum(picked, axis=1, keepdims=True)
    cnt_ref[...] = cnt_scr[...]


def _route(logits_p, logits_s, bias_b, tr):
    ntp, nts = logits_p.shape[1] // tr, logits_s.shape[1] // tr
    t = (ntp + nts) * tr
    kt = lambda dt: (pl.BlockSpec((TOP_K, tr), lambda i: (0, i)), jax.ShapeDtypeStruct((TOP_K, t), dt))
    outs = [kt(I32), kt(F32), kt(I32),
            (pl.BlockSpec((N_EXPERTS, LANES), lambda i: (0, 0)), jax.ShapeDtypeStruct((N_EXPERTS, LANES), F32))]
    return pl.pallas_call(
        functools.partial(_route_body, tr=tr, n_prompt_tiles=ntp),
        grid=(ntp + nts,),
        in_specs=[pl.BlockSpec((N_EXPERTS, tr), lambda i: (0, jnp.minimum(i, ntp - 1))),
                  pl.BlockSpec((N_EXPERTS, tr), lambda i: (0, jnp.maximum(i - ntp, 0))),
                  pl.BlockSpec((N_EXPERTS, LANES), lambda i: (0, 0))],
        out_specs=[o[0] for o in outs],
        out_shape=[o[1] for o in outs],
        scratch_shapes=[pltpu.VMEM((N_EXPERTS, LANES), F32)],
        compiler_params=_cparams(("arbitrary",)),
        name="route",
    )(logits_p, logits_s, bias_b)


def _pos_body(idx_ref, rank_ref, cnt_ref, pos_ref, *, tr):
    ei = lax.broadcasted_iota(I32, (N_EXPERTS, N_EXPERTS), 0)
    ej = lax.broadcasted_iota(I32, (N_EXPERTS, N_EXPERTS), 1)
    below = jnp.where(ej < ei, 1.0, 0.0).astype(BF16)
    start = sum(_dot(below, part) for part in _split3_bf16(cnt_ref[...]))[:, 0:1]
    row_e = lax.broadcasted_iota(I32, (N_EXPERTS, tr), 0)
    pos_ref[...] = jnp.concatenate(
        [jnp.sum(jnp.where(row_e == idx_ref[k:k + 1, :], start, 0.0), axis=0, keepdims=True) for k in range(TOP_K)],
        axis=0).astype(I32) + rank_ref[...]


def _positions(idx, rank, cnt, tr):
    t = idx.shape[1]
    kt = pl.BlockSpec((TOP_K, tr), lambda i: (0, i))
    return pl.pallas_call(
        functools.partial(_pos_body, tr=tr),
        grid=(t // tr,),
        in_specs=[kt, kt, pl.BlockSpec((N_EXPERTS, LANES), lambda i: (0, 0))],
        out_specs=kt,
        out_shape=jax.ShapeDtypeStruct((TOP_K, t), I32),
        compiler_params=_cparams(("arbitrary",)),
        name="positions",
    )(idx, rank, cnt)


def _dispatch_body(pos_ref, hp_ref, hs_ref, xs_ref, sem, *, tt, n_prompt_tiles):
    i = pl.program_id(0)

    def scatter(src_ref):
        def body(t, carry):
            for k in range(TOP_K):
                pltpu.make_async_copy(src_ref.at[_row_lines(t)], xs_ref.at[_row_lines(pos_ref[0, 0, t * TOP_K + k])],
                                      sem).start(priority=k % 2)
            return carry
        lax.fori_loop(0, tt, body, 0)
        for _ in range(TOP_K):
            pltpu.make_async_copy(src_ref, xs_ref.at[pl.ds(0, tt * ROW_TILES)], sem).wait()

    @pl.when(i < n_prompt_tiles)
    def _():
        scatter(hp_ref)

    @pl.when(i >= n_prompt_tiles)
    def _():
        scatter(hs_ref)


def _dispatch(pos_tiles, h_prompt, h_sample, tt):
    tp, ts = h_prompt.shape[0] // ROW_TILES, h_sample.shape[0] // ROW_TILES
    ntp, nts = tp // tt, ts // tt
    n_rows = TOP_K * (tp + ts)
    return pl.pallas_call(
        functools.partial(_dispatch_body, tt=tt, n_prompt_tiles=ntp),
        grid=(ntp + nts,),
        in_specs=[pl.BlockSpec((1, 1, tt * TOP_K), lambda i: (i, 0, 0), memory_space=pltpu.SMEM),
                  pl.BlockSpec((tt * ROW_TILES, LANES), lambda i: (jnp.minimum(i, ntp - 1), 0)),
                  pl.BlockSpec((tt * ROW_TILES, LANES), lambda i: (jnp.maximum(i - ntp, 0), 0))],
        out_specs=pl.BlockSpec(memory_space=pl.ANY),
        out_shape=jax.ShapeDtypeStruct((n_rows * ROW_TILES, LANES), U32),
        scratch_shapes=[pltpu.SemaphoreType.DMA(())],
        compiler_params=_cparams(("arbitrary",)),
        name="dispatch",
    )(pos_tiles, h_prompt, h_sample)


GMM_ROWS = 2048
GMM_SUB = 512


def _gmm_body(ve_ref, vb_ref, vlo_ref, vhi_ref, vfirst_ref, vnew_ref,
              xs_ref, wg_ref, wu_ref, wd_ref, ys_ref, wgu_s, wd_s):
    v = pl.program_id(0)
    lo = vlo_ref[v]
    hi = vhi_ref[v]

    @pl.when(vnew_ref[v] == 1)
    def _():
        wgu_s[:, :D_EXPERT] = wg_ref[0].astype(BF16)
        wgu_s[:, D_EXPERT:] = wu_ref[0].astype(BF16)
        wd_s[...] = wd_ref[0].astype(BF16)

    @pl.when(vfirst_ref[v] == 1)
    def _():
        ys_ref[...] = jnp.zeros_like(ys_ref)

    def sub_block(s, carry):
        r0 = pl.multiple_of(s * GMM_SUB, GMM_SUB)
        x_lo, x_hi = _unpack_bf16_pairs(_load_rows(xs_ref, (), r0, GMM_SUB))
        gu = _dot(x_lo.astype(BF16), wgu_s[:HALF, :]) + _dot(x_hi.astype(BF16), wgu_s[HALF:, :])
        act = _silu(gu[:, :D_EXPERT]) * gu[:, D_EXPERT:]
        y = _pack_bf16_pairs(_dot(act.astype(BF16), wd_s[...]))
        row = lax.broadcasted_iota(I32, y.shape, 0) + r0
        _store_rows(ys_ref, (), r0, jnp.where((row >= lo) & (row < hi), y, _load_rows(ys_ref, (), r0, GMM_SUB)))
        return carry

    lax.fori_loop(lo // GMM_SUB, (hi + GMM_SUB - 1) // GMM_SUB, sub_block, 0)


def _gmm_schedule(counts, n_rows):
    nb = n_rows // GMM_ROWS
    n_vis = nb + N_EXPERTS
    ends = jnp.cumsum(counts)
    starts = ends - counts
    first_blk = starts // GMM_ROWS
    last_blk = jnp.maximum(ends - 1, 0) // GMM_ROWS
    per_e = jnp.where(counts > 0, last_blk - first_blk + 1, 0)
    vis_end = jnp.cumsum(per_e)
    total = vis_end[-1]
    v = jnp.minimum(jnp.arange(n_vis, dtype=I32), total - 1)
    e = jnp.minimum(jnp.sum((vis_end[None, :] <= v[:, None]).astype(I32), axis=1), N_EXPERTS - 1)
    table = jnp.stack([first_blk, vis_end - per_e, starts, ends], axis=1).astype(I32)
    pick = e[:, None] == jnp.arange(N_EXPERTS, dtype=I32)[None, :]
    fb, v0, st, en = jnp.moveaxis(jnp.sum(jnp.where(pick[:, :, None], table[None], 0), axis=1), 1, 0)
    blk = fb + (v - v0)
    lo = jnp.maximum(st, blk * GMM_ROWS) - blk * GMM_ROWS
    hi = jnp.minimum(en, (blk + 1) * GMM_ROWS) - blk * GMM_ROWS
    hi = jnp.where(jnp.arange(n_vis) < total, hi, lo)
    prev = lambda a: jnp.concatenate([jnp.full((1,), -1, I32), a[:-1]])
    first = (blk != prev(blk)).astype(I32)
    new_e = (e != prev(e)).astype(I32)
    return e, blk, lo.astype(I32), hi.astype(I32), first, new_e


def _gmm(xs, counts, we_gate, we_up, we_down):
    n_rows = xs.shape[0] // ROW_TILES
    sched = _gmm_schedule(counts, n_rows)
    n_vis = sched[0].shape[0]
    grid_spec = pltpu.PrefetchScalarGridSpec(
        num_scalar_prefetch=len(sched),
        grid=(n_vis,),
        in_specs=[pl.BlockSpec((GMM_ROWS * ROW_TILES, LANES), lambda v, ve, vb, *_: (vb[v], 0)),
                  pl.BlockSpec((1, D_MODEL, D_EXPERT), lambda v, ve, *_: (ve[v], 0, 0)),
                  pl.BlockSpec((1, D_MODEL, D_EXPERT), lambda v, ve, *_: (ve[v], 0, 0)),
                  pl.BlockSpec((1, D_EXPERT, D_MODEL), lambda v, ve, *_: (ve[v], 0, 0))],
        out_specs=pl.BlockSpec((GMM_ROWS * ROW_TILES, LANES), lambda v, ve, vb, *_: (vb[v], 0)),
        scratch_shapes=[pltpu.VMEM((D_MODEL, 2 * D_EXPERT), BF16), pltpu.VMEM((D_EXPERT, D_MODEL), BF16)])
    return pl.pallas_call(
        _gmm_body,
        grid_spec=grid_spec,
        out_shape=jax.ShapeDtypeStruct(xs.shape, U32),
        compiler_params=_cparams(("arbitrary",)),
        name="expert_ffn",
    )(*sched, xs, we_gate, we_up, we_down)


def _combine_body(pos_ref, posn_ref, w_ref, pre_ref, gt2_ref, g2_ref, b2_ref, ys_ref, out_ref, gbuf, sem,
                  *, tt, n_steps):
    step = pl.program_id(0) * pl.num_programs(1) + pl.program_id(1)
    slot = step % 2

    def gather(p_ref, into):
        def body(t, carry):
            for k in range(TOP_K):
                pltpu.make_async_copy(ys_ref.at[_row_lines(p_ref[0, 0, t * TOP_K + k])],
                                      gbuf.at[into, k, _row_lines(t)], sem.at[into]).start(priority=k % 2)
            return carry
        lax.fori_loop(0, tt, body, 0)

    @pl.when(step == 0)
    def _():
        gather(pos_ref, 0)

    @pl.when(step + 1 < n_steps)
    def _():
        gather(posn_ref, 1 - slot)

    pltpu.make_async_copy(gbuf.at[slot], gbuf.at[slot], sem.at[slot]).wait()
    acc_lo = jnp.zeros((tt, HALF), F32)
    acc_hi = jnp.zeros((tt, HALF), F32)
    for k in range(TOP_K):
        y_lo, y_hi = _unpack_bf16_pairs(_load_rows(gbuf, (slot, k), 0, tt))
        wk = w_ref[:, k:k + 1]
        acc_lo = acc_lo + wk * y_lo
        acc_hi = acc_hi + wk * y_hi
    routed = jnp.concatenate([acc_lo, acc_hi], axis=1)
    y = pre_ref[0] + (1.0 + gt2_ref[0]) * routed
    out_ref[0] = _layer_norm(y) * g2_ref[...] + b2_ref[...]


def _combine(pos_tiles, w_tk, pre, gt2, ln2_g, ln2_b, ys, tt, tok0):
    b, l, _ = pre.shape
    nt = l // tt
    blk0 = tok0 // tt
    last = blk0 + b * nt - 1
    pos_spec = lambda nxt: pl.BlockSpec((1, 1, tt * TOP_K),
                                        lambda i, j: (jnp.minimum(blk0 + i * nt + j + nxt, last), 0, 0),
                                        memory_space=pltpu.SMEM)
    return pl.pallas_call(
        functools.partial(_combine_body, tt=tt, n_steps=b * nt),
        grid=(b, nt),
        in_specs=[pos_spec(0), pos_spec(1),
                  pl.BlockSpec((tt, TOP_K), lambda i, j: (blk0 + i * nt + j, 0)),
                  pl.BlockSpec((1, tt, D_MODEL), lambda i, j: (i, j, 0)),
                  _mod_spec(gt2, tt),
                  pl.BlockSpec((1, D_MODEL), lambda i, j: (0, 0)),
                  pl.BlockSpec((1, D_MODEL), lambda i, j: (0, 0)),
                  pl.BlockSpec(memory_space=pl.ANY)],
        out_specs=pl.BlockSpec((1, tt, D_MODEL), lambda i, j: (i, j, 0)),
        out_shape=jax.ShapeDtypeStruct((b, l, D_MODEL), F32),
        scratch_shapes=[pltpu.VMEM((2, TOP_K, tt * ROW_TILES, LANES), U32), pltpu.SemaphoreType.DMA((2,))],
        compiler_params=_cparams(("arbitrary", "arbitrary")),
        name="combine",
    )(pos_tiles, pos_tiles, w_tk, pre, gt2, ln2_g, ln2_b, ys)


def _prep_params(w_in, conv_qkv_w, a_log, dt_bias, dw_w, dw_b, cn_g, cn_b):
    z0 = QKV_WIDTH
    b0 = z0 + GDN_WIDTH
    g0 = b0 + 2 * GDN_HEADS
    w_bg = w_in[:, b0:g0]
    w_in_r = jnp.concatenate(
        [w_in[:, :b0], w_in[:, g0:], w_bg, jnp.zeros((D_MODEL, LANES - 2 * GDN_HEADS), w_in.dtype)], axis=1).astype(BF16)
    pad_h = jnp.zeros((GDN_HEADS,), F32)
    al = jnp.concatenate([pad_h, a_log.astype(F32)])
    db = jnp.concatenate([pad_h, dt_bias.astype(F32)])
    gp_c = jnp.zeros((SUBLANES, LANES), F32).at[0, :2 * GDN_HEADS].set(al).at[1, :2 * GDN_HEADS].set(db)
    gp_r = jnp.zeros((SUBLANES, LANES), F32).at[:, 0].set(al).at[:, 1].set(db)
    return dict(
        w_in_r=w_in_r, w_bgt=w_bg.T.astype(BF16), conv_qkv_w=conv_qkv_w.astype(F32),
        dw_w=jnp.concatenate([dw_w, jnp.zeros((1, CONV_WIDTH), dw_w.dtype)], axis=0).astype(F32),
        dw_b=dw_b.reshape(1, -1).astype(F32), cn_g=cn_g.reshape(1, -1).astype(F32), cn_b=cn_b.reshape(1, -1).astype(F32),
        gp_c=gp_c, gp_r=gp_r)


def _prep_mid_params(w_out, ln1_g, ln1_b, w_router, router_bias, ws_gate, ws_up, ws_down, ln2_g, ln2_b, gdn_norm_w):
    row = lambda a: a.reshape(1, -1).astype(F32)
    wr_t = w_router.astype(F32).T
    wr_hi = wr_t.astype(BF16)
    return dict(
        w_out=w_out.astype(BF16), ln1_g=row(ln1_g), ln1_b=row(ln1_b),
        ws_gu=jnp.concatenate([ws_gate, ws_up], axis=1).astype(BF16), ws_down=ws_down.astype(BF16),
        wr_hi=wr_hi, wr_lo=(wr_t - wr_hi.astype(F32)).astype(BF16),
        bias_b=jnp.broadcast_to(router_bias.astype(F32)[:, None], (N_EXPERTS, LANES)),
        ln2_g=row(ln2_g), ln2_b=row(ln2_b), gnw=row(gdn_norm_w))


def _tile(n, pref):
    t = min(pref, n)
    while n % t:
        t //= 2
    return t


def _token_mixer(x, mod, s_gdn, s_qkv, s_dw, prm, mprm):
    b, l, _ = x.shape
    assert l >= DW_CONV - 1 and l % SUBLANES == 0
    tl = _tile(l, 256)
    sq_pad = jnp.pad(s_qkv.astype(F32), ((0, 0), (QKV_TAIL - (GDN_CONV - 1), 0), (0, 0)))
    sd_pad = jnp.pad(s_dw.astype(F32), ((0, 0), (DW_TAIL - (DW_CONV - 1), 0), (0, 0)))
    q, k, v, z, bgc, bgr, uc, nq, nd = _front(x, mod, sq_pad, sd_pad, prm, tl)
    c = min(CHUNK, l)
    n_chunks = l // c
    g = _tile(n_chunks, 8)
    bgr = bgr.reshape(b, SUBLANES, n_chunks, c).transpose(0, 2, 1, 3)
    o, s_new = _gdn(q, k, v, z, bgc, bgr, s_gdn.astype(F32), mprm["gnw"], c, g)
    return o, uc, s_new, nq[:, QKV_TAIL - (GDN_CONV - 1):], nd[:, DW_TAIL - (DW_CONV - 1):]


def kernel(x_prompt, x_sample, state_gdn, state_qkv_conv, state_dw_conv, c_prompt, c_sample, w_ada, b_ada, w_in, conv_qkv_w, a_log, dt_bias, gdn_norm_w, dw_w, dw_b, cn_g, cn_b, w_out, ln1_g, ln1_b, w_router, router_bias, we_gate, we_up, we_down, ws_gate, ws_up, ws_down, ln2_g, ln2_b):
    bp, lp, _ = x_prompt.shape
    bs, ls, _ = x_sample.shape
    tp, ts = bp * lp, bs * ls
    yp, ys = x_prompt, x_sample
    c_all = jnp.concatenate([c_prompt, c_sample], axis=0)
    new_p, new_s = [], []
    for l in range(w_ada.shape[0]):
        prm = _prep_params(w_in[l], conv_qkv_w[l], a_log[l], dt_bias[l], dw_w[l], dw_b[l], cn_g[l], cn_b[l])
        mprm = _prep_mid_params(w_out[l], ln1_g[l], ln1_b[l], w_router[l], router_bias[l], ws_gate[l], ws_up[l],
                                ws_down[l], ln2_g[l], ln2_b[l], gdn_norm_w[l])
        mod = _ada_mod(c_all, w_ada[l], b_ada[l]).reshape(bp + bs, 6, D_MODEL)
        mod_p, mod_s = mod[:bp], mod[bp:]

        zg = jnp.zeros((bp, GDN_HEADS, GDN_HEAD_DIM, GDN_HEAD_DIM), F32)
        zq = jnp.zeros((bp, GDN_CONV - 1, QKV_WIDTH), F32)
        zd = jnp.zeros((bp, DW_CONV - 1, CONV_WIDTH), F32)
        o_p, uc_p, g_p, q_p, d_p = _token_mixer(yp, mod_p, zg, zq, zd, prm, mprm)
        o_s, uc_s, g_s, q_s, d_s = _token_mixer(ys, mod_s, state_gdn[l], state_qkv_conv[l], state_dw_conv[l], prm, mprm)

        mods_p = tuple(mod_p[:, j:j + 1, :] for j in (2, 3, 4, 5))
        mods_s = tuple(jnp.repeat(mod_s[:, j, :], ls, axis=0)[None] for j in (2, 3, 4, 5))
        flat = lambda a: a.reshape(1, ts, a.shape[-1])
        pre_p, h_p, lg_p = _mid(o_p, uc_p, yp, mods_p, mprm, _tile(lp, 512))
        pre_s, h_s, lg_s = _mid(flat(o_s), flat(uc_s), flat(ys), mods_s, mprm, _tile(ts, 256))

        tt = _tile(ts, 512)
        assert tp % tt == 0 and lp % tt == 0 and (TOP_K * (tp + ts)) % GMM_ROWS == 0
        idx, w, rank, cnt = _route(lg_p, lg_s, mprm["bias_b"], tt)
        pos = _positions(idx, rank, cnt, tt)
        pos_tiles = pos.T.reshape((tp + ts) // tt, 1, tt * TOP_K)
        xs = _dispatch(pos_tiles, h_p.reshape(tp * ROW_TILES, LANES), h_s.reshape(ts * ROW_TILES, LANES), tt)
        ye = _gmm(xs, cnt[:, 0].astype(I32), we_gate[l], we_up[l], we_down[l])
        w_tk = w.T
        yp = _combine(pos_tiles, w_tk, pre_p, mods_p[3], mprm["ln2_g"], mprm["ln2_b"], ye, tt, 0)
        ys = _combine(pos_tiles, w_tk, pre_s, mods_s[3], mprm["ln2_g"], mprm["ln2_b"], ye, tt, tp
                      ).reshape(bs, ls, D_MODEL)
        new_p.append((g_p.astype(state_gdn.dtype), q_p.astype(x_prompt.dtype), d_p.astype(x_prompt.dtype)))
        new_s.append((g_s.astype(state_gdn.dtype), q_s.astype(state_qkv_conv.dtype), d_s.astype(state_dw_conv.dtype)))
    stack = lambda rows, j: jnp.stack([r[j] for r in rows])
    return (yp, ys, stack(new_p, 0), stack(new_p, 1), stack(new_p, 2), stack(new_s, 0), stack(new_s, 1), stack(new_s, 2))
```

```python
import functools

import jax
import jax.numpy as jnp
from jax import lax
from jax.experimental import pallas as pl
from jax.experimental.pallas import tpu as pltpu

F32 = jnp.float32
BF16 = jnp.bfloat16
I32 = jnp.int32
U32 = jnp.uint32

D_MODEL = 1024
GDN_WIDTH = 512
CONV_WIDTH = 512
GDN_HEAD_DIM = 128
GDN_HEADS = 4
QKV_WIDTH = 3 * GDN_WIDTH
GDN_CONV = 4
DW_CONV = 31
CHUNK = 64
N_EXPERTS = 256
TOP_K = 8
N_GROUPS = 8
GROUP_SIZE = N_EXPERTS // N_GROUPS
TOPK_GROUPS = 4
D_EXPERT = 256
D_SHARED = 256
ROUTED_SCALE = 2.5
LN_EPS = 1e-5
RMS_EPS = 1e-6
L2_EPS = 1e-6
DEPTH = 1
ALPHA = (2.0 * DEPTH) ** 0.25

LANES = 128
SUBLANES = 8
VMEM_LIMIT_BYTES = 56 * 1024 * 1024

COL_Z = QKV_WIDTH
COL_GV = COL_Z + GDN_WIDTH
COL_GG = COL_GV + CONV_WIDTH
COL_BG = COL_GG + CONV_WIDTH
D_PROJ_PAD = COL_BG + LANES
QKV_TAIL = SUBLANES
DW_TAIL = 32
NEG_BIG = -1e30


def _cparams(sem):
    return pltpu.CompilerParams(dimension_semantics=sem, vmem_limit_bytes=VMEM_LIMIT_BYTES)


def _split_bf16(x):
    hi = x.astype(BF16)
    lo = (x - hi.astype(F32)).astype(BF16)
    return hi, lo


def _dot(a, b):
    return jnp.dot(a, b, preferred_element_type=F32)


def _dot_nt(a, b):
    return lax.dot_general(a, b, (((1,), (1,)), ((), ())), preferred_element_type=F32)


def _dot_tn(a, b):
    return lax.dot_general(a, b, (((0,), (0,)), ((), ())), preferred_element_type=F32)


def _dot_hp(a, b):
    ah, al = _split_bf16(a)
    bh, bl = _split_bf16(b)
    return _dot(ah, bh) + (_dot(ah, bl) + _dot(al, bh))


def _sigmoid(x):
    return 1.0 / (1.0 + jnp.exp(-x))


def _silu(x):
    return x * _sigmoid(x)


def _softplus(x):
    return jnp.maximum(x, 0.0) + jnp.log(1.0 + jnp.exp(-jnp.abs(x)))


def _layer_norm(x):
    mu = jnp.mean(x, axis=-1, keepdims=True)
    xc = x - mu
    var = jnp.mean(xc * xc, axis=-1, keepdims=True)
    return xc * lax.rsqrt(var + LN_EPS)


def _ada_body(c_ref, w_ref, b_ref, o_ref):
    o_ref[...] = _dot_hp(_silu(c_ref[...]), w_ref[...]) + b_ref[...]


def _ada_mod(c, w_ada, b_ada):
    bt = c.shape[0]
    n_col = w_ada.shape[1] // D_MODEL
    return pl.pallas_call(
        _ada_body,
        grid=(n_col,),
        in_specs=[
            pl.BlockSpec((bt, D_MODEL), lambda j: (0, 0)),
            pl.BlockSpec((D_MODEL, D_MODEL), lambda j: (0, j)),
            pl.BlockSpec((1, D_MODEL), lambda j: (0, j)),
        ],
        out_specs=pl.BlockSpec((bt, D_MODEL), lambda j: (0, j)),
        out_shape=jax.ShapeDtypeStruct((bt, w_ada.shape[1]), F32),
        compiler_params=_cparams(("arbitrary",)),
        name="ada_mod",
    )(c, w_ada, b_ada.reshape(1, -1))


def _front_body(x_ref, mod_ref, win_ref, wbgt_ref, cw_ref, dww_ref, dwb_ref, cng_ref, cnb_ref,
                gpc_ref, gpr_ref, sq_ref, sd_ref,
                q_ref, k_ref, v_ref, z_ref, bgc_ref, bgr_ref, uc_ref, nq_ref, nd_ref,
                qkv_buf, u_buf, *, tl):
    t = pl.program_id(1)
    rq = QKV_TAIL + tl
    ru = DW_TAIL + tl
    n_qkv = QKV_WIDTH // LANES
    n_u = CONV_WIDTH // LANES
    lanes = lambda c: slice(c * LANES, (c + 1) * LANES)

    @pl.when(t == 0)
    def _():
        for c in range(n_qkv):
            qkv_buf[c * rq:c * rq + QKV_TAIL, :] = sq_ref[0, :, lanes(c)]
        for c in range(n_u):
            u_buf[c * ru:c * ru + DW_TAIL, :] = sd_ref[0, :, lanes(c)]

    sh1 = mod_ref[0, 0:1, :]
    sc1 = mod_ref[0, 1:2, :]
    h = _layer_norm(x_ref[0]) * (1.0 + sc1) + sh1
    hb = h.astype(BF16)

    qkv = _dot(hb, win_ref[:, 0:QKV_WIDTH])
    for c in range(n_qkv):
        qkv_buf[c * rq + QKV_TAIL:(c + 1) * rq, :] = qkv[:, lanes(c)]
    off_q = QKV_TAIL - (GDN_CONV - 1)
    for c in range(n_qkv):
        acc = cw_ref[0:1, lanes(c)] * qkv_buf[c * rq + off_q:c * rq + off_q + tl, :]
        for j in range(1, GDN_CONV):
            acc = acc + cw_ref[j:j + 1, lanes(c)] * qkv_buf[c * rq + off_q + j:c * rq + off_q + j + tl, :]
        a = _silu(acc)
        hd = c % GDN_HEADS
        if c < GDN_HEADS:
            a = a * lax.rsqrt(jnp.sum(a * a, axis=-1, keepdims=True) + L2_EPS) * (GDN_HEAD_DIM ** -0.5)
            q_ref[0, :, lanes(hd)] = a.astype(BF16)
        elif c < 2 * GDN_HEADS:
            a = a * lax.rsqrt(jnp.sum(a * a, axis=-1, keepdims=True) + L2_EPS)
            k_ref[0, :, lanes(hd)] = a.astype(BF16)
        else:
            v_ref[0, :, lanes(hd)] = a.astype(BF16)
    z_ref[0] = _dot(hb, win_ref[:, COL_Z:COL_GV]).astype(BF16)

    raw_c = _dot(hb, win_ref[:, COL_BG:D_PROJ_PAD])
    lane = lax.broadcasted_iota(I32, raw_c.shape, 1)
    neg_a_c = -jnp.exp(gpc_ref[0:1, :])
    g_c = neg_a_c * _softplus(raw_c + gpc_ref[1:2, :])
    bgc_ref[0] = jnp.where(lane < GDN_HEADS, _sigmoid(raw_c), g_c)
    raw_r = _dot_nt(wbgt_ref[...], hb)
    row = lax.broadcasted_iota(I32, raw_r.shape, 0)
    neg_a_r = -jnp.exp(gpr_ref[:, 0:1])
    g_r = neg_a_r * _softplus(raw_r + gpr_ref[:, 1:2])
    bgr_ref[0] = jnp.where(row < GDN_HEADS, _sigmoid(raw_r), g_r)

    gv = _dot(hb, win_ref[:, COL_GV:COL_GG])
    gg = _dot(hb, win_ref[:, COL_GG:COL_BG])
    u = gv * _sigmoid(gg)
    for c in range(n_u):
        u_buf[c * ru + DW_TAIL:(c + 1) * ru, :] = u[:, lanes(c)]
    off_u = DW_TAIL - (DW_CONV - 1)
    daccs = []
    for c in range(n_u):
        dacc = dwb_ref[:, lanes(c)] + dww_ref[0:1, lanes(c)] * u_buf[c * ru + off_u:c * ru + off_u + tl, :]
        for j in range(1, DW_CONV):
            dacc = dacc + dww_ref[j:j + 1, lanes(c)] * u_buf[c * ru + off_u + j:c * ru + off_u + j + tl, :]
        daccs.append(dacc)
    dacc = jnp.concatenate(daccs, axis=1)
    uc_ref[0] = _silu(_layer_norm(dacc) * cng_ref[...] + cnb_ref[...]).astype(BF16)

    for c in range(n_qkv):
        nq = qkv_buf[c * rq + tl:(c + 1) * rq, :]
        qkv_buf[c * rq:c * rq + QKV_TAIL, :] = nq
        nq_ref[0, :, lanes(c)] = nq
    for c in range(n_u):
        nd = u_buf[c * ru + tl:(c + 1) * ru, :]
        u_buf[c * ru:c * ru + DW_TAIL, :] = nd
        nd_ref[0, :, lanes(c)] = nd


def _front(x, mod, sq_pad, sd_pad, prm, tl):
    b, l, _ = x.shape
    nt = l // tl
    tok = lambda w, dt: (pl.BlockSpec((1, tl, w), lambda i, j: (i, j, 0)), jax.ShapeDtypeStruct((b, l, w), dt))
    full = lambda a: pl.BlockSpec(a.shape, lambda i, j: (0,) * a.ndim)
    outs = [tok(GDN_WIDTH, BF16), tok(GDN_WIDTH, BF16), tok(GDN_WIDTH, BF16), tok(GDN_WIDTH, BF16),
            tok(LANES, F32),
            (pl.BlockSpec((1, SUBLANES, tl), lambda i, j: (i, 0, j)), jax.ShapeDtypeStruct((b, SUBLANES, l), F32)),
            tok(CONV_WIDTH, BF16),
            (pl.BlockSpec((1, QKV_TAIL, QKV_WIDTH), lambda i, j: (i, 0, 0)),
             jax.ShapeDtypeStruct((b, QKV_TAIL, QKV_WIDTH), F32)),
            (pl.BlockSpec((1, DW_TAIL, CONV_WIDTH), lambda i, j: (i, 0, 0)),
             jax.ShapeDtypeStruct((b, DW_TAIL, CONV_WIDTH), F32))]
    consts = [prm["w_in_r"], prm["w_bgt"], prm["conv_qkv_w"], prm["dw_w"], prm["dw_b"], prm["cn_g"], prm["cn_b"],
              prm["gp_c"], prm["gp_r"]]
    return pl.pallas_call(
        functools.partial(_front_body, tl=tl),
        grid=(b, nt),
        in_specs=[pl.BlockSpec((1, tl, D_MODEL), lambda i, j: (i, j, 0)),
                  pl.BlockSpec((1, 6, D_MODEL), lambda i, j: (i, 0, 0))]
                 + [full(a) for a in consts]
                 + [pl.BlockSpec((1, QKV_TAIL, QKV_WIDTH), lambda i, j: (i, 0, 0)),
                    pl.BlockSpec((1, DW_TAIL, CONV_WIDTH), lambda i, j: (i, 0, 0))],
        out_specs=[o[0] for o in outs],
        out_shape=[o[1] for o in outs],
        scratch_shapes=[pltpu.VMEM((QKV_WIDTH // LANES * (QKV_TAIL + tl), LANES), F32),
                        pltpu.VMEM((CONV_WIDTH // LANES * (DW_TAIL + tl), LANES), F32)],
        compiler_params=_cparams(("arbitrary", "arbitrary")),
        name="mixer_front",
    )(x, mod, *consts, sq_pad, sd_pad)


def _split3_bf16(x):
    hi = x.astype(BF16)
    r1 = x - hi.astype(F32)
    mid = r1.astype(BF16)
    lo = (r1 - mid.astype(F32)).astype(BF16)
    return hi, mid, lo


def _tri_inverse_stages(ms, ri, ci, out):
    c = ms[0].shape[0]
    eye = jnp.where(ri == ci, 1.0, 0.0)
    pair = (ri >> 1) == (ci >> 1)
    ps = [eye - jnp.where(pair, m, 0.0) for m in ms]
    w = 2
    while w < c:
        s = w.bit_length() - 1
        sel = ((ri >> (s + 1)) == (ci >> (s + 1))) & ((ri >> s) > (ci >> s))
        pbs = [p.astype(BF16) for p in ps]
        xs = [_dot(pb, jnp.where(sel, m, 0.0).astype(BF16)).astype(BF16) for pb, m in zip(pbs, ms)]
        yield
        ps = [p - _dot(x, pb) for p, x, pb in zip(ps, xs, pbs)]
        yield
        w *= 2
    out.extend(ps)


def _gdn_body(q_ref, k_ref, v_ref, z_ref, bgc_ref, bgr_ref, s0_ref, gnw_ref, o_ref, sout_ref,
              s_scr, ku_s, kw_s, au_s, qe_s, egl_s, *, c, g, ncg):
    cg = pl.program_id(1)
    wset = cg % 2 if ncg > 1 else 0
    rset = 1 - wset if ncg > 1 else 0

    ri = lax.broadcasted_iota(I32, (c, c), 0)
    ci = lax.broadcasted_iota(I32, (c, c), 1)
    causal = ri >= ci
    strict = ri > ci
    lower = jnp.where(causal, 1.0, 0.0).astype(BF16)
    upper = jnp.where(ri <= ci, 1.0, 0.0).astype(BF16)
    gnw = gnw_ref[...]

    heads = range(GDN_HEADS)
    hcols = [slice(hd * GDN_HEAD_DIM, (hd + 1) * GDN_HEAD_DIM) for hd in heads]

    def intra_stages():
        prob = []
        for i in range(g):
            rows = pl.ds(i * c, c)
            bgc = bgc_ref[0, rows, :]
            bgr = bgr_ref[0, i]
            gc_c = sum(_dot(lower, part) for part in _split3_bf16(bgc))
            gc_r = sum(_dot(part, upper) for part in _split3_bf16(bgr))
            for hd in heads:
                gcc = gc_c[:, GDN_HEADS + hd:GDN_HEADS + hd + 1]
                gcr = gc_r[GDN_HEADS + hd:GDN_HEADS + hd + 1, :]
                prob.append(dict(i=i, hd=hd, rows=rows, cols=hcols[hd], beta=bgc[:, hd:hd + 1], gcc=gcc,
                                 decay=jnp.exp(jnp.where(causal, gcc - gcr, NEG_BIG))))
        yield
        for p in prob:
            p["kh"] = k_ref[0, p["rows"], p["cols"]]
            p["kb"] = p["kh"].astype(F32) * p["beta"]
        ms = [jnp.where(strict, _dot_nt(p["kb"].astype(BF16), p["kh"]) * p["decay"], 0.0) for p in prob]
        yield
        t_invs = []
        yield from _tri_inverse_stages(ms, ri, ci, t_invs)
        uws = []
        for p, t_inv in zip(prob, t_invs):
            p["egc"] = jnp.exp(p["gcc"])
            vf = v_ref[0, p["rows"], p["cols"]].astype(F32)
            rhs = jnp.concatenate([vf * p["beta"], p["kb"] * p["egc"]], axis=1).astype(BF16)
            uws.append(_dot(t_inv.astype(BF16), rhs).astype(BF16))
        yield
        for p in prob:
            p["qh"] = q_ref[0, p["rows"], p["cols"]]
            p["a"] = jnp.where(causal, _dot_nt(p["qh"], p["kh"]) * p["decay"], 0.0).astype(BF16)
            p["g_last"] = p["gcc"][c - 1:c, :]
            p["kd"] = (p["kh"].astype(F32) * jnp.exp(p["g_last"] - p["gcc"])).astype(BF16)
        yield
        kuws = [_dot_tn(p["kd"], uw) for p, uw in zip(prob, uws)]
        yield
        auws = [_dot(p["a"], uw) for p, uw in zip(prob, uws)]
        yield
        for p, kuw, auw in zip(prob, kuws, auws):
            rows, cols, hd = p["rows"], p["cols"], p["hd"]
            srows = pl.ds(p["i"] * GDN_HEAD_DIM, GDN_HEAD_DIM)
            ku_s[wset, srows, cols] = kuw[:, :GDN_HEAD_DIM]
            kw_s[wset, srows, cols] = kuw[:, GDN_HEAD_DIM:].astype(BF16)
            au_s[wset, rows, cols] = auw[:, :GDN_HEAD_DIM]
            qe_s[wset, rows, cols] = (p["qh"].astype(F32) * p["egc"] - auw[:, GDN_HEAD_DIM:]).astype(BF16)
            egl_s[wset, pl.ds(p["i"] * SUBLANES + hd, 1), :] = jnp.broadcast_to(jnp.exp(p["g_last"]), (1, LANES))

    def inter(i):
        rows = pl.ds(i * c, c)
        srows = pl.ds(i * GDN_HEAD_DIM, GDN_HEAD_DIM)
        ss = [s_scr[hd] for hd in heads]
        sbs = [s.astype(BF16) for s in ss]
        upd = [_dot(kw_s[rset, srows, hcols[hd]], sbs[hd]) for hd in heads]
        for hd in heads:
            s_scr[hd] = ss[hd] * egl_s[rset, pl.ds(i * SUBLANES + hd, 1), :] + (ku_s[rset, srows, hcols[hd]] - upd[hd])
        os_ = [_dot(qe_s[rset, rows, hcols[hd]], sbs[hd]) + au_s[rset, rows, hcols[hd]] for hd in heads]
        for hd in heads:
            o = os_[hd]
            gate = _silu(z_ref[0, rows, hcols[hd]].astype(F32))
            on = o * lax.rsqrt(jnp.mean(o * o, axis=-1, keepdims=True) + RMS_EPS) * gnw * gate
            o_ref[0, rows, hcols[hd]] = on.astype(BF16)

    def run(do_intra, do_inter):
        stages = intra_stages() if do_intra else iter(())
        todo = list(range(g)) if do_inter else []
        n_stages = 6 + 2 * (c.bit_length() - 2)
        every = max(1, n_stages // max(1, len(todo)))
        for n, _ in enumerate(stages):
            if todo and n % every == 0:
                inter(todo.pop(0))
        for i in todo:
            inter(i)

    if ncg == 1:
        s_scr[...] = s0_ref[0]
        run(True, False)
        run(False, True)
        sout_ref[0] = s_scr[...]
        return

    @pl.when(cg == 0)
    def _():
        s_scr[...] = s0_ref[0]
        run(True, False)

    @pl.when((cg > 0) & (cg < ncg))
    def _():
        run(True, True)

    @pl.when(cg == ncg)
    def _():
        run(False, True)
        sout_ref[0] = s_scr[...]


def _gdn(q, k, v, z, bgc, bgr, s0, gnw, c, g):
    b, l, _ = q.shape
    ncg = l // (c * g)
    cur = lambda w: pl.BlockSpec((1, c * g, w), lambda i, j: (i, jnp.minimum(j, ncg - 1), 0))
    prev = lambda w: pl.BlockSpec((1, c * g, w), lambda i, j: (i, jnp.maximum(j - 1, 0), 0))
    st = pl.BlockSpec((1, GDN_HEADS, GDN_HEAD_DIM, GDN_HEAD_DIM), lambda i, j: (i, 0, 0, 0))
    tok_buf = lambda dt: pltpu.VMEM((2, c * g, GDN_WIDTH), dt)
    state_buf = lambda dt: pltpu.VMEM((2, g * GDN_HEAD_DIM, GDN_WIDTH), dt)
    return pl.pallas_call(
        functools.partial(_gdn_body, c=c, g=g, ncg=ncg),
        grid=(b, ncg + 1 if ncg > 1 else 1),
        in_specs=[cur(GDN_WIDTH), cur(GDN_WIDTH), cur(GDN_WIDTH), prev(GDN_WIDTH), cur(LANES),
                  pl.BlockSpec((1, g, SUBLANES, c), lambda i, j: (i, jnp.minimum(j, ncg - 1), 0, 0)), st,
                  pl.BlockSpec((1, GDN_HEAD_DIM), lambda i, j: (0, 0))],
        out_specs=[prev(GDN_WIDTH), st],
        out_shape=[jax.ShapeDtypeStruct((b, l, GDN_WIDTH), BF16), jax.ShapeDtypeStruct(s0.shape, F32)],
        scratch_shapes=[pltpu.VMEM((GDN_HEADS, GDN_HEAD_DIM, GDN_HEAD_DIM), F32),
                        state_buf(F32), state_buf(BF16), tok_buf(F32), tok_buf(BF16),
                        pltpu.VMEM((2, g * SUBLANES, LANES), F32)],
        compiler_params=_cparams(("arbitrary", "arbitrary")),
        name="gdn",
    )(q, k, v, z, bgc, bgr, s0, gnw)


HALF = D_MODEL // 2
HI_MASK = 0xFFFF0000


def _pack_bf16_pairs(x):
    bits = pltpu.bitcast(x.astype(BF16).astype(F32), U32)
    return (bits[:, :HALF] >> 16) | (bits[:, HALF:] & jnp.uint32(HI_MASK))


def _unpack_bf16_pairs(p):
    return pltpu.bitcast(p << 16, F32), pltpu.bitcast(p & jnp.uint32(HI_MASK), F32)


ROW_TILES = HALF // LANES


def _row_lines(r):
    return pl.ds(r * ROW_TILES, ROW_TILES)


def _store_rows(ref, lead, r0, packed):
    n = packed.shape[0]
    for q in range(ROW_TILES):
        ref[(*lead, pl.ds(r0 * ROW_TILES + q, n, stride=ROW_TILES), slice(None))] = packed[:, q * LANES:(q + 1) * LANES]


def _load_rows(ref, lead, r0, n):
    return jnp.concatenate(
        [ref[(*lead, pl.ds(r0 * ROW_TILES + q, n, stride=ROW_TILES), slice(None))] for q in range(ROW_TILES)], axis=1)


def _mid_body(o_ref, uc_ref, x_ref, gt1_ref, sh2_ref, sc2_ref, gt2_ref,
              wout_ref, g1_ref, b1_ref, wsgu_ref, wsd_ref, wrh_ref, wrl_ref, pre_ref, h2p_ref, lgt_ref):
    gt1, sh2, sc2, gt2 = gt1_ref[0], sh2_ref[0], sc2_ref[0], gt2_ref[0]
    mix = _dot(jnp.concatenate([o_ref[0], uc_ref[0]], axis=1), wout_ref[...])
    x1 = _layer_norm(ALPHA * x_ref[0] + (1.0 + gt1) * mix) * g1_ref[...] + b1_ref[...]
    h2 = _layer_norm(x1) * (1.0 + sc2) + sh2
    hh, hl = _split_bf16(h2)
    lgt_ref[...] = _dot_nt(wrh_ref[...], hh) + (_dot_nt(wrh_ref[...], hl) + _dot_nt(wrl_ref[...], hh))
    gu = _dot(hh, wsgu_ref[...])
    act = _silu(gu[:, :D_SHARED]) * gu[:, D_SHARED:]
    shared = _dot(act.astype(BF16), wsd_ref[...])
    pre_ref[0] = ALPHA * x1 + (1.0 + gt2) * shared
    _store_rows(h2p_ref, (0,), 0, _pack_bf16_pairs(h2))


def _mod_spec(m, tl):
    if m.shape[1] == 1:
        return pl.BlockSpec((1, 1, D_MODEL), lambda i, j: (i, 0, 0))
    return pl.BlockSpec((1, tl, D_MODEL), lambda i, j: (i, j, 0))


def _mid(o, uc, x, mods, prm, tl):
    b, l, _ = x.shape
    nt = l // tl
    full = lambda a: pl.BlockSpec(a.shape, lambda i, j: (0,) * a.ndim)
    consts = [prm["w_out"], prm["ln1_g"], prm["ln1_b"], prm["ws_gu"], prm["ws_down"], prm["wr_hi"], prm["wr_lo"]]
    return pl.pallas_call(
        _mid_body,
        grid=(b, nt),
        in_specs=[pl.BlockSpec((1, tl, GDN_WIDTH), lambda i, j: (i, j, 0)),
                  pl.BlockSpec((1, tl, CONV_WIDTH), lambda i, j: (i, j, 0)),
                  pl.BlockSpec((1, tl, D_MODEL), lambda i, j: (i, j, 0))]
                 + [_mod_spec(m, tl) for m in mods] + [full(a) for a in consts],
        out_specs=[pl.BlockSpec((1, tl, D_MODEL), lambda i, j: (i, j, 0)),
                   pl.BlockSpec((1, tl * ROW_TILES, LANES), lambda i, j: (i, j, 0)),
                   pl.BlockSpec((N_EXPERTS, tl), lambda i, j: (0, i * nt + j))],
        out_shape=[jax.ShapeDtypeStruct((b, l, D_MODEL), F32), jax.ShapeDtypeStruct((b, l * ROW_TILES, LANES), U32),
                   jax.ShapeDtypeStruct((N_EXPERTS, b * l), F32)],
        compiler_params=_cparams(("arbitrary", "arbitrary")),
        name="mixer_out",
    )(o, uc, x, *mods, *consts)


def _first_max(x, row, n):
    m = jnp.max(x, axis=0, keepdims=True)
    ix = jnp.min(jnp.where(x == m, row, float(n)), axis=0, keepdims=True)
    return m, ix


def _route_body(lgp_ref, lgs_ref, bias_ref, idx_ref, w_ref, rank_ref, cnt_ref, cnt_scr, *, tr, n_prompt_tiles):
    @pl.when(pl.program_id(0) == 0)
    def _():
        cnt_scr[...] = jnp.zeros_like(cnt_scr)

    neg = -jnp.inf
    logits = jnp.where(pl.program_id(0) < n_prompt_tiles, lgp_ref[...], lgs_ref[...])
    scores = _sigmoid(logits)
    sel = scores + bias_ref[:, 0:1]
    row_g = lax.broadcasted_iota(I32, (GROUP_SIZE, tr), 0).astype(F32)
    gs = []
    for g in range(N_GROUPS):
        blk = sel[g * GROUP_SIZE:(g + 1) * GROUP_SIZE, :]
        m1, i1 = _first_max(blk, row_g, GROUP_SIZE)
        m2 = jnp.max(jnp.where(row_g == i1, neg, blk), axis=0, keepdims=True)
        gs.append(m1 + m2)
    gs = jnp.concatenate(gs, axis=0)
    row_n = lax.broadcasted_iota(I32, (N_GROUPS, tr), 0).astype(F32)
    chosen = jnp.zeros((N_GROUPS, tr), F32)
    for _ in range(TOPK_GROUPS):
        _, ix = _first_max(gs, row_n, N_GROUPS)
        hit = row_n == ix
        chosen = jnp.where(hit, 1.0, chosen)
        gs = jnp.where(hit, neg, gs)
    selm = jnp.concatenate(
        [jnp.where(chosen[g:g + 1, :] > 0.5, sel[g * GROUP_SIZE:(g + 1) * GROUP_SIZE, :], neg) for g in range(N_GROUPS)],
        axis=0)
    row_e = lax.broadcasted_iota(I32, (N_EXPERTS, tr), 0).astype(F32)
    idxs, ws = [], []
    picked = jnp.zeros((N_EXPERTS, tr), F32)
    for _ in range(TOP_K):
        _, ix = _first_max(selm, row_e, N_EXPERTS)
        hit = row_e == ix
        ws.append(jnp.sum(jnp.where(hit, scores, 0.0), axis=0, keepdims=True))
        idxs.append(ix)
        selm = jnp.where(hit, neg, selm)
        picked = jnp.where(hit, 1.0, picked)
    wsum = ws[0]
    for wk in ws[1:]:
        wsum = wsum + wk
    idx_ref[...] = jnp.concatenate(idxs, axis=0).astype(I32)
    w_ref[...] = jnp.concatenate(ws, axis=0) / wsum * ROUTED_SCALE
    ti = lax.broadcasted_iota(I32, (tr, tr), 0)
    tj = lax.broadcasted_iota(I32, (tr, tr), 1)
    before = _dot(picked.astype(BF16), jnp.where(ti < tj, 1.0, 0.0).astype(BF16)) + cnt_scr[:, 0:1]
    rank_ref[...] = jnp.concatenate(
        [jnp.sum(jnp.where(row_e == ix, before, 0.0), axis=0, keepdims=True) for ix in idxs], axis=0).astype(I32)
    cnt_scr[...] = cnt_scr[...] + jnp.sum(picked, axis=1, keepdims=True)
    cnt_ref[...] = cnt_scr[...]


def _route(logits_p, logits_s, bias_b, tr):
    ntp, nts = logits_p.shape[1] // tr, logits_s.shape[1] // tr
    t = (ntp + nts) * tr
    kt = lambda dt: (pl.BlockSpec((TOP_K, tr), lambda i: (0, i)), jax.ShapeDtypeStruct((TOP_K, t), dt))
    outs = [kt(I32), kt(F32), kt(I32),
            (pl.BlockSpec((N_EXPERTS, LANES), lambda i: (0, 0)), jax.ShapeDtypeStruct((N_EXPERTS, LANES), F32))]
    return pl.pallas_call(
        functools.partial(_route_body, tr=tr, n_prompt_tiles=ntp),
        grid=(ntp + nts,),
        in_specs=[pl.BlockSpec((N_EXPERTS, tr), lambda i: (0, jnp.minimum(i, ntp - 1))),
                  pl.BlockSpec((N_EXPERTS, tr), lambda i: (0, jnp.maximum(i - ntp, 0))),
                  pl.BlockSpec((N_EXPERTS, LANES), lambda i: (0, 0))],
        out_specs=[o[0] for o in outs],
        out_shape=[o[1] for o in outs],
        scratch_shapes=[pltpu.VMEM((N_EXPERTS, LANES), F32)],
        compiler_params=_cparams(("arbitrary",)),
        name="route",
    )(logits_p, logits_s, bias_b)


def _pos_body(idx_ref, rank_ref, cnt_ref, pos_ref, *, tr):
    ei = lax.broadcasted_iota(I32, (N_EXPERTS, N_EXPERTS), 0)
    ej = lax.broadcasted_iota(I32, (N_EXPERTS, N_EXPERTS), 1)
    below = jnp.where(ej < ei, 1.0, 0.0).astype(BF16)
    start = sum(_dot(below, part) for part in _split3_bf16(cnt_ref[...]))[:, 0:1]
    row_e = lax.broadcasted_iota(I32, (N_EXPERTS, tr), 0)
    pos_ref[...] = jnp.concatenate(
        [jnp.sum(jnp.where(row_e == idx_ref[k:k + 1, :], start, 0.0), axis=0, keepdims=True) for k in range(TOP_K)],
        axis=0).astype(I32) + rank_ref[...]


def _positions(idx, rank, cnt, tr):
    t = idx.shape[1]
    kt = pl.BlockSpec((TOP_K, tr), lambda i: (0, i))
    return pl.pallas_call(
        functools.partial(_pos_body, tr=tr),
        grid=(t // tr,),
        in_specs=[kt, kt, pl.BlockSpec((N_EXPERTS, LANES), lambda i: (0, 0))],
        out_specs=kt,
        out_shape=jax.ShapeDtypeStruct((TOP_K, t), I32),
        compiler_params=_cparams(("arbitrary",)),
        name="positions",
    )(idx, rank, cnt)


HBM_SOURCED_COPIES = (2, 5)


def _dispatch_body(pos_ref, hp_ref, hs_ref, hp_hbm, hs_hbm, xs_ref, sem, *, tt, n_prompt_tiles):
    i = pl.program_id(0)

    def scatter(src_ref, src_hbm, tile):
        def body(t, carry):
            for k in range(TOP_K):
                dst = xs_ref.at[_row_lines(pos_ref[0, 0, t * TOP_K + k])]
                if k in HBM_SOURCED_COPIES:
                    pltpu.make_async_copy(src_hbm.at[_row_lines(tile * tt + t)], dst, sem).start()
                else:
                    pltpu.make_async_copy(src_ref.at[_row_lines(t)], dst, sem).start(priority=k % 2)
            return carry
        lax.fori_loop(0, tt, body, 0)
        for _ in range(TOP_K):
            pltpu.make_async_copy(src_ref, xs_ref.at[pl.ds(0, tt * ROW_TILES)], sem).wait()

    @pl.when(i < n_prompt_tiles)
    def _():
        scatter(hp_ref, hp_hbm, i)

    @pl.when(i >= n_prompt_tiles)
    def _():
        scatter(hs_ref, hs_hbm, i - n_prompt_tiles)


def _dispatch(pos_tiles, h_prompt, h_sample, tt):
    tp, ts = h_prompt.shape[0] // ROW_TILES, h_sample.shape[0] // ROW_TILES
    ntp, nts = tp // tt, ts // tt
    n_rows = TOP_K * (tp + ts)
    return pl.pallas_call(
        functools.partial(_dispatch_body, tt=tt, n_prompt_tiles=ntp),
        grid=(ntp + nts,),
        in_specs=[pl.BlockSpec((1, 1, tt * TOP_K), lambda i: (i, 0, 0), memory_space=pltpu.SMEM),
                  pl.BlockSpec((tt * ROW_TILES, LANES), lambda i: (jnp.minimum(i, ntp - 1), 0)),
                  pl.BlockSpec((tt * ROW_TILES, LANES), lambda i: (jnp.maximum(i - ntp, 0), 0)),
                  pl.BlockSpec(memory_space=pl.ANY), pl.BlockSpec(memory_space=pl.ANY)],
        out_specs=pl.BlockSpec(memory_space=pl.ANY),
        out_shape=jax.ShapeDtypeStruct((n_rows * ROW_TILES, LANES), U32),
        scratch_shapes=[pltpu.SemaphoreType.DMA(())],
        compiler_params=_cparams(("arbitrary",)),
        name="dispatch",
    )(pos_tiles, h_prompt, h_sample, h_prompt, h_sample)


GMM_ROWS = 2048
GMM_SUB = 512


def _gmm_body(ve_ref, vb_ref, vlo_ref, vhi_ref, vfirst_ref, vnew_ref,
              xs_ref, wg_ref, wu_ref, wd_ref, ys_ref, wgu_s, wd_s):
    v = pl.program_id(0)
    lo = vlo_ref[v]
    hi = vhi_ref[v]

    @pl.when(vnew_ref[v] == 1)
    def _():
        wgu_s[:, :D_EXPERT] = wg_ref[0].astype(BF16)
        wgu_s[:, D_EXPERT:] = wu_ref[0].astype(BF16)
        wd_s[...] = wd_ref[0].astype(BF16)

    @pl.when(vfirst_ref[v] == 1)
    def _():
        ys_ref[...] = jnp.zeros_like(ys_ref)

    def sub_block(s, carry):
        r0 = pl.multiple_of(s * GMM_SUB, GMM_SUB)
        x_lo, x_hi = _unpack_bf16_pairs(_load_rows(xs_ref, (), r0, GMM_SUB))
        gu = _dot(x_lo.astype(BF16), wgu_s[:HALF, :]) + _dot(x_hi.astype(BF16), wgu_s[HALF:, :])
        act = _silu(gu[:, :D_EXPERT]) * gu[:, D_EXPERT:]
        y = _pack_bf16_pairs(_dot(act.astype(BF16), wd_s[...]))
        row = lax.broadcasted_iota(I32, y.shape, 0) + r0
        _store_rows(ys_ref, (), r0, jnp.where((row >= lo) & (row < hi), y, _load_rows(ys_ref, (), r0, GMM_SUB)))
        return carry

    lax.fori_loop(lo // GMM_SUB, (hi + GMM_SUB - 1) // GMM_SUB, sub_block, 0)


def _gmm_schedule(counts, n_rows):
    nb = n_rows // GMM_ROWS
    n_vis = nb + N_EXPERTS
    ends = jnp.cumsum(counts)
    starts = ends - counts
    first_blk = starts // GMM_ROWS
    last_blk = jnp.maximum(ends - 1, 0) // GMM_ROWS
    per_e = jnp.where(counts > 0, last_blk - first_blk + 1, 0)
    vis_end = jnp.cumsum(per_e)
    total = vis_end[-1]
    v = jnp.minimum(jnp.arange(n_vis, dtype=I32), total - 1)
    e = jnp.minimum(jnp.sum((vis_end[None, :] <= v[:, None]).astype(I32), axis=1), N_EXPERTS - 1)
    table = jnp.stack([first_blk, vis_end - per_e, starts, ends], axis=1).astype(I32)
    pick = e[:, None] == jnp.arange(N_EXPERTS, dtype=I32)[None, :]
    fb, v0, st, en = jnp.moveaxis(jnp.sum(jnp.where(pick[:, :, None], table[None], 0), axis=1), 1, 0)
    blk = fb + (v - v0)
    lo = jnp.maximum(st, blk * GMM_ROWS) - blk * GMM_ROWS
    hi = jnp.minimum(en, (blk + 1) * GMM_ROWS) - blk * GMM_ROWS
    hi = jnp.where(jnp.arange(n_vis) < total, hi, lo)
    prev = lambda a: jnp.concatenate([jnp.full((1,), -1, I32), a[:-1]])
    first = (blk != prev(blk)).astype(I32)
    new_e = (e != prev(e)).astype(I32)
    return e, blk, lo.astype(I32), hi.astype(I32), first, new_e


def _gmm(xs, counts, we_gate, we_up, we_down):
    n_rows = xs.shape[0] // ROW_TILES
    sched = _gmm_schedule(counts, n_rows)
    n_vis = sched[0].shape[0]
    grid_spec = pltpu.PrefetchScalarGridSpec(
        num_scalar_prefetch=len(sched),
        grid=(n_vis,),
        in_specs=[pl.BlockSpec((GMM_ROWS * ROW_TILES, LANES), lambda v, ve, vb, *_: (vb[v], 0)),
                  pl.BlockSpec((1, D_MODEL, D_EXPERT), lambda v, ve, *_: (ve[v], 0, 0)),
                  pl.BlockSpec((1, D_MODEL, D_EXPERT), lambda v, ve, *_: (ve[v], 0, 0)),
                  pl.BlockSpec((1, D_EXPERT, D_MODEL), lambda v, ve, *_: (ve[v], 0, 0))],
        out_specs=pl.BlockSpec((GMM_ROWS * ROW_TILES, LANES), lambda v, ve, vb, *_: (vb[v], 0)),
        scratch_shapes=[pltpu.VMEM((D_MODEL, 2 * D_EXPERT), BF16), pltpu.VMEM((D_EXPERT, D_MODEL), BF16)])
    return pl.pallas_call(
        _gmm_body,
        grid_spec=grid_spec,
        out_shape=jax.ShapeDtypeStruct(xs.shape, U32),
        compiler_params=_cparams(("arbitrary",)),
        name="expert_ffn",
    )(*sched, xs, we_gate, we_up, we_down)


def _combine_body(pos_ref, posn_ref, w_ref, pre_ref, gt2_ref, g2_ref, b2_ref, ys_ref, out_ref, gbuf, sem,
                  *, tt, n_steps):
    step = pl.program_id(0) * pl.num_programs(1) + pl.program_id(1)
    slot = step % 2

    def gather(p_ref, into):
        def body(t, carry):
            for k in range(TOP_K):
                pltpu.make_async_copy(ys_ref.at[_row_lines(p_ref[0, 0, t * TOP_K + k])],
                                      gbuf.at[into, k, _row_lines(t)], sem.at[into]).start(priority=k % 2)
            return carry
        lax.fori_loop(0, tt, body, 0)

    @pl.when(step == 0)
    def _():
        gather(pos_ref, 0)

    @pl.when(step + 1 < n_steps)
    def _():
        gather(posn_ref, 1 - slot)

    pltpu.make_async_copy(gbuf.at[slot], gbuf.at[slot], sem.at[slot]).wait()
    acc_lo = jnp.zeros((tt, HALF), F32)
    acc_hi = jnp.zeros((tt, HALF), F32)
    for k in range(TOP_K):
        y_lo, y_hi = _unpack_bf16_pairs(_load_rows(gbuf, (slot, k), 0, tt))
        wk = w_ref[:, k:k + 1]
        acc_lo = acc_lo + wk * y_lo
        acc_hi = acc_hi + wk * y_hi
    routed = jnp.concatenate([acc_lo, acc_hi], axis=1)
    y = pre_ref[0] + (1.0 + gt2_ref[0]) * routed
    out_ref[0] = _layer_norm(y) * g2_ref[...] + b2_ref[...]


def _combine(pos_tiles, w_tk, pre, gt2, ln2_g, ln2_b, ys, tt, tok0):
    b, l, _ = pre.shape
    nt = l // tt
    blk0 = tok0 // tt
    last = blk0 + b * nt - 1
    pos_spec = lambda nxt: pl.BlockSpec((1, 1, tt * TOP_K),
                                        lambda i, j: (jnp.minimum(blk0 + i * nt + j + nxt, last), 0, 0),
                                        memory_space=pltpu.SMEM)
    return pl.pallas_call(
        functools.partial(_combine_body, tt=tt, n_steps=b * nt),
        grid=(b, nt),
        in_specs=[pos_spec(0), pos_spec(1),
                  pl.BlockSpec((tt, TOP_K), lambda i, j: (blk0 + i * nt + j, 0)),
                  pl.BlockSpec((1, tt, D_MODEL), lambda i, j: (i, j, 0)),
                  _mod_spec(gt2, tt),
                  pl.BlockSpec((1, D_MODEL), lambda i, j: (0, 0)),
                  pl.BlockSpec((1, D_MODEL), lambda i, j: (0, 0)),
                  pl.BlockSpec(memory_space=pl.ANY)],
        out_specs=pl.BlockSpec((1, tt, D_MODEL), lambda i, j: (i, j, 0)),
        out_shape=jax.ShapeDtypeStruct((b, l, D_MODEL), F32),
        scratch_shapes=[pltpu.VMEM((2, TOP_K, tt * ROW_TILES, LANES), U32), pltpu.SemaphoreType.DMA((2,))],
        compiler_params=_cparams(("arbitrary", "arbitrary")),
        name="combine",
    )(pos_tiles, pos_tiles, w_tk, pre, gt2, ln2_g, ln2_b, ys)


def _prep_params(w_in, conv_qkv_w, a_log, dt_bias, dw_w, dw_b, cn_g, cn_b):
    z0 = QKV_WIDTH
    b0 = z0 + GDN_WIDTH
    g0 = b0 + 2 * GDN_HEADS
    w_bg = w_in[:, b0:g0]
    w_in_r = jnp.concatenate(
        [w_in[:, :b0], w_in[:, g0:], w_bg, jnp.zeros((D_MODEL, LANES - 2 * GDN_HEADS), w_in.dtype)], axis=1).astype(BF16)
    pad_h = jnp.zeros((GDN_HEADS,), F32)
    al = jnp.concatenate([pad_h, a_log.astype(F32)])
    db = jnp.concatenate([pad_h, dt_bias.astype(F32)])
    gp_c = jnp.zeros((SUBLANES, LANES), F32).at[0, :2 * GDN_HEADS].set(al).at[1, :2 * GDN_HEADS].set(db)
    gp_r = jnp.zeros((SUBLANES, LANES), F32).at[:, 0].set(al).at[:, 1].set(db)
    return dict(
        w_in_r=w_in_r, w_bgt=w_bg.T.astype(BF16), conv_qkv_w=conv_qkv_w.astype(F32),
        dw_w=jnp.concatenate([dw_w, jnp.zeros((1, CONV_WIDTH), dw_w.dtype)], axis=0).astype(F32),
        dw_b=dw_b.reshape(1, -1).astype(F32), cn_g=cn_g.reshape(1, -1).astype(F32), cn_b=cn_b.reshape(1, -1).astype(F32),
        gp_c=gp_c, gp_r=gp_r)


def _prep_mid_params(w_out, ln1_g, ln1_b, w_router, router_bias, ws_gate, ws_up, ws_down, ln2_g, ln2_b, gdn_norm_w):
    row = lambda a: a.reshape(1, -1).astype(F32)
    wr_t = w_router.astype(F32).T
    wr_hi = wr_t.astype(BF16)
    return dict(
        w_out=w_out.astype(BF16), ln1_g=row(ln1_g), ln1_b=row(ln1_b),
        ws_gu=jnp.concatenate([ws_gate, ws_up], axis=1).astype(BF16), ws_down=ws_down.astype(BF16),
        wr_hi=wr_hi, wr_lo=(wr_t - wr_hi.astype(F32)).astype(BF16),
        bias_b=jnp.broadcast_to(router_bias.astype(F32)[:, None], (N_EXPERTS, LANES)),
        ln2_g=row(ln2_g), ln2_b=row(ln2_b), gnw=row(gdn_norm_w))


def _tile(n, pref):
    t = min(pref, n)
    while n % t:
        t //= 2
    return t


def _token_mixer(x, mod, s_gdn, s_qkv, s_dw, prm, mprm):
    b, l, _ = x.shape
    assert l >= DW_CONV - 1 and l % SUBLANES == 0
    tl = _tile(l, 256)
    sq_pad = jnp.pad(s_qkv.astype(F32), ((0, 0), (QKV_TAIL - (GDN_CONV - 1), 0), (0, 0)))
    sd_pad = jnp.pad(s_dw.astype(F32), ((0, 0), (DW_TAIL - (DW_CONV - 1), 0), (0, 0)))
    q, k, v, z, bgc, bgr, uc, nq, nd = _front(x, mod, sq_pad, sd_pad, prm, tl)
    c = min(CHUNK, l)
    n_chunks = l // c
    g = _tile(n_chunks, 8)
    bgr = bgr.reshape(b, SUBLANES, n_chunks, c).transpose(0, 2, 1, 3)
    o, s_new = _gdn(q, k, v, z, bgc, bgr, s_gdn.astype(F32), mprm["gnw"], c, g)
    return o, uc, s_new, nq[:, QKV_TAIL - (GDN_CONV - 1):], nd[:, DW_TAIL - (DW_CONV - 1):]


def kernel(x_prompt, x_sample, state_gdn, state_qkv_conv, state_dw_conv, c_prompt, c_sample, w_ada, b_ada, w_in, conv_qkv_w, a_log, dt_bias, gdn_norm_w, dw_w, dw_b, cn_g, cn_b, w_out, ln1_g, ln1_b, w_router, router_bias, we_gate, we_up, we_down, ws_gate, ws_up, ws_down, ln2_g, ln2_b):
    bp, lp, _ = x_prompt.shape
    bs, ls, _ = x_sample.shape
    tp, ts = bp * lp, bs * ls
    yp, ys = x_prompt, x_sample
    c_all = jnp.concatenate([c_prompt, c_sample], axis=0)
    new_p, new_s = [], []
    for l in range(w_ada.shape[0]):
        prm = _prep_params(w_in[l], conv_qkv_w[l], a_log[l], dt_bias[l], dw_w[l], dw_b[l], cn_g[l], cn_b[l])
        mprm = _prep_mid_params(w_out[l], ln1_g[l], ln1_b[l], w_router[l], router_bias[l], ws_gate[l], ws_up[l],
                                ws_down[l], ln2_g[l], ln2_b[l], gdn_norm_w[l])
        mod = _ada_mod(c_all, w_ada[l], b_ada[l]).reshape(bp + bs, 6, D_MODEL)
        mod_p, mod_s = mod[:bp], mod[bp:]

        zg = jnp.zeros((bp, GDN_HEADS, GDN_HEAD_DIM, GDN_HEAD_DIM), F32)
        zq = jnp.zeros((bp, GDN_CONV - 1, QKV_WIDTH), F32)
        zd = jnp.zeros((bp, DW_CONV - 1, CONV_WIDTH), F32)
        o_p, uc_p, g_p, q_p, d_p = _token_mixer(yp, mod_p, zg, zq, zd, prm, mprm)
        o_s, uc_s, g_s, q_s, d_s = _token_mixer(ys, mod_s, state_gdn[l], state_qkv_conv[l], state_dw_conv[l], prm, mprm)

        mods_p = tuple(mod_p[:, j:j + 1, :] for j in (2, 3, 4, 5))
        mods_s = tuple(jnp.repeat(mod_s[:, j, :], ls, axis=0)[None] for j in (2, 3, 4, 5))
        flat = lambda a: a.reshape(1, ts, a.shape[-1])
        pre_p, h_p, lg_p = _mid(o_p, uc_p, yp, mods_p, mprm, _tile(lp, 512))
        pre_s, h_s, lg_s = _mid(flat(o_s), flat(uc_s), flat(ys), mods_s, mprm, _tile(ts, 256))

        tt = _tile(ts, 512)
        assert tp % tt == 0 and lp % tt == 0 and (TOP_K * (tp + ts)) % GMM_ROWS == 0
        idx, w, rank, cnt = _route(lg_p, lg_s, mprm["bias_b"], tt)
        pos = _positions(idx, rank, cnt, tt)
        pos_tiles = pos.T.reshape((tp + ts) // tt, 1, tt * TOP_K)
        xs = _dispatch(pos_tiles, h_p.reshape(tp * ROW_TILES, LANES), h_s.reshape(ts * ROW_TILES, LANES), tt)
        ye = _gmm(xs, cnt[:, 0].astype(I32), we_gate[l], we_up[l], we_down[l])
        w_tk = w.T
        yp = _combine(pos_tiles, w_tk, pre_p, mods_p[3], mprm["ln2_g"], mprm["ln2_b"], ye, tt, 0)
        ys = _combine(pos_tiles, w_tk, pre_s, mods_s[3], mprm["ln2_g"], mprm["ln2_b"], ye, tt, tp
                      ).reshape(bs, ls, D_MODEL)
        new_p.append((g_p.astype(state_gdn.dtype), q_p.astype(x_prompt.dtype), d_p.astype(x_prompt.dtype)))
        new_s.append((g_s.astype(state_gdn.dtype), q_s.astype(state_qkv_conv.dtype), d_s.astype(state_dw_conv.dtype)))
    stack = lambda rows, j: jnp.stack([r[j] for r in rows])
    return (yp, ys, stack(new_p, 0), stack(new_p, 1), stack(new_p, 2), stack(new_s, 0), stack(new_s, 1), stack(new_s, 2))
```

```python
import functools

import jax
import jax.numpy as jnp
from jax import lax
from jax.experimental import pallas as pl
from jax.experimental.pallas import tpu as pltpu

F32 = jnp.float32
BF16 = jnp.bfloat16
I32 = jnp.int32
U32 = jnp.uint32

D_MODEL = 1024
GDN_WIDTH = 512
CONV_WIDTH = 512
GDN_HEAD_DIM = 128
GDN_HEADS = 4
QKV_WIDTH = 3 * GDN_WIDTH
GDN_CONV = 4
DW_CONV = 31
CHUNK = 64
N_EXPERTS = 256
TOP_K = 8
N_GROUPS = 8
GROUP_SIZE = N_EXPERTS // N_GROUPS
TOPK_GROUPS = 4
D_EXPERT = 256
D_SHARED = 256
ROUTED_SCALE = 2.5
LN_EPS = 1e-5
RMS_EPS = 1e-6
L2_EPS = 1e-6
DEPTH = 1
ALPHA = (2.0 * DEPTH) ** 0.25

LANES = 128
SUBLANES = 8
VMEM_LIMIT_BYTES = 56 * 1024 * 1024

COL_Z = QKV_WIDTH
COL_GV = COL_Z + GDN_WIDTH
COL_GG = COL_GV + CONV_WIDTH
COL_BG = COL_GG + CONV_WIDTH
D_PROJ_PAD = COL_BG + LANES
QKV_TAIL = SUBLANES
DW_TAIL = 32
NEG_BIG = -1e30


def _cparams(sem):
    return pltpu.CompilerParams(dimension_semantics=sem, vmem_limit_bytes=VMEM_LIMIT_BYTES)


def _split_bf16(x):
    hi = x.astype(BF16)
    lo = (x - hi.astype(F32)).astype(BF16)
    return hi, lo


def _dot(a, b):
    return jnp.dot(a, b, preferred_element_type=F32)


def _dot_nt(a, b):
    return lax.dot_general(a, b, (((1,), (1,)), ((), ())), preferred_element_type=F32)


def _dot_tn(a, b):
    return lax.dot_general(a, b, (((0,), (0,)), ((), ())), preferred_element_type=F32)


def _dot_hp(a, b):
    ah, al = _split_bf16(a)
    bh, bl = _split_bf16(b)
    return _dot(ah, bh) + (_dot(ah, bl) + _dot(al, bh))


def _sigmoid(x):
    return 1.0 / (1.0 + jnp.exp(-x))


def _silu(x):
    return x * _sigmoid(x)


def _softplus(x):
    return jnp.maximum(x, 0.0) + jnp.log(1.0 + jnp.exp(-jnp.abs(x)))


def _layer_norm(x):
    mu = jnp.mean(x, axis=-1, keepdims=True)
    xc = x - mu
    var = jnp.mean(xc * xc, axis=-1, keepdims=True)
    return xc * lax.rsqrt(var + LN_EPS)


def _ada_body(c_ref, w_ref, b_ref, o_ref):
    o_ref[...] = _dot_hp(_silu(c_ref[...]), w_ref[...]) + b_ref[...]


def _ada_mod(c, w_ada, b_ada):
    bt = c.shape[0]
    n_col = w_ada.shape[1] // D_MODEL
    return pl.pallas_call(
        _ada_body,
        grid=(n_col,),
        in_specs=[
            pl.BlockSpec((bt, D_MODEL), lambda j: (0, 0)),
            pl.BlockSpec((D_MODEL, D_MODEL), lambda j: (0, j)),
            pl.BlockSpec((1, D_MODEL), lambda j: (0, j)),
        ],
        out_specs=pl.BlockSpec((bt, D_MODEL), lambda j: (0, j)),
        out_shape=jax.ShapeDtypeStruct((bt, w_ada.shape[1]), F32),
        compiler_params=_cparams(("arbitrary",)),
        name="ada_mod",
    )(c, w_ada, b_ada.reshape(1, -1))


def _front_body(x_ref, mod_ref, win_ref, wbgt_ref, cw_ref, dww_ref, dwb_ref, cng_ref, cnb_ref,
                gpc_ref, gpr_ref, sq_ref, sd_ref,
                q_ref, k_ref, v_ref, z_ref, bgc_ref, bgr_ref, uc_ref, nq_ref, nd_ref,
                qkv_buf, u_buf, *, tl):
    t = pl.program_id(1)
    rq = QKV_TAIL + tl
    ru = DW_TAIL + tl
    n_qkv = QKV_WIDTH // LANES
    n_u = CONV_WIDTH // LANES
    lanes = lambda c: slice(c * LANES, (c + 1) * LANES)

    @pl.when(t == 0)
    def _():
        for c in range(n_qkv):
            qkv_buf[c * rq:c * rq + QKV_TAIL, :] = sq_ref[0, :, lanes(c)]
        for c in range(n_u):
            u_buf[c * ru:c * ru + DW_TAIL, :] = sd_ref[0, :, lanes(c)]

    sh1 = mod_ref[0, 0:1, :]
    sc1 = mod_ref[0, 1:2, :]
    h = _layer_norm(x_ref[0]) * (1.0 + sc1) + sh1
    hb = h.astype(BF16)

    qkv = _dot(hb, win_ref[:, 0:QKV_WIDTH])
    for c in range(n_qkv):
        qkv_buf[c * rq + QKV_TAIL:(c + 1) * rq, :] = qkv[:, lanes(c)]
    off_q = QKV_TAIL - (GDN_CONV - 1)
    for c in range(n_qkv):
        acc = cw_ref[0:1, lanes(c)] * qkv_buf[c * rq + off_q:c * rq + off_q + tl, :]
        for j in range(1, GDN_CONV):
            acc = acc + cw_ref[j:j + 1, lanes(c)] * qkv_buf[c * rq + off_q + j:c * rq + off_q + j + tl, :]
        a = _silu(acc)
        hd = c % GDN_HEADS
        if c < GDN_HEADS:
            a = a * lax.rsqrt(jnp.sum(a * a, axis=-1, keepdims=True) + L2_EPS) * (GDN_HEAD_DIM ** -0.5)
            q_ref[0, :, lanes(hd)] = a.astype(BF16)
        elif c < 2 * GDN_HEADS:
            a = a * lax.rsqrt(jnp.sum(a * a, axis=-1, keepdims=True) + L2_EPS)
            k_ref[0, :, lanes(hd)] = a.astype(BF16)
        else:
            v_ref[0, :, lanes(hd)] = a.astype(BF16)
    z_ref[0] = _dot(hb, win_ref[:, COL_Z:COL_GV]).astype(BF16)

    raw_c = _dot(hb, win_ref[:, COL_BG:D_PROJ_PAD])
    lane = lax.broadcasted_iota(I32, raw_c.shape, 1)
    neg_a_c = -jnp.exp(gpc_ref[0:1, :])
    g_c = neg_a_c * _softplus(raw_c + gpc_ref[1:2, :])
    bgc_ref[0] = jnp.where(lane < GDN_HEADS, _sigmoid(raw_c), g_c)
    raw_r = _dot_nt(wbgt_ref[...], hb)
    row = lax.broadcasted_iota(I32, raw_r.shape, 0)
    neg_a_r = -jnp.exp(gpr_ref[:, 0:1])
    g_r = neg_a_r * _softplus(raw_r + gpr_ref[:, 1:2])
    bgr_ref[0] = jnp.where(row < GDN_HEADS, _sigmoid(raw_r), g_r)

    gv = _dot(hb, win_ref[:, COL_GV:COL_GG])
    gg = _dot(hb, win_ref[:, COL_GG:COL_BG])
    u = gv * _sigmoid(gg)
    for c in range(n_u):
        u_buf[c * ru + DW_TAIL:(c + 1) * ru, :] = u[:, lanes(c)]
    off_u = DW_TAIL - (DW_CONV - 1)
    daccs = []
    for c in range(n_u):
        dacc = dwb_ref[:, lanes(c)] + dww_ref[0:1, lanes(c)] * u_buf[c * ru + off_u:c * ru + off_u + tl, :]
        for j in range(1, DW_CONV):
            dacc = dacc + dww_ref[j:j + 1, lanes(c)] * u_buf[c * ru + off_u + j:c * ru + off_u + j + tl, :]
        daccs.append(dacc)
    dacc = jnp.concatenate(daccs, axis=1)
    uc_ref[0] = _silu(_layer_norm(dacc) * cng_ref[...] + cnb_ref[...]).astype(BF16)

    for c in range(n_qkv):
        nq = qkv_buf[c * rq + tl:(c + 1) * rq, :]
        qkv_buf[c * rq:c * rq + QKV_TAIL, :] = nq
        nq_ref[0, :, lanes(c)] = nq
    for c in range(n_u):
        nd = u_buf[c * ru + tl:(c + 1) * ru, :]
        u_buf[c * ru:c * ru + DW_TAIL, :] = nd
        nd_ref[0, :, lanes(c)] = nd


def _front(x, mod, sq_pad, sd_pad, prm, tl):
    b, l, _ = x.shape
    nt = l // tl
    tok = lambda w, dt: (pl.BlockSpec((1, tl, w), lambda i, j: (i, j, 0)), jax.ShapeDtypeStruct((b, l, w), dt))
    full = lambda a: pl.BlockSpec(a.shape, lambda i, j: (0,) * a.ndim)
    outs = [tok(GDN_WIDTH, BF16), tok(GDN_WIDTH, BF16), tok(GDN_WIDTH, BF16), tok(GDN_WIDTH, BF16),
            tok(LANES, F32),
            (pl.BlockSpec((1, SUBLANES, tl), lambda i, j: (i, 0, j)), jax.ShapeDtypeStruct((b, SUBLANES, l), F32)),
            tok(CONV_WIDTH, BF16),
            (pl.BlockSpec((1, QKV_TAIL, QKV_WIDTH), lambda i, j: (i, 0, 0)),
             jax.ShapeDtypeStruct((b, QKV_TAIL, QKV_WIDTH), F32)),
            (pl.BlockSpec((1, DW_TAIL, CONV_WIDTH), lambda i, j: (i, 0, 0)),
             jax.ShapeDtypeStruct((b, DW_TAIL, CONV_WIDTH), F32))]
    consts = [prm["w_in_r"], prm["w_bgt"], prm["conv_qkv_w"], prm["dw_w"], prm["dw_b"], prm["cn_g"], prm["cn_b"],
              prm["gp_c"], prm["gp_r"]]
    return pl.pallas_call(
        functools.partial(_front_body, tl=tl),
        grid=(b, nt),
        in_specs=[pl.BlockSpec((1, tl, D_MODEL), lambda i, j: (i, j, 0)),
                  pl.BlockSpec((1, 6, D_MODEL), lambda i, j: (i, 0, 0))]
                 + [full(a) for a in consts]
                 + [pl.BlockSpec((1, QKV_TAIL, QKV_WIDTH), lambda i, j: (i, 0, 0)),
                    pl.BlockSpec((1, DW_TAIL, CONV_WIDTH), lambda i, j: (i, 0, 0))],
        out_specs=[o[0] for o in outs],
        out_shape=[o[1] for o in outs],
        scratch_shapes=[pltpu.VMEM((QKV_WIDTH // LANES * (QKV_TAIL + tl), LANES), F32),
                        pltpu.VMEM((CONV_WIDTH // LANES * (DW_TAIL + tl), LANES), F32)],
        compiler_params=_cparams(("arbitrary", "arbitrary")),
        name="mixer_front",
    )(x, mod, *consts, sq_pad, sd_pad)


def _split3_bf16(x):
    hi = x.astype(BF16)
    r1 = x - hi.astype(F32)
    mid = r1.astype(BF16)
    lo = (r1 - mid.astype(F32)).astype(BF16)
    return hi, mid, lo


def _tri_inverse_stages(ms, ri, ci, out):
    c = ms[0].shape[0]
    eye = jnp.where(ri == ci, 1.0, 0.0)
    pair = (ri >> 1) == (ci >> 1)
    ps = [eye - jnp.where(pair, m, 0.0) for m in ms]
    w = 2
    while w < c:
        s = w.bit_length() - 1
        sel = ((ri >> (s + 1)) == (ci >> (s + 1))) & ((ri >> s) > (ci >> s))
        pbs = [p.astype(BF16) for p in ps]
        xs = [_dot(pb, jnp.where(sel, m, 0.0).astype(BF16)).astype(BF16) for pb, m in zip(pbs, ms)]
        yield
        ps = [p - _dot(x, pb) for p, x, pb in zip(ps, xs, pbs)]
        yield
        w *= 2
    out.extend(ps)


def _gdn_body(q_ref, k_ref, v_ref, z_ref, bgc_ref, bgr_ref, s0_ref, gnw_ref, o_ref, sout_ref,
              s_scr, ku_s, kw_s, au_s, qe_s, egl_s, *, c, g, ncg):
    cg = pl.program_id(1)
    wset = cg % 2 if ncg > 1 else 0
    rset = 1 - wset if ncg > 1 else 0

    ri = lax.broadcasted_iota(I32, (c, c), 0)
    ci = lax.broadcasted_iota(I32, (c, c), 1)
    causal = ri >= ci
    strict = ri > ci
    lower = jnp.where(causal, 1.0, 0.0).astype(BF16)
    upper = jnp.where(ri <= ci, 1.0, 0.0).astype(BF16)
    gnw = gnw_ref[...]

    heads = range(GDN_HEADS)
    hcols = [slice(hd * GDN_HEAD_DIM, (hd + 1) * GDN_HEAD_DIM) for hd in heads]

    def intra_stages():
        prob = []
        for i in range(g):
            rows = pl.ds(i * c, c)
            bgc = bgc_ref[0, rows, :]
            bgr = bgr_ref[0, i]
            gc_c = sum(_dot(lower, part) for part in _split3_bf16(bgc))
            gc_r = sum(_dot(part, upper) for part in _split3_bf16(bgr))
            for hd in heads:
                gcc = gc_c[:, GDN_HEADS + hd:GDN_HEADS + hd + 1]
                gcr = gc_r[GDN_HEADS + hd:GDN_HEADS + hd + 1, :]
                prob.append(dict(i=i, hd=hd, rows=rows, cols=hcols[hd], beta=bgc[:, hd:hd + 1], gcc=gcc,
                                 decay=jnp.exp(jnp.where(causal, gcc - gcr, NEG_BIG))))
        yield
        for p in prob:
            p["kh"] = k_ref[0, p["rows"], p["cols"]]
            p["kb"] = p["kh"].astype(F32) * p["beta"]
        ms = [jnp.where(strict, _dot_nt(p["kb"].astype(BF16), p["kh"]) * p["decay"], 0.0) for p in prob]
        yield
        t_invs = []
        yield from _tri_inverse_stages(ms, ri, ci, t_invs)
        uws = []
        for p, t_inv in zip(prob, t_invs):
            p["egc"] = jnp.exp(p["gcc"])
            vf = v_ref[0, p["rows"], p["cols"]].astype(F32)
            rhs = jnp.concatenate([vf * p["beta"], p["kb"] * p["egc"]], axis=1).astype(BF16)
            uws.append(_dot(t_inv.astype(BF16), rhs).astype(BF16))
        yield
        for p in prob:
            p["qh"] = q_ref[0, p["rows"], p["cols"]]
            p["a"] = jnp.where(causal, _dot_nt(p["qh"], p["kh"]) * p["decay"], 0.0).astype(BF16)
            p["g_last"] = p["gcc"][c - 1:c, :]
            p["kd"] = (p["kh"].astype(F32) * jnp.exp(p["g_last"] - p["gcc"])).astype(BF16)
        yield
        kuws = [_dot_tn(p["kd"], uw) for p, uw in zip(prob, uws)]
        yield
        auws = [_dot(p["a"], uw) for p, uw in zip(prob, uws)]
        yield
        for p, kuw, auw in zip(prob, kuws, auws):
            rows, cols, hd = p["rows"], p["cols"], p["hd"]
            srows = pl.ds(p["i"] * GDN_HEAD_DIM, GDN_HEAD_DIM)
            ku_s[wset, srows, cols] = kuw[:, :GDN_HEAD_DIM]
            kw_s[wset, srows, cols] = kuw[:, GDN_HEAD_DIM:].astype(BF16)
            au_s[wset, rows, cols] = auw[:, :GDN_HEAD_DIM]
            qe_s[wset, rows, cols] = (p["qh"].astype(F32) * p["egc"] - auw[:, GDN_HEAD_DIM:]).astype(BF16)
            egl_s[wset, pl.ds(p["i"] * SUBLANES + hd, 1), :] = jnp.broadcast_to(jnp.exp(p["g_last"]), (1, LANES))

    def inter(i):
        rows = pl.ds(i * c, c)
        srows = pl.ds(i * GDN_HEAD_DIM, GDN_HEAD_DIM)
        ss = [s_scr[hd] for hd in heads]
        sbs = [s.astype(BF16) for s in ss]
        upd = [_dot(kw_s[rset, srows, hcols[hd]], sbs[hd]) for hd in heads]
        for hd in heads:
            s_scr[hd] = ss[hd] * egl_s[rset, pl.ds(i * SUBLANES + hd, 1), :] + (ku_s[rset, srows, hcols[hd]] - upd[hd])
        os_ = [_dot(qe_s[rset, rows, hcols[hd]], sbs[hd]) + au_s[rset, rows, hcols[hd]] for hd in heads]
        for hd in heads:
            o = os_[hd]
            gate = _silu(z_ref[0, rows, hcols[hd]].astype(F32))
            on = o * lax.rsqrt(jnp.mean(o * o, axis=-1, keepdims=True) + RMS_EPS) * gnw * gate
            o_ref[0, rows, hcols[hd]] = on.astype(BF16)

    def run(do_intra, do_inter):
        stages = intra_stages() if do_intra else iter(())
        todo = list(range(g)) if do_inter else []
        n_stages = 6 + 2 * (c.bit_length() - 2)
        every = max(1, n_stages // max(1, len(todo)))
        for n, _ in enumerate(stages):
            if todo and n % every == 0:
                inter(todo.pop(0))
        for i in todo:
            inter(i)

    if ncg == 1:
        s_scr[...] = s0_ref[0]
        run(True, False)
        run(False, True)
        sout_ref[0] = s_scr[...]
        return

    @pl.when(cg == 0)
    def _():
        s_scr[...] = s0_ref[0]
        run(True, False)

    @pl.when((cg > 0) & (cg < ncg))
    def _():
        run(True, True)

    @pl.when(cg == ncg)
    def _():
        run(False, True)
        sout_ref[0] = s_scr[...]


def _gdn(q, k, v, z, bgc, bgr, s0, gnw, c, g):
    b, l, _ = q.shape
    ncg = l // (c * g)
    cur = lambda w: pl.BlockSpec((1, c * g, w), lambda i, j: (i, jnp.minimum(j, ncg - 1), 0))
    prev = lambda w: pl.BlockSpec((1, c * g, w), lambda i, j: (i, jnp.maximum(j - 1, 0), 0))
    st = pl.BlockSpec((1, GDN_HEADS, GDN_HEAD_DIM, GDN_HEAD_DIM), lambda i, j: (i, 0, 0, 0))
    tok_buf = lambda dt: pltpu.VMEM((2, c * g, GDN_WIDTH), dt)
    state_buf = lambda dt: pltpu.VMEM((2, g * GDN_HEAD_DIM, GDN_WIDTH), dt)
    return pl.pallas_call(
        functools.partial(_gdn_body, c=c, g=g, ncg=ncg),
        grid=(b, ncg + 1 if ncg > 1 else 1),
        in_specs=[cur(GDN_WIDTH), cur(GDN_WIDTH), cur(GDN_WIDTH), prev(GDN_WIDTH), cur(LANES),
                  pl.BlockSpec((1, g, SUBLANES, c), lambda i, j: (i, jnp.minimum(j, ncg - 1), 0, 0)), st,
                  pl.BlockSpec((1, GDN_HEAD_DIM), lambda i, j: (0, 0))],
        out_specs=[prev(GDN_WIDTH), st],
        out_shape=[jax.ShapeDtypeStruct((b, l, GDN_WIDTH), BF16), jax.ShapeDtypeStruct(s0.shape, F32)],
        scratch_shapes=[pltpu.VMEM((GDN_HEADS, GDN_HEAD_DIM, GDN_HEAD_DIM), F32),
                        state_buf(F32), state_buf(BF16), tok_buf(F32), tok_buf(BF16),
                        pltpu.VMEM((2, g * SUBLANES, LANES), F32)],
        compiler_params=_cparams(("arbitrary", "arbitrary")),
        name="gdn",
    )(q, k, v, z, bgc, bgr, s0, gnw)


HALF = D_MODEL // 2
HI_MASK = 0xFFFF0000


def _pack_bf16_pairs(x):
    bits = pltpu.bitcast(x.astype(BF16).astype(F32), U32)
    return (bits[:, :HALF] >> 16) | (bits[:, HALF:] & jnp.uint32(HI_MASK))


def _unpack_bf16_pairs(p):
    return pltpu.bitcast(p << 16, F32), pltpu.bitcast(p & jnp.uint32(HI_MASK), F32)


ROW_TILES = HALF // LANES


def _row_lines(r):
    return pl.ds(r * ROW_TILES, ROW_TILES)


def _store_rows(ref, lead, r0, packed):
    n = packed.shape[0]
    for q in range(ROW_TILES):
        ref[(*lead, pl.ds(r0 * ROW_TILES + q, n, stride=ROW_TILES), slice(None))] = packed[:, q * LANES:(q + 1) * LANES]


def _load_rows(ref, lead, r0, n):
    return jnp.concatenate(
        [ref[(*lead, pl.ds(r0 * ROW_TILES + q, n, stride=ROW_TILES), slice(None))] for q in range(ROW_TILES)], axis=1)


def _mid_body(o_ref, uc_ref, x_ref, gt1_ref, sh2_ref, sc2_ref,
              wout_ref, g1_ref, b1_ref, wrh_ref, wrl_ref, pre_ref, h2p_ref, lgt_ref):
    gt1, sh2, sc2 = gt1_ref[0], sh2_ref[0], sc2_ref[0]
    mix = _dot(jnp.concatenate([o_ref[0], uc_ref[0]], axis=1), wout_ref[...])
    x1 = _layer_norm(ALPHA * x_ref[0] + (1.0 + gt1) * mix) * g1_ref[...] + b1_ref[...]
    h2 = _layer_norm(x1) * (1.0 + sc2) + sh2
    hh, hl = _split_bf16(h2)
    lgt_ref[...] = _dot_nt(wrh_ref[...], hh) + (_dot_nt(wrh_ref[...], hl) + _dot_nt(wrl_ref[...], hh))
    pre_ref[0] = ALPHA * x1
    _store_rows(h2p_ref, (0,), 0, _pack_bf16_pairs(h2))


def _mod_spec(m, tl):
    if m.shape[1] == 1:
        return pl.BlockSpec((1, 1, D_MODEL), lambda i, j: (i, 0, 0))
    return pl.BlockSpec((1, tl, D_MODEL), lambda i, j: (i, j, 0))


def _mid(o, uc, x, mods, prm, tl):
    b, l, _ = x.shape
    nt = l // tl
    full = lambda a: pl.BlockSpec(a.shape, lambda i, j: (0,) * a.ndim)
    consts = [prm["w_out"], prm["ln1_g"], prm["ln1_b"], prm["wr_hi"], prm["wr_lo"]]
    return pl.pallas_call(
        _mid_body,
        grid=(b, nt),
        in_specs=[pl.BlockSpec((1, tl, GDN_WIDTH), lambda i, j: (i, j, 0)),
                  pl.BlockSpec((1, tl, CONV_WIDTH), lambda i, j: (i, j, 0)),
                  pl.BlockSpec((1, tl, D_MODEL), lambda i, j: (i, j, 0))]
                 + [_mod_spec(m, tl) for m in mods] + [full(a) for a in consts],
        out_specs=[pl.BlockSpec((1, tl, D_MODEL), lambda i, j: (i, j, 0)),
                   pl.BlockSpec((1, tl * ROW_TILES, LANES), lambda i, j: (i, j, 0)),
                   pl.BlockSpec((N_EXPERTS, tl), lambda i, j: (0, i * nt + j))],
        out_shape=[jax.ShapeDtypeStruct((b, l, D_MODEL), F32), jax.ShapeDtypeStruct((b, l * ROW_TILES, LANES), U32),
                   jax.ShapeDtypeStruct((N_EXPERTS, b * l), F32)],
        compiler_params=_cparams(("arbitrary", "arbitrary")),
        name="mixer_out",
    )(o, uc, x, *mods, *consts)


def _first_max(x, row, n):
    m = jnp.max(x, axis=0, keepdims=True)
    ix = jnp.min(jnp.where(x == m, row, float(n)), axis=0, keepdims=True)
    return m, ix


def _route_body(lgp_ref, lgs_ref, bias_ref, idx_ref, w_ref, rank_ref, cnt_ref, cnt_scr, *, tr, n_prompt_tiles):
    @pl.when(pl.program_id(0) == 0)
    def _():
        cnt_scr[...] = jnp.zeros_like(cnt_scr)

    neg = -jnp.inf
    logits = jnp.where(pl.program_id(0) < n_prompt_tiles, lgp_ref[...], lgs_ref[...])
    scores = _sigmoid(logits)
    sel = scores + bias_ref[:, 0:1]
    row_g = lax.broadcasted_iota(I32, (GROUP_SIZE, tr), 0).astype(F32)
    gs = []
    for g in range(N_GROUPS):
        blk = sel[g * GROUP_SIZE:(g + 1) * GROUP_SIZE, :]
        m1, i1 = _first_max(blk, row_g, GROUP_SIZE)
        m2 = jnp.max(jnp.where(row_g == i1, neg, blk), axis=0, keepdims=True)
        gs.append(m1 + m2)
    gs = jnp.concatenate(gs, axis=0)
    row_n = lax.broadcasted_iota(I32, (N_GROUPS, tr), 0).astype(F32)
    chosen = jnp.zeros((N_GROUPS, tr), F32)
    for _ in range(TOPK_GROUPS):
        _, ix = _first_max(gs, row_n, N_GROUPS)
        hit = row_n == ix
        chosen = jnp.where(hit, 1.0, chosen)
        gs = jnp.where(hit, neg, gs)
    selm = jnp.concatenate(
        [jnp.where(chosen[g:g + 1, :] > 0.5, sel[g * GROUP_SIZE:(g + 1) * GROUP_SIZE, :], neg) for g in range(N_GROUPS)],
        axis=0)
    row_e = lax.broadcasted_iota(I32, (N_EXPERTS, tr), 0).astype(F32)
    idxs, ws = [], []
    picked = jnp.zeros((N_EXPERTS, tr), F32)
    for _ in range(TOP_K):
        _, ix = _first_max(selm, row_e, N_EXPERTS)
        hit = row_e == ix
        ws.append(jnp.sum(jnp.where(hit, scores, 0.0), axis=0, keepdims=True))
        idxs.append(ix)
        selm = jnp.where(hit, neg, selm)
        picked = jnp.where(hit, 1.0, picked)
    wsum = ws[0]
    for wk in ws[1:]:
        wsum = wsum + wk
    idx_ref[...] = jnp.concatenate(idxs, axis=0).astype(I32)
    w_ref[...] = jnp.concatenate(ws, axis=0) / wsum * ROUTED_SCALE
    ti = lax.broadcasted_iota(I32, (tr, tr), 0)
    tj = lax.broadcasted_iota(I32, (tr, tr), 1)
    before = _dot(picked.astype(BF16), jnp.where(ti < tj, 1.0, 0.0).astype(BF16)) + cnt_scr[:, 0:1]
    rank_ref[...] = jnp.concatenate(
        [jnp.sum(jnp.where(row_e == ix, before, 0.0), axis=0, keepdims=True) for ix in idxs], axis=0).astype(I32)
    cnt_scr[...] = cnt_scr[...] + jnp.sum(picked, axis=1, keepdims=True)
    cnt_ref[...] = cnt_scr[...]


def _route(logits_p, logits_s, bias_b, tr):
    ntp, nts = logits_p.shape[1] // tr, logits_s.shape[1] // tr
    t = (ntp + nts) * tr
    kt = lambda dt: (pl.BlockSpec((TOP_K, tr), lambda i: (0, i)), jax.ShapeDtypeStruct((TOP_K, t), dt))
    outs = [kt(I32), kt(F32), kt(I32),
            (pl.BlockSpec((N_EXPERTS, LANES), lambda i: (0, 0)), jax.ShapeDtypeStruct((N_EXPERTS, LANES), F32))]
    return pl.pallas_call(
        functools.partial(_route_body, tr=tr, n_prompt_tiles=ntp),
        grid=(ntp + nts,),
        in_specs=[pl.BlockSpec((N_EXPERTS, tr), lambda i: (0, jnp.minimum(i, ntp - 1))),
                  pl.BlockSpec((N_EXPERTS, tr), lambda i: (0, jnp.maximum(i - ntp, 0))),
                  pl.BlockSpec((N_EXPERTS, LANES), lambda i: (0, 0))],
        out_specs=[o[0] for o in outs],
        out_shape=[o[1] for o in outs],
        scratch_shapes=[pltpu.VMEM((N_EXPERTS, LANES), F32)],
        compiler_params=_cparams(("arbitrary",)),
        name="route",
    )(logits_p, logits_s, bias_b)


def _pos_body(idx_ref, rank_ref, cnt_ref, pos_ref, *, tr):
    ei = lax.broadcasted_iota(I32, (N_EXPERTS, N_EXPERTS), 0)
    ej = lax.broadcasted_iota(I32, (N_EXPERTS, N_EXPERTS), 1)
    below = jnp.where(ej < ei, 1.0, 0.0).astype(BF16)
    start = sum(_dot(below, part) for part in _split3_bf16(cnt_ref[...]))[:, 0:1]
    row_e = lax.broadcasted_iota(I32, (N_EXPERTS, tr), 0)
    pos_ref[...] = jnp.concatenate(
        [jnp.sum(jnp.where(row_e == idx_ref[k:k + 1, :], start, 0.0), axis=0, keepdims=True) for k in range(TOP_K)],
        axis=0).astype(I32) + rank_ref[...]


def _positions(idx, rank, cnt, tr):
    t = idx.shape[1]
    kt = pl.BlockSpec((TOP_K, tr), lambda i: (0, i))
    return pl.pallas_call(
        functools.partial(_pos_body, tr=tr),
        grid=(t // tr,),
        in_specs=[kt, kt, pl.BlockSpec((N_EXPERTS, LANES), lambda i: (0, 0))],
        out_specs=kt,
        out_shape=jax.ShapeDtypeStruct((TOP_K, t), I32),
        compiler_params=_cparams(("arbitrary",)),
        name="positions",
    )(idx, rank, cnt)


def _dispatch_body(pos_ref, hp_ref, hs_ref, wsgu_ref, wsd_ref, xs_ref, shared_ref, sem, *, tt, n_prompt_tiles):
    i = pl.program_id(0)

    def scatter(src_ref):
        def body(t, carry):
            for k in range(TOP_K):
                pltpu.make_async_copy(src_ref.at[_row_lines(t)], xs_ref.at[_row_lines(pos_ref[0, 0, t * TOP_K + k])],
                                      sem).start(priority=k % 2)
            return carry
        lax.fori_loop(0, tt, body, 0)
        x_lo, x_hi = _unpack_bf16_pairs(_load_rows(src_ref, (), 0, tt))
        gu = _dot(x_lo.astype(BF16), wsgu_ref[:HALF, :]) + _dot(x_hi.astype(BF16), wsgu_ref[HALF:, :])
        act = _silu(gu[:, :D_SHARED]) * gu[:, D_SHARED:]
        shared_ref[...] = _dot(act.astype(BF16), wsd_ref[...])
        for _ in range(TOP_K):
            pltpu.make_async_copy(src_ref, xs_ref.at[pl.ds(0, tt * ROW_TILES)], sem).wait()

    @pl.when(i < n_prompt_tiles)
    def _():
        scatter(hp_ref)

    @pl.when(i >= n_prompt_tiles)
    def _():
        scatter(hs_ref)


def _dispatch(pos_tiles, h_prompt, h_sample, ws_gu, ws_down, tt):
    tp, ts = h_prompt.shape[0] // ROW_TILES, h_sample.shape[0] // ROW_TILES
    ntp, nts = tp // tt, ts // tt
    n_rows = TOP_K * (tp + ts)
    return pl.pallas_call(
        functools.partial(_dispatch_body, tt=tt, n_prompt_tiles=ntp),
        grid=(ntp + nts,),
        in_specs=[pl.BlockSpec((1, 1, tt * TOP_K), lambda i: (i, 0, 0), memory_space=pltpu.SMEM),
                  pl.BlockSpec((tt * ROW_TILES, LANES), lambda i: (jnp.minimum(i, ntp - 1), 0)),
                  pl.BlockSpec((tt * ROW_TILES, LANES), lambda i: (jnp.maximum(i - ntp, 0), 0)),
                  pl.BlockSpec(ws_gu.shape, lambda i: (0, 0)), pl.BlockSpec(ws_down.shape, lambda i: (0, 0))],
        out_specs=[pl.BlockSpec(memory_space=pl.ANY), pl.BlockSpec((tt, D_MODEL), lambda i: (i, 0))],
        out_shape=[jax.ShapeDtypeStruct((n_rows * ROW_TILES, LANES), U32),
                   jax.ShapeDtypeStruct((tp + ts, D_MODEL), F32)],
        scratch_shapes=[pltpu.SemaphoreType.DMA(())],
        compiler_params=_cparams(("arbitrary",)),
        name="dispatch",
    )(pos_tiles, h_prompt, h_sample, ws_gu, ws_down)


GMM_ROWS = 2048
GMM_SUB = 512


def _gmm_body(ve_ref, vb_ref, vlo_ref, vhi_ref, vfirst_ref, vnew_ref,
              xs_ref, wg_ref, wu_ref, wd_ref, ys_ref, wgu_s, wd_s):
    v = pl.program_id(0)
    lo = vlo_ref[v]
    hi = vhi_ref[v]

    @pl.when(vnew_ref[v] == 1)
    def _():
        wgu_s[:, :D_EXPERT] = wg_ref[0].astype(BF16)
        wgu_s[:, D_EXPERT:] = wu_ref[0].astype(BF16)
        wd_s[...] = wd_ref[0].astype(BF16)

    @pl.when(vfirst_ref[v] == 1)
    def _():
        ys_ref[...] = jnp.zeros_like(ys_ref)

    def sub_block(s, carry):
        r0 = pl.multiple_of(s * GMM_SUB, GMM_SUB)
        x_lo, x_hi = _unpack_bf16_pairs(_load_rows(xs_ref, (), r0, GMM_SUB))
        gu = _dot(x_lo.astype(BF16), wgu_s[:HALF, :]) + _dot(x_hi.astype(BF16), wgu_s[HALF:, :])
        act = _silu(gu[:, :D_EXPERT]) * gu[:, D_EXPERT:]
        y = _pack_bf16_pairs(_dot(act.astype(BF16), wd_s[...]))
        row = lax.broadcasted_iota(I32, y.shape, 0) + r0
        _store_rows(ys_ref, (), r0, jnp.where((row >= lo) & (row < hi), y, _load_rows(ys_ref, (), r0, GMM_SUB)))
        return carry

    lax.fori_loop(lo // GMM_SUB, (hi + GMM_SUB - 1) // GMM_SUB, sub_block, 0)


def _gmm_schedule(counts, n_rows):
    nb = n_rows // GMM_ROWS
    n_vis = nb + N_EXPERTS
    ends = jnp.cumsum(counts)
    starts = ends - counts
    first_blk = starts // GMM_ROWS
    last_blk = jnp.maximum(ends - 1, 0) // GMM_ROWS
    per_e = jnp.where(counts > 0, last_blk - first_blk + 1, 0)
    vis_end = jnp.cumsum(per_e)
    total = vis_end[-1]
    v = jnp.minimum(jnp.arange(n_vis, dtype=I32), total - 1)
    e = jnp.minimum(jnp.sum((vis_end[None, :] <= v[:, None]).astype(I32), axis=1), N_EXPERTS - 1)
    table = jnp.stack([first_blk, vis_end - per_e, starts, ends], axis=1).astype(I32)
    pick = e[:, None] == jnp.arange(N_EXPERTS, dtype=I32)[None, :]
    fb, v0, st, en = jnp.moveaxis(jnp.sum(jnp.where(pick[:, :, None], table[None], 0), axis=1), 1, 0)
    blk = fb + (v - v0)
    lo = jnp.maximum(st, blk * GMM_ROWS) - blk * GMM_ROWS
    hi = jnp.minimum(en, (blk + 1) * GMM_ROWS) - blk * GMM_ROWS
    hi = jnp.where(jnp.arange(n_vis) < total, hi, lo)
    prev = lambda a: jnp.concatenate([jnp.full((1,), -1, I32), a[:-1]])
    first = (blk != prev(blk)).astype(I32)
    new_e = (e != prev(e)).astype(I32)
    return e, blk, lo.astype(I32), hi.astype(I32), first, new_e


def _gmm(xs, counts, we_gate, we_up, we_down):
    n_rows = xs.shape[0] // ROW_TILES
    sched = _gmm_schedule(counts, n_rows)
    n_vis = sched[0].shape[0]
    grid_spec = pltpu.PrefetchScalarGridSpec(
        num_scalar_prefetch=len(sched),
        grid=(n_vis,),
        in_specs=[pl.BlockSpec((GMM_ROWS * ROW_TILES, LANES), lambda v, ve, vb, *_: (vb[v], 0)),
                  pl.BlockSpec((1, D_MODEL, D_EXPERT), lambda v, ve, *_: (ve[v], 0, 0)),
                  pl.BlockSpec((1, D_MODEL, D_EXPERT), lambda v, ve, *_: (ve[v], 0, 0)),
                  pl.BlockSpec((1, D_EXPERT, D_MODEL), lambda v, ve, *_: (ve[v], 0, 0))],
        out_specs=pl.BlockSpec((GMM_ROWS * ROW_TILES, LANES), lambda v, ve, vb, *_: (vb[v], 0)),
        scratch_shapes=[pltpu.VMEM((D_MODEL, 2 * D_EXPERT), BF16), pltpu.VMEM((D_EXPERT, D_MODEL), BF16)])
    return pl.pallas_call(
        _gmm_body,
        grid_spec=grid_spec,
        out_shape=jax.ShapeDtypeStruct(xs.shape, U32),
        compiler_params=_cparams(("arbitrary",)),
        name="expert_ffn",
    )(*sched, xs, we_gate, we_up, we_down)


def _combine_body(pos_ref, posn_ref, w_ref, pre_ref, shared_ref, gt2_ref, g2_ref, b2_ref, ys_ref, out_ref, gbuf, sem,
                  *, tt, n_steps):
    step = pl.program_id(0) * pl.num_programs(1) + pl.program_id(1)
    slot = step % 2

    def gather(p_ref, into):
        def body(t, carry):
            for k in range(TOP_K):
                pltpu.make_async_copy(ys_ref.at[_row_lines(p_ref[0, 0, t * TOP_K + k])],
                                      gbuf.at[into, k, _row_lines(t)], sem.at[into]).start(priority=k % 2)
            return carry
        lax.fori_loop(0, tt, body, 0)

    @pl.when(step == 0)
    def _():
        gather(pos_ref, 0)

    @pl.when(step + 1 < n_steps)
    def _():
        gather(posn_ref, 1 - slot)

    pltpu.make_async_copy(gbuf.at[slot], gbuf.at[slot], sem.at[slot]).wait()
    acc_lo = jnp.zeros((tt, HALF), F32)
    acc_hi = jnp.zeros((tt, HALF), F32)
    for k in range(TOP_K):
        y_lo, y_hi = _unpack_bf16_pairs(_load_rows(gbuf, (slot, k), 0, tt))
        wk = w_ref[:, k:k + 1]
        acc_lo = acc_lo + wk * y_lo
        acc_hi = acc_hi + wk * y_hi
    routed = jnp.concatenate([acc_lo, acc_hi], axis=1)
    y = pre_ref[0] + (1.0 + gt2_ref[0]) * (shared_ref[...] + routed)
    out_ref[0] = _layer_norm(y) * g2_ref[...] + b2_ref[...]


def _combine(pos_tiles, w_tk, pre, shared, gt2, ln2_g, ln2_b, ys, tt, tok0):
    b, l, _ = pre.shape
    nt = l // tt
    blk0 = tok0 // tt
    last = blk0 + b * nt - 1
    pos_spec = lambda nxt: pl.BlockSpec((1, 1, tt * TOP_K),
                                        lambda i, j: (jnp.minimum(blk0 + i * nt + j + nxt, last), 0, 0),
                                        memory_space=pltpu.SMEM)
    return pl.pallas_call(
        functools.partial(_combine_body, tt=tt, n_steps=b * nt),
        grid=(b, nt),
        in_specs=[pos_spec(0), pos_spec(1),
                  pl.BlockSpec((tt, TOP_K), lambda i, j: (blk0 + i * nt + j, 0)),
                  pl.BlockSpec((1, tt, D_MODEL), lambda i, j: (i, j, 0)),
                  pl.BlockSpec((tt, D_MODEL), lambda i, j: (blk0 + i * nt + j, 0)),
                  _mod_spec(gt2, tt),
                  pl.BlockSpec((1, D_MODEL), lambda i, j: (0, 0)),
                  pl.BlockSpec((1, D_MODEL), lambda i, j: (0, 0)),
                  pl.BlockSpec(memory_space=pl.ANY)],
        out_specs=pl.BlockSpec((1, tt, D_MODEL), lambda i, j: (i, j, 0)),
        out_shape=jax.ShapeDtypeStruct((b, l, D_MODEL), F32),
        scratch_shapes=[pltpu.VMEM((2, TOP_K, tt * ROW_TILES, LANES), U32), pltpu.SemaphoreType.DMA((2,))],
        compiler_params=_cparams(("arbitrary", "arbitrary")),
        name="combine",
    )(pos_tiles, pos_tiles, w_tk, pre, shared, gt2, ln2_g, ln2_b, ys)


def _prep_params(w_in, conv_qkv_w, a_log, dt_bias, dw_w, dw_b, cn_g, cn_b):
    z0 = QKV_WIDTH
    b0 = z0 + GDN_WIDTH
    g0 = b0 + 2 * GDN_HEADS
    w_bg = w_in[:, b0:g0]
    w_in_r = jnp.concatenate(
        [w_in[:, :b0], w_in[:, g0:], w_bg, jnp.zeros((D_MODEL, LANES - 2 * GDN_HEADS), w_in.dtype)], axis=1).astype(BF16)
    pad_h = jnp.zeros((GDN_HEADS,), F32)
    al = jnp.concatenate([pad_h, a_log.astype(F32)])
    db = jnp.concatenate([pad_h, dt_bias.astype(F32)])
    gp_c = jnp.zeros((SUBLANES, LANES), F32).at[0, :2 * GDN_HEADS].set(al).at[1, :2 * GDN_HEADS].set(db)
    gp_r = jnp.zeros((SUBLANES, LANES), F32).at[:, 0].set(al).at[:, 1].set(db)
    return dict(
        w_in_r=w_in_r, w_bgt=w_bg.T.astype(BF16), conv_qkv_w=conv_qkv_w.astype(F32),
        dw_w=jnp.concatenate([dw_w, jnp.zeros((1, CONV_WIDTH), dw_w.dtype)], axis=0).astype(F32),
        dw_b=dw_b.reshape(1, -1).astype(F32), cn_g=cn_g.reshape(1, -1).astype(F32), cn_b=cn_b.reshape(1, -1).astype(F32),
        gp_c=gp_c, gp_r=gp_r)


def _prep_mid_params(w_out, ln1_g, ln1_b, w_router, router_bias, ws_gate, ws_up, ws_down, ln2_g, ln2_b, gdn_norm_w):
    row = lambda a: a.reshape(1, -1).astype(F32)
    wr_t = w_router.astype(F32).T
    wr_hi = wr_t.astype(BF16)
    return dict(
        w_out=w_out.astype(BF16), ln1_g=row(ln1_g), ln1_b=row(ln1_b),
        ws_gu=jnp.concatenate([ws_gate, ws_up], axis=1).astype(BF16), ws_down=ws_down.astype(BF16),
        wr_hi=wr_hi, wr_lo=(wr_t - wr_hi.astype(F32)).astype(BF16),
        bias_b=jnp.broadcast_to(router_bias.astype(F32)[:, None], (N_EXPERTS, LANES)),
        ln2_g=row(ln2_g), ln2_b=row(ln2_b), gnw=row(gdn_norm_w))


def _tile(n, pref):
    t = min(pref, n)
    while n % t:
        t //= 2
    return t


def _token_mixer(x, mod, s_gdn, s_qkv, s_dw, prm, mprm):
    b, l, _ = x.shape
    assert l >= DW_CONV - 1 and l % SUBLANES == 0
    tl = _tile(l, 256)
    sq_pad = jnp.pad(s_qkv.astype(F32), ((0, 0), (QKV_TAIL - (GDN_CONV - 1), 0), (0, 0)))
    sd_pad = jnp.pad(s_dw.astype(F32), ((0, 0), (DW_TAIL - (DW_CONV - 1), 0), (0, 0)))
    q, k, v, z, bgc, bgr, uc, nq, nd = _front(x, mod, sq_pad, sd_pad, prm, tl)
    c = min(CHUNK, l)
    n_chunks = l // c
    g = _tile(n_chunks, 8)
    bgr = bgr.reshape(b, SUBLANES, n_chunks, c).transpose(0, 2, 1, 3)
    o, s_new = _gdn(q, k, v, z, bgc, bgr, s_gdn.astype(F32), mprm["gnw"], c, g)
    return o, uc, s_new, nq[:, QKV_TAIL - (GDN_CONV - 1):], nd[:, DW_TAIL - (DW_CONV - 1):]


def kernel(x_prompt, x_sample, state_gdn, state_qkv_conv, state_dw_conv, c_prompt, c_sample, w_ada, b_ada, w_in, conv_qkv_w, a_log, dt_bias, gdn_norm_w, dw_w, dw_b, cn_g, cn_b, w_out, ln1_g, ln1_b, w_router, router_bias, we_gate, we_up, we_down, ws_gate, ws_up, ws_down, ln2_g, ln2_b):
    bp, lp, _ = x_prompt.shape
    bs, ls, _ = x_sample.shape
    tp, ts = bp * lp, bs * ls
    yp, ys = x_prompt, x_sample
    c_all = jnp.concatenate([c_prompt, c_sample], axis=0)
    new_p, new_s = [], []
    for l in range(w_ada.shape[0]):
        prm = _prep_params(w_in[l], conv_qkv_w[l], a_log[l], dt_bias[l], dw_w[l], dw_b[l], cn_g[l], cn_b[l])
        mprm = _prep_mid_params(w_out[l], ln1_g[l], ln1_b[l], w_router[l], router_bias[l], ws_gate[l], ws_up[l],
                                ws_down[l], ln2_g[l], ln2_b[l], gdn_norm_w[l])
        mod = _ada_mod(c_all, w_ada[l], b_ada[l]).reshape(bp + bs, 6, D_MODEL)
        mod_p, mod_s = mod[:bp], mod[bp:]

        zg = jnp.zeros((bp, GDN_HEADS, GDN_HEAD_DIM, GDN_HEAD_DIM), F32)
        zq = jnp.zeros((bp, GDN_CONV - 1, QKV_WIDTH), F32)
        zd = jnp.zeros((bp, DW_CONV - 1, CONV_WIDTH), F32)
        o_p, uc_p, g_p, q_p, d_p = _token_mixer(yp, mod_p, zg, zq, zd, prm, mprm)
        o_s, uc_s, g_s, q_s, d_s = _token_mixer(ys, mod_s, state_gdn[l], state_qkv_conv[l], state_dw_conv[l], prm, mprm)

        mods_p = tuple(mod_p[:, j:j + 1, :] for j in (2, 3, 4, 5))
        mods_s = tuple(jnp.repeat(mod_s[:, j, :], ls, axis=0)[None] for j in (2, 3, 4, 5))
        flat = lambda a: a.reshape(1, ts, a.shape[-1])
        pre_p, h_p, lg_p = _mid(o_p, uc_p, yp, mods_p[:3], mprm, _tile(lp, 512))
        pre_s, h_s, lg_s = _mid(flat(o_s), flat(uc_s), flat(ys), mods_s[:3], mprm, _tile(ts, 256))

        tt = _tile(ts, 512)
        assert tp % tt == 0 and lp % tt == 0 and (TOP_K * (tp + ts)) % GMM_ROWS == 0
        idx, w, rank, cnt = _route(lg_p, lg_s, mprm["bias_b"], tt)
        pos = _positions(idx, rank, cnt, tt)
        pos_tiles = pos.T.reshape((tp + ts) // tt, 1, tt * TOP_K)
        xs, shared = _dispatch(pos_tiles, h_p.reshape(tp * ROW_TILES, LANES), h_s.reshape(ts * ROW_TILES, LANES),
                               mprm["ws_gu"], mprm["ws_down"], tt)
        ye = _gmm(xs, cnt[:, 0].astype(I32), we_gate[l], we_up[l], we_down[l])
        w_tk = w.T
        yp = _combine(pos_tiles, w_tk, pre_p, shared, mods_p[3], mprm["ln2_g"], mprm["ln2_b"], ye, tt, 0)
        ys = _combine(pos_tiles, w_tk, pre_s, shared, mods_s[3], mprm["ln2_g"], mprm["ln2_b"], ye, tt, tp
                      ).reshape(bs, ls, D_MODEL)
        new_p.append((g_p.astype(state_gdn.dtype), q_p.astype(x_prompt.dtype), d_p.astype(x_prompt.dtype)))
        new_s.append((g_s.astype(state_gdn.dtype), q_s.astype(state_qkv_conv.dtype), d_s.astype(state_dw_conv.dtype)))
    stack = lambda rows, j: jnp.stack([r[j] for r in rows])
    return (yp, ys, stack(new_p, 0), stack(new_p, 1), stack(new_p, 2), stack(new_s, 0), stack(new_s, 1), stack(new_s, 2))
```

```python
import functools

import jax
import jax.numpy as jnp
from jax import lax
from jax.experimental import pallas as pl
from jax.experimental.pallas import tpu as pltpu

F32 = jnp.float32
BF16 = jnp.bfloat16
I32 = jnp.int32
U32 = jnp.uint32

D_MODEL = 1024
GDN_WIDTH = 512
CONV_WIDTH = 512
GDN_HEAD_DIM = 128
GDN_HEADS = 4
QKV_WIDTH = 3 * GDN_WIDTH
GDN_CONV = 4
DW_CONV = 31
CHUNK = 64
N_EXPERTS = 256
TOP_K = 8
N_GROUPS = 8
GROUP_SIZE = N_EXPERTS // N_GROUPS
TOPK_GROUPS = 4
D_EXPERT = 256
D_SHARED = 256
ROUTED_SCALE = 2.5
LN_EPS = 1e-5
RMS_EPS = 1e-6
L2_EPS = 1e-6
DEPTH = 1
ALPHA = (2.0 * DEPTH) ** 0.25

LANES = 128
SUBLANES = 8
VMEM_LIMIT_BYTES = 56 * 1024 * 1024

COL_Z = QKV_WIDTH
COL_GV = COL_Z + GDN_WIDTH
COL_GG = COL_GV + CONV_WIDTH
COL_BG = COL_GG + CONV_WIDTH
D_PROJ_PAD = COL_BG + LANES
QKV_TAIL = SUBLANES
DW_TAIL = 32
NEG_BIG = -1e30


def _cparams(sem):
    return pltpu.CompilerParams(dimension_semantics=sem, vmem_limit_bytes=VMEM_LIMIT_BYTES)


def _split_bf16(x):
    hi = x.astype(BF16)
    lo = (x - hi.astype(F32)).astype(BF16)
    return hi, lo


def _dot(a, b):
    return jnp.dot(a, b, preferred_element_type=F32)


def _dot_nt(a, b):
    return lax.dot_general(a, b, (((1,), (1,)), ((), ())), preferred_element_type=F32)


def _dot_tn(a, b):
    return lax.dot_general(a, b, (((0,), (0,)), ((), ())), preferred_element_type=F32)


def _dot_hp(a, b):
    ah, al = _split_bf16(a)
    bh, bl = _split_bf16(b)
    return _dot(ah, bh) + (_dot(ah, bl) + _dot(al, bh))


def _sigmoid(x):
    return 1.0 / (1.0 + jnp.exp(-x))


def _silu(x):
    return x * _sigmoid(x)


def _softplus(x):
    return jnp.maximum(x, 0.0) + jnp.log(1.0 + jnp.exp(-jnp.abs(x)))


def _layer_norm(x):
    mu = jnp.mean(x, axis=-1, keepdims=True)
    xc = x - mu
    var = jnp.mean(xc * xc, axis=-1, keepdims=True)
    return xc * lax.rsqrt(var + LN_EPS)


def _ada_body(c_ref, w_ref, b_ref, o_ref):
    o_ref[...] = _dot_hp(_silu(c_ref[...]), w_ref[...]) + b_ref[...]


def _ada_mod(c, w_ada, b_ada):
    bt = c.shape[0]
    n_col = w_ada.shape[1] // D_MODEL
    return pl.pallas_call(
        _ada_body,
        grid=(n_col,),
        in_specs=[
            pl.BlockSpec((bt, D_MODEL), lambda j: (0, 0)),
            pl.BlockSpec((D_MODEL, D_MODEL), lambda j: (0, j)),
            pl.BlockSpec((1, D_MODEL), lambda j: (0, j)),
        ],
        out_specs=pl.BlockSpec((bt, D_MODEL), lambda j: (0, j)),
        out_shape=jax.ShapeDtypeStruct((bt, w_ada.shape[1]), F32),
        compiler_params=_cparams(("arbitrary",)),
        name="ada_mod",
    )(c, w_ada, b_ada.reshape(1, -1))


def _front_body(x_ref, mod_ref, win_ref, wbgt_ref, cw_ref, dww_ref, dwb_ref, cng_ref, cnb_ref,
                gpc_ref, gpr_ref, sq_ref, sd_ref,
                q_ref, k_ref, v_ref, z_ref, bgc_ref, bgr_ref, uc_ref, nq_ref, nd_ref,
                qkv_buf, u_buf, *, tl):
    t = pl.program_id(1)
    rq = QKV_TAIL + tl
    ru = DW_TAIL + tl
    n_qkv = QKV_WIDTH // LANES
    n_u = CONV_WIDTH // LANES
    lanes = lambda c: slice(c * LANES, (c + 1) * LANES)

    @pl.when(t == 0)
    def _():
        for c in range(n_qkv):
            qkv_buf[c * rq:c * rq + QKV_TAIL, :] = sq_ref[0, :, lanes(c)]
        for c in range(n_u):
            u_buf[c * ru:c * ru + DW_TAIL, :] = sd_ref[0, :, lanes(c)]

    sh1 = mod_ref[0, 0:1, :]
    sc1 = mod_ref[0, 1:2, :]
    h = _layer_norm(x_ref[0]) * (1.0 + sc1) + sh1
    hb = h.astype(BF16)

    qkv = _dot(hb, win_ref[:, 0:QKV_WIDTH])
    for c in range(n_qkv):
        qkv_buf[c * rq + QKV_TAIL:(c + 1) * rq, :] = qkv[:, lanes(c)]
    off_q = QKV_TAIL - (GDN_CONV - 1)
    for c in range(n_qkv):
        acc = cw_ref[0:1, lanes(c)] * qkv_buf[c * rq + off_q:c * rq + off_q + tl, :]
        for j in range(1, GDN_CONV):
            acc = acc + cw_ref[j:j + 1, lanes(c)] * qkv_buf[c * rq + off_q + j:c * rq + off_q + j + tl, :]
        a = _silu(acc)
        hd = c % GDN_HEADS
        if c < GDN_HEADS:
            a = a * lax.rsqrt(jnp.sum(a * a, axis=-1, keepdims=True) + L2_EPS) * (GDN_HEAD_DIM ** -0.5)
            q_ref[0, :, lanes(hd)] = a.astype(BF16)
        elif c < 2 * GDN_HEADS:
            a = a * lax.rsqrt(jnp.sum(a * a, axis=-1, keepdims=True) + L2_EPS)
            k_ref[0, :, lanes(hd)] = a.astype(BF16)
        else:
            v_ref[0, :, lanes(hd)] = a.astype(BF16)
    z_ref[0] = _dot(hb, win_ref[:, COL_Z:COL_GV]).astype(BF16)

    raw_c = _dot(hb, win_ref[:, COL_BG:D_PROJ_PAD])
    lane = lax.broadcasted_iota(I32, raw_c.shape, 1)
    neg_a_c = -jnp.exp(gpc_ref[0:1, :])
    g_c = neg_a_c * _softplus(raw_c + gpc_ref[1:2, :])
    bgc_ref[0] = jnp.where(lane < GDN_HEADS, _sigmoid(raw_c), g_c)
    raw_r = _dot_nt(wbgt_ref[...], hb)
    row = lax.broadcasted_iota(I32, raw_r.shape, 0)
    neg_a_r = -jnp.exp(gpr_ref[:, 0:1])
    g_r = neg_a_r * _softplus(raw_r + gpr_ref[:, 1:2])
    bgr_ref[0] = jnp.where(row < GDN_HEADS, _sigmoid(raw_r), g_r)

    gv = _dot(hb, win_ref[:, COL_GV:COL_GG])
    gg = _dot(hb, win_ref[:, COL_GG:COL_BG])
    u = gv * _sigmoid(gg)
    for c in range(n_u):
        u_buf[c * ru + DW_TAIL:(c + 1) * ru, :] = u[:, lanes(c)]
    off_u = DW_TAIL - (DW_CONV - 1)
    daccs = []
    for c in range(n_u):
        dacc = dwb_ref[:, lanes(c)] + dww_ref[0:1, lanes(c)] * u_buf[c * ru + off_u:c * ru + off_u + tl, :]
        for j in range(1, DW_CONV):
            dacc = dacc + dww_ref[j:j + 1, lanes(c)] * u_buf[c * ru + off_u + j:c * ru + off_u + j + tl, :]
        daccs.append(dacc)
    dacc = jnp.concatenate(daccs, axis=1)
    uc_ref[0] = _silu(_layer_norm(dacc) * cng_ref[...] + cnb_ref[...]).astype(BF16)

    for c in range(n_qkv):
        nq = qkv_buf[c * rq + tl:(c + 1) * rq, :]
        qkv_buf[c * rq:c * rq + QKV_TAIL, :] = nq
        nq_ref[0, :, lanes(c)] = nq
    for c in range(n_u):
        nd = u_buf[c * ru + tl:(c + 1) * ru, :]
        u_buf[c * ru:c * ru + DW_TAIL, :] = nd
        nd_ref[0, :, lanes(c)] = nd


def _front(x, mod, sq_pad, sd_pad, prm, tl):
    b, l, _ = x.shape
    nt = l // tl
    tok = lambda w, dt: (pl.BlockSpec((1, tl, w), lambda i, j: (i, j, 0)), jax.ShapeDtypeStruct((b, l, w), dt))
    full = lambda a: pl.BlockSpec(a.shape, lambda i, j: (0,) * a.ndim)
    outs = [tok(GDN_WIDTH, BF16), tok(GDN_WIDTH, BF16), tok(GDN_WIDTH, BF16), tok(GDN_WIDTH, BF16),
            tok(LANES, F32),
            (pl.BlockSpec((1, SUBLANES, tl), lambda i, j: (i, 0, j)), jax.ShapeDtypeStruct((b, SUBLANES, l), F32)),
            tok(CONV_WIDTH, BF16),
            (pl.BlockSpec((1, QKV_TAIL, QKV_WIDTH), lambda i, j: (i, 0, 0)),
             jax.ShapeDtypeStruct((b, QKV_TAIL, QKV_WIDTH), F32)),
            (pl.BlockSpec((1, DW_TAIL, CONV_WIDTH), lambda i, j: (i, 0, 0)),
             jax.ShapeDtypeStruct((b, DW_TAIL, CONV_WIDTH), F32))]
    consts = [prm["w_in_r"], prm["w_bgt"], prm["conv_qkv_w"], prm["dw_w"], prm["dw_b"], prm["cn_g"], prm["cn_b"],
              prm["gp_c"], prm["gp_r"]]
    return pl.pallas_call(
        functools.partial(_front_body, tl=tl),
        grid=(b, nt),
        in_specs=[pl.BlockSpec((1, tl, D_MODEL), lambda i, j: (i, j, 0)),
                  pl.BlockSpec((1, 6, D_MODEL), lambda i, j: (i, 0, 0))]
                 + [full(a) for a in consts]
                 + [pl.BlockSpec((1, QKV_TAIL, QKV_WIDTH), lambda i, j: (i, 0, 0)),
                    pl.BlockSpec((1, DW_TAIL, CONV_WIDTH), lambda i, j: (i, 0, 0))],
        out_specs=[o[0] for o in outs],
        out_shape=[o[1] for o in outs],
        scratch_shapes=[pltpu.VMEM((QKV_WIDTH // LANES * (QKV_TAIL + tl), LANES), F32),
                        pltpu.VMEM((CONV_WIDTH // LANES * (DW_TAIL + tl), LANES), F32)],
        compiler_params=_cparams(("arbitrary", "arbitrary")),
        name="mixer_front",
    )(x, mod, *consts, sq_pad, sd_pad)


def _split3_bf16(x):
    hi = x.astype(BF16)
    r1 = x - hi.astype(F32)
    mid = r1.astype(BF16)
    lo = (r1 - mid.astype(F32)).astype(BF16)
    return hi, mid, lo


def _tri_inverse_stages(ms, ri, ci, out):
    c = ms[0].shape[0]
    eye = jnp.where(ri == ci, 1.0, 0.0)
    pair = (ri >> 1) == (ci >> 1)
    ps = [eye - jnp.where(pair, m, 0.0) for m in ms]
    w = 2
    while w < c:
        s = w.bit_length() - 1
        sel = ((ri >> (s + 1)) == (ci >> (s + 1))) & ((ri >> s) > (ci >> s))
        pbs = [p.astype(BF16) for p in ps]
        xs = [_dot(pb, jnp.where(sel, m, 0.0).astype(BF16)).astype(BF16) for pb, m in zip(pbs, ms)]
        yield
        ps = [p - _dot(x, pb) for p, x, pb in zip(ps, xs, pbs)]
        yield
        w *= 2
    out.extend(ps)


def _gdn_body(q_ref, k_ref, v_ref, z_ref, bgc_ref, bgr_ref, s0_ref, gnw_ref, o_ref, sout_ref,
              s_scr, ku_s, kw_s, au_s, qe_s, egl_s, *, c, g, ncg):
    cg = pl.program_id(1)
    wset = cg % 2 if ncg > 1 else 0
    rset = 1 - wset if ncg > 1 else 0

    ri = lax.broadcasted_iota(I32, (c, c), 0)
    ci = lax.broadcasted_iota(I32, (c, c), 1)
    causal = ri >= ci
    strict = ri > ci
    lower = jnp.where(causal, 1.0, 0.0).astype(BF16)
    upper = jnp.where(ri <= ci, 1.0, 0.0).astype(BF16)
    gnw = gnw_ref[...]

    heads = range(GDN_HEADS)
    hcols = [slice(hd * GDN_HEAD_DIM, (hd + 1) * GDN_HEAD_DIM) for hd in heads]

    def intra_stages():
        prob = []
        for i in range(g):
            rows = pl.ds(i * c, c)
            bgc = bgc_ref[0, rows, :]
            bgr = bgr_ref[0, i]
            gc_c = sum(_dot(lower, part) for part in _split3_bf16(bgc))
            gc_r = sum(_dot(part, upper) for part in _split3_bf16(bgr))
            for hd in heads:
                gcc = gc_c[:, GDN_HEADS + hd:GDN_HEADS + hd + 1]
                gcr = gc_r[GDN_HEADS + hd:GDN_HEADS + hd + 1, :]
                prob.append(dict(i=i, hd=hd, rows=rows, cols=hcols[hd], beta=bgc[:, hd:hd + 1], gcc=gcc,
                                 decay=jnp.exp(jnp.where(causal, gcc - gcr, NEG_BIG))))
        yield
        for p in prob:
            p["kh"] = k_ref[0, p["rows"], p["cols"]]
            p["kb"] = p["kh"].astype(F32) * p["beta"]
        ms = [jnp.where(strict, _dot_nt(p["kb"].astype(BF16), p["kh"]) * p["decay"], 0.0) for p in prob]
        yield
        t_invs = []
        yield from _tri_inverse_stages(ms, ri, ci, t_invs)
        uws = []
        for p, t_inv in zip(prob, t_invs):
            p["egc"] = jnp.exp(p["gcc"])
            vf = v_ref[0, p["rows"], p["cols"]].astype(F32)
            rhs = jnp.concatenate([vf * p["beta"], p["kb"] * p["egc"]], axis=1).astype(BF16)
            uws.append(_dot(t_inv.astype(BF16), rhs).astype(BF16))
        yield
        for p in prob:
            p["qh"] = q_ref[0, p["rows"], p["cols"]]
            p["a"] = jnp.where(causal, _dot_nt(p["qh"], p["kh"]) * p["decay"], 0.0).astype(BF16)
            p["g_last"] = p["gcc"][c - 1:c, :]
            p["kd"] = (p["kh"].astype(F32) * jnp.exp(p["g_last"] - p["gcc"])).astype(BF16)
        yield
        kuws = [_dot_tn(p["kd"], uw) for p, uw in zip(prob, uws)]
        yield
        auws = [_dot(p["a"], uw) for p, uw in zip(prob, uws)]
        yield
        for p, kuw, auw in zip(prob, kuws, auws):
            rows, cols, hd = p["rows"], p["cols"], p["hd"]
            srows = pl.ds(p["i"] * GDN_HEAD_DIM, GDN_HEAD_DIM)
            ku_s[wset, srows, cols] = kuw[:, :GDN_HEAD_DIM]
            kw_s[wset, srows, cols] = kuw[:, GDN_HEAD_DIM:].astype(BF16)
            au_s[wset, rows, cols] = auw[:, :GDN_HEAD_DIM]
            qe_s[wset, rows, cols] = (p["qh"].astype(F32) * p["egc"] - auw[:, GDN_HEAD_DIM:]).astype(BF16)
            egl_s[wset, pl.ds(p["i"] * SUBLANES + hd, 1), :] = jnp.broadcast_to(jnp.exp(p["g_last"]), (1, LANES))

    def inter(i):
        rows = pl.ds(i * c, c)
        srows = pl.ds(i * GDN_HEAD_DIM, GDN_HEAD_DIM)
        ss = [s_scr[hd] for hd in heads]
        sbs = [s.astype(BF16) for s in ss]
        upd = [_dot(kw_s[rset, srows, hcols[hd]], sbs[hd]) for hd in heads]
        for hd in heads:
            s_scr[hd] = ss[hd] * egl_s[rset, pl.ds(i * SUBLANES + hd, 1), :] + (ku_s[rset, srows, hcols[hd]] - upd[hd])
        os_ = [_dot(qe_s[rset, rows, hcols[hd]], sbs[hd]) + au_s[rset, rows, hcols[hd]] for hd in heads]
        for hd in heads:
            o = os_[hd]
            gate = _silu(z_ref[0, rows, hcols[hd]].astype(F32))
            on = o * lax.rsqrt(jnp.mean(o * o, axis=-1, keepdims=True) + RMS_EPS) * gnw * gate
            o_ref[0, rows, hcols[hd]] = on.astype(BF16)

    def run(do_intra, do_inter):
        stages = intra_stages() if do_intra else iter(())
        todo = list(range(g)) if do_inter else []
        n_stages = 6 + 2 * (c.bit_length() - 2)
        every = max(1, n_stages // max(1, len(todo)))
        for n, _ in enumerate(stages):
            if todo and n % every == 0:
                inter(todo.pop(0))
        for i in todo:
            inter(i)

    if ncg == 1:
        s_scr[...] = s0_ref[0]
        run(True, False)
        run(False, True)
        sout_ref[0] = s_scr[...]
        return

    @pl.when(cg == 0)
    def _():
        s_scr[...] = s0_ref[0]
        run(True, False)

    @pl.when((cg > 0) & (cg < ncg))
    def _():
        run(True, True)

    @pl.when(cg == ncg)
    def _():
        run(False, True)
        sout_ref[0] = s_scr[...]


def _gdn(q, k, v, z, bgc, bgr, s0, gnw, c, g):
    b, l, _ = q.shape
    ncg = l // (c * g)
    cur = lambda w: pl.BlockSpec((1, c * g, w), lambda i, j: (i, jnp.minimum(j, ncg - 1), 0))
    prev = lambda w: pl.BlockSpec((1, c * g, w), lambda i, j: (i, jnp.maximum(j - 1, 0), 0))
    st = pl.BlockSpec((1, GDN_HEADS, GDN_HEAD_DIM, GDN_HEAD_DIM), lambda i, j: (i, 0, 0, 0))
    tok_buf = lambda dt: pltpu.VMEM((2, c * g, GDN_WIDTH), dt)
    state_buf = lambda dt: pltpu.VMEM((2, g * GDN_HEAD_DIM, GDN_WIDTH), dt)
    return pl.pallas_call(
        functools.partial(_gdn_body, c=c, g=g, ncg=ncg),
        grid=(b, ncg + 1 if ncg > 1 else 1),
        in_specs=[cur(GDN_WIDTH), cur(GDN_WIDTH), cur(GDN_WIDTH), prev(GDN_WIDTH), cur(LANES),
                  pl.BlockSpec((1, g, SUBLANES, c), lambda i, j: (i, jnp.minimum(j, ncg - 1), 0, 0)), st,
                  pl.BlockSpec((1, GDN_HEAD_DIM), lambda i, j: (0, 0))],
        out_specs=[prev(GDN_WIDTH), st],
        out_shape=[jax.ShapeDtypeStruct((b, l, GDN_WIDTH), BF16), jax.ShapeDtypeStruct(s0.shape, F32)],
        scratch_shapes=[pltpu.VMEM((GDN_HEADS, GDN_HEAD_DIM, GDN_HEAD_DIM), F32),
                        state_buf(F32), state_buf(BF16), tok_buf(F32), tok_buf(BF16),
                        pltpu.VMEM((2, g * SUBLANES, LANES), F32)],
        compiler_params=_cparams(("arbitrary", "arbitrary")),
        name="gdn",
    )(q, k, v, z, bgc, bgr, s0, gnw)


HALF = D_MODEL // 2
HI_MASK = 0xFFFF0000


def _pack_bf16_pairs(x):
    bits = pltpu.bitcast(x.astype(BF16).astype(F32), U32)
    return (bits[:, :HALF] >> 16) | (bits[:, HALF:] & jnp.uint32(HI_MASK))


def _unpack_bf16_pairs(p):
    return pltpu.bitcast(p << 16, F32), pltpu.bitcast(p & jnp.uint32(HI_MASK), F32)


ROW_TILES = HALF // LANES


def _row_lines(r):
    return pl.ds(r * ROW_TILES, ROW_TILES)


def _store_rows(ref, lead, r0, packed):
    n = packed.shape[0]
    for q in range(ROW_TILES):
        ref[(*lead, pl.ds(r0 * ROW_TILES + q, n, stride=ROW_TILES), slice(None))] = packed[:, q * LANES:(q + 1) * LANES]


def _load_rows(ref, lead, r0, n):
    return jnp.concatenate(
        [ref[(*lead, pl.ds(r0 * ROW_TILES + q, n, stride=ROW_TILES), slice(None))] for q in range(ROW_TILES)], axis=1)


def _mid_body(o_ref, uc_ref, x_ref, gt1_ref, sh2_ref, sc2_ref, gt2_ref,
              wout_ref, g1_ref, b1_ref, wsgu_ref, wsd_ref, wrh_ref, wrl_ref, pre_ref, h2p_ref, lgt_ref):
    gt1, sh2, sc2, gt2 = gt1_ref[0], sh2_ref[0], sc2_ref[0], gt2_ref[0]
    mix = _dot(jnp.concatenate([o_ref[0], uc_ref[0]], axis=1), wout_ref[...])
    x1 = _layer_norm(ALPHA * x_ref[0] + (1.0 + gt1) * mix) * g1_ref[...] + b1_ref[...]
    h2 = _layer_norm(x1) * (1.0 + sc2) + sh2
    hh, hl = _split_bf16(h2)
    lgt_ref[...] = _dot_nt(wrh_ref[...], hh) + (_dot_nt(wrh_ref[...], hl) + _dot_nt(wrl_ref[...], hh))
    gu = _dot(hh, wsgu_ref[...])
    act = _silu(gu[:, :D_SHARED]) * gu[:, D_SHARED:]
    shared = _dot(act.astype(BF16), wsd_ref[...])
    pre_ref[0] = ALPHA * x1 + (1.0 + gt2) * shared
    _store_rows(h2p_ref, (0,), 0, _pack_bf16_pairs(h2))


def _mod_spec(m, tl):
    if m.shape[1] == 1:
        return pl.BlockSpec((1, 1, D_MODEL), lambda i, j: (i, 0, 0))
    return pl.BlockSpec((1, tl, D_MODEL), lambda i, j: (i, j, 0))


def _mid(o, uc, x, mods, prm, tl):
    b, l, _ = x.shape
    nt = l // tl
    full = lambda a: pl.BlockSpec(a.shape, lambda i, j: (0,) * a.ndim)
    consts = [prm["w_out"], prm["ln1_g"], prm["ln1_b"], prm["ws_gu"], prm["ws_down"], prm["wr_hi"], prm["wr_lo"]]
    return pl.pallas_call(
        _mid_body,
        grid=(b, nt),
        in_specs=[pl.BlockSpec((1, tl, GDN_WIDTH), lambda i, j: (i, j, 0)),
                  pl.BlockSpec((1, tl, CONV_WIDTH), lambda i, j: (i, j, 0)),
                  pl.BlockSpec((1, tl, D_MODEL), lambda i, j: (i, j, 0))]
                 + [_mod_spec(m, tl) for m in mods] + [full(a) for a in consts],
        out_specs=[pl.BlockSpec((1, tl, D_MODEL), lambda i, j: (i, j, 0)),
                   pl.BlockSpec((1, tl * ROW_TILES, LANES), lambda i, j: (i, j, 0)),
                   pl.BlockSpec((N_EXPERTS, tl), lambda i, j: (0, i * nt + j))],
        out_shape=[jax.ShapeDtypeStruct((b, l, D_MODEL), F32), jax.ShapeDtypeStruct((b, l * ROW_TILES, LANES), U32),
                   jax.ShapeDtypeStruct((N_EXPERTS, b * l), F32)],
        compiler_params=_cparams(("arbitrary", "arbitrary")),
        name="mixer_out",
    )(o, uc, x, *mods, *consts)


def _first_max(x, row, n):
    m = jnp.max(x, axis=0, keepdims=True)
    ix = jnp.min(jnp.where(x == m, row, float(n)), axis=0, keepdims=True)
    return m, ix


def _route_body(lgp_ref, lgs_ref, bias_ref, idx_ref, w_ref, rank_ref, cnt_ref, cnt_scr, *, tr, n_prompt_tiles):
    @pl.when(pl.program_id(0) == 0)
    def _():
        cnt_scr[...] = jnp.zeros_like(cnt_scr)

    neg = -jnp.inf
    logits = jnp.where(pl.program_id(0) < n_prompt_tiles, lgp_ref[...], lgs_ref[...])
    scores = _sigmoid(logits)
    sel = scores + bias_ref[:, 0:1]
    row_g = lax.broadcasted_iota(I32, (GROUP_SIZE, tr), 0).astype(F32)
    gs = []
    for g in range(N_GROUPS):
        blk = sel[g * GROUP_SIZE:(g + 1) * GROUP_SIZE, :]
        m1, i1 = _first_max(blk, row_g, GROUP_SIZE)
        m2 = jnp.max(jnp.where(row_g == i1, neg, blk), axis=0, keepdims=True)
        gs.append(m1 + m2)
    gs = jnp.concatenate(gs, axis=0)
    row_n = lax.broadcasted_iota(I32, (N_GROUPS, tr), 0).astype(F32)
    chosen = jnp.zeros((N_GROUPS, tr), F32)
    for _ in range(TOPK_GROUPS):
        _, ix = _first_max(gs, row_n, N_GROUPS)
        hit = row_n == ix
        chosen = jnp.where(hit, 1.0, chosen)
        gs = jnp.where(hit, neg, gs)
    selm = jnp.concatenate(
        [jnp.where(chosen[g:g + 1, :] > 0.5, sel[g * GROUP_SIZE:(g + 1) * GROUP_SIZE, :], neg) for g in range(N_GROUPS)],
        axis=0)
    row_e = lax.broadcasted_iota(I32, (N_EXPERTS, tr), 0).astype(F32)
    idxs, ws = [], []
    picked = jnp.zeros((N_EXPERTS, tr), F32)
    for _ in range(TOP_K):
        _, ix = _first_max(selm, row_e, N_EXPERTS)
        hit = row_e == ix
        ws.append(jnp.sum(jnp.where(hit, scores, 0.0), axis=0, keepdims=True))
        idxs.append(ix)
        selm = jnp.where(hit, neg, selm)
        picked = jnp.where(hit, 1.0, picked)
    wsum = ws[0]
    for wk in ws[1:]:
        wsum = wsum + wk
    idx_ref[...] = jnp.concatenate(idxs, axis=0).astype(I32)
    w_ref[...] = jnp.concatenate(ws, axis=0) / wsum * ROUTED_SCALE
    ti = lax.broadcasted_iota(I32, (tr, tr), 0)
    tj = lax.broadcasted_iota(I32, (tr, tr), 1)
    before = _dot(picked.astype(BF16), jnp.where(ti < tj, 1.0, 0.0).astype(BF16)) + cnt_scr[:, 0:1]
    rank_ref[...] = jnp.concatenate(
        [jnp.sum(jnp.where(row_e == ix, before, 0.0), axis=0, keepdims=True) for ix in idxs], axis=0).astype(I32)
    cnt_scr[...] = cnt_scr[...] + jnp.sum(picked, axis=1, keepdims=True)
    cnt_ref[...] = cnt_scr[...]


def _route(logits_p, logits_s, bias_b, tr):
    ntp, nts = logits_p.shape[1] // tr, logits_s.shape[1] // tr
    t = (ntp + nts) * tr
    kt = lambda dt: (pl.BlockSpec((TOP_K, tr), lambda i: (0, i)), jax.ShapeDtypeStruct((TOP_K, t), dt))
    outs = [kt(I32), kt(F32), kt(I32),
            (pl.BlockSpec((N_EXPERTS, LANES), lambda i: (0, 0)), jax.ShapeDtypeStruct((N_EXPERTS, LANES), F32))]
    return pl.pallas_call(
        functools.partial(_route_body, tr=tr, n_prompt_tiles=ntp),
        grid=(ntp + nts,),
        in_specs=[pl.BlockSpec((N_EXPERTS, tr), lambda i: (0, jnp.minimum(i, ntp - 1))),
                  pl.BlockSpec((N_EXPERTS, tr), lambda i: (0, jnp.maximum(i - ntp, 0))),
                  pl.BlockSpec((N_EXPERTS, LANES), lambda i: (0, 0))],
        out_specs=[o[0] for o in outs],
        out_shape=[o[1] for o in outs],
        scratch_shapes=[pltpu.VMEM((N_EXPERTS, LANES), F32)],
        compiler_params=_cparams(("arbitrary",)),
        name="route",
    )(logits_p, logits_s, bias_b)


def _pos_body(idx_ref, rank_ref, cnt_ref, pos_ref, *, tr):
    ei = lax.broadcasted_iota(I32, (N_EXPERTS, N_EXPERTS), 0)
    ej = lax.broadcasted_iota(I32, (N_EXPERTS, N_EXPERTS), 1)
    below = jnp.where(ej < ei, 1.0, 0.0).astype(BF16)
    start = sum(_dot(below, part) for part in _split3_bf16(cnt_ref[...]))[:, 0:1]
    row_e = lax.broadcasted_iota(I32, (N_EXPERTS, tr), 0)
    pos_ref[...] = jnp.concatenate(
        [jnp.sum(jnp.where(row_e == idx_ref[k:k + 1, :], start, 0.0), axis=0, keepdims=True) for k in range(TOP_K)],
        axis=0).astype(I32) + rank_ref[...]


def _positions(idx, rank, cnt, tr):
    t = idx.shape[1]
    kt = pl.BlockSpec((TOP_K, tr), lambda i: (0, i))
    return pl.pallas_call(
        functools.partial(_pos_body, tr=tr),
        grid=(t // tr,),
        in_specs=[kt, kt, pl.BlockSpec((N_EXPERTS, LANES), lambda i: (0, 0))],
        out_specs=kt,
        out_shape=jax.ShapeDtypeStruct((TOP_K, t), I32),
        compiler_params=_cparams(("arbitrary",)),
        name="positions",
    )(idx, rank, cnt)


def _dispatch_body(pos_ref, hp_ref, hs_ref, xs_ref, sem, *, tt, n_prompt_tiles):
    i = pl.program_id(0)

    def scatter(src_ref):
        def body(t, carry):
            for k in range(TOP_K):
                pltpu.make_async_copy(src_ref.at[_row_lines(t)], xs_ref.at[_row_lines(pos_ref[0, 0, t * TOP_K + k])],
                                      sem).start(priority=k % 2)
            return carry
        lax.fori_loop(0, tt, body, 0)
        for _ in range(TOP_K):
            pltpu.make_async_copy(src_ref, xs_ref.at[pl.ds(0, tt * ROW_TILES)], sem).wait()

    @pl.when(i < n_prompt_tiles)
    def _():
        scatter(hp_ref)

    @pl.when(i >= n_prompt_tiles)
    def _():
        scatter(hs_ref)


def _dispatch(pos_tiles, h_prompt, h_sample, tt):
    tp, ts = h_prompt.shape[0] // ROW_TILES, h_sample.shape[0] // ROW_TILES
    ntp, nts = tp // tt, ts // tt
    n_rows = TOP_K * (tp + ts)
    return pl.pallas_call(
        functools.partial(_dispatch_body, tt=tt, n_prompt_tiles=ntp),
        grid=(ntp + nts,),
        in_specs=[pl.BlockSpec((1, 1, tt * TOP_K), lambda i: (i, 0, 0), memory_space=pltpu.SMEM),
                  pl.BlockSpec((tt * ROW_TILES, LANES), lambda i: (jnp.minimum(i, ntp - 1), 0)),
                  pl.BlockSpec((tt * ROW_TILES, LANES), lambda i: (jnp.maximum(i - ntp, 0), 0))],
        out_specs=pl.BlockSpec(memory_space=pl.ANY),
        out_shape=jax.ShapeDtypeStruct((n_rows * ROW_TILES, LANES), U32),
        scratch_shapes=[pltpu.SemaphoreType.DMA(())],
        compiler_params=_cparams(("arbitrary",)),
        name="dispatch",
    )(pos_tiles, h_prompt, h_sample)


GMM_ROWS = 2048
GMM_SUB = 512


def _gmm_body(ve_ref, vb_ref, vlo_ref, vhi_ref, vfirst_ref, vnew_ref, vnext_ref,
              xs_ref, wg_hbm, wu_hbm, wd_hbm, ys_ref, wgu_s, wd_s, wg_f, wu_f, wd_f, sem):
    v = pl.program_id(0)
    lo = vlo_ref[v]
    hi = vhi_ref[v]

    def weight_copies(e):
        return (pltpu.make_async_copy(wg_hbm.at[e], wg_f, sem.at[0]),
                pltpu.make_async_copy(wu_hbm.at[e], wu_f, sem.at[1]),
                pltpu.make_async_copy(wd_hbm.at[e], wd_f, sem.at[2]))

    @pl.when(v == 0)
    def _():
        for cp in weight_copies(ve_ref[0]):
            cp.start()

    @pl.when(vnew_ref[v] == 1)
    def _():
        for cp in weight_copies(ve_ref[v]):
            cp.wait()
        wgu_s[:, :D_EXPERT] = wg_f[...].astype(BF16)
        wgu_s[:, D_EXPERT:] = wu_f[...].astype(BF16)
        wd_s[...] = wd_f[...].astype(BF16)

        @pl.when(vnext_ref[v] >= 0)
        def _():
            for cp in weight_copies(vnext_ref[v]):
                cp.start()

    @pl.when(vfirst_ref[v] == 1)
    def _():
        ys_ref[...] = jnp.zeros_like(ys_ref)

    def sub_block(s, carry):
        r0 = pl.multiple_of(s * GMM_SUB, GMM_SUB)
        x_lo, x_hi = _unpack_bf16_pairs(_load_rows(xs_ref, (), r0, GMM_SUB))
        gu = _dot(x_lo.astype(BF16), wgu_s[:HALF, :]) + _dot(x_hi.astype(BF16), wgu_s[HALF:, :])
        act = _silu(gu[:, :D_EXPERT]) * gu[:, D_EXPERT:]
        y = _pack_bf16_pairs(_dot(act.astype(BF16), wd_s[...]))
        row = lax.broadcasted_iota(I32, y.shape, 0) + r0
        _store_rows(ys_ref, (), r0, jnp.where((row >= lo) & (row < hi), y, _load_rows(ys_ref, (), r0, GMM_SUB)))
        return carry

    lax.fori_loop(lo // GMM_SUB, (hi + GMM_SUB - 1) // GMM_SUB, sub_block, 0)


def _gmm_schedule(counts, n_rows):
    nb = n_rows // GMM_ROWS
    n_vis = nb + N_EXPERTS
    ends = jnp.cumsum(counts)
    starts = ends - counts
    first_blk = starts // GMM_ROWS
    last_blk = jnp.maximum(ends - 1, 0) // GMM_ROWS
    per_e = jnp.where(counts > 0, last_blk - first_blk + 1, 0)
    vis_end = jnp.cumsum(per_e)
    total = vis_end[-1]
    v = jnp.minimum(jnp.arange(n_vis, dtype=I32), total - 1)
    e = jnp.minimum(jnp.sum((vis_end[None, :] <= v[:, None]).astype(I32), axis=1), N_EXPERTS - 1)
    ids = jnp.arange(N_EXPERTS, dtype=I32)
    later = jnp.where((ids[None, :] > ids[:, None]) & (counts[None, :] > 0), ids[None, :], N_EXPERTS)
    nxt = jnp.min(later, axis=1)
    nxt = jnp.where(nxt == N_EXPERTS, -1, nxt)
    table = jnp.stack([first_blk, vis_end - per_e, starts, ends, nxt], axis=1).astype(I32)
    pick = e[:, None] == ids[None, :]
    fb, v0, st, en, nx = jnp.moveaxis(jnp.sum(jnp.where(pick[:, :, None], table[None], 0), axis=1), 1, 0)
    blk = fb + (v - v0)
    lo = jnp.maximum(st, blk * GMM_ROWS) - blk * GMM_ROWS
    hi = jnp.minimum(en, (blk + 1) * GMM_ROWS) - blk * GMM_ROWS
    hi = jnp.where(jnp.arange(n_vis) < total, hi, lo)
    prev = lambda a: jnp.concatenate([jnp.full((1,), -1, I32), a[:-1]])
    first = (blk != prev(blk)).astype(I32)
    new_e = (e != prev(e)).astype(I32)
    return e, blk, lo.astype(I32), hi.astype(I32), first, new_e, nx.astype(I32)


def _gmm(xs, counts, we_gate, we_up, we_down):
    n_rows = xs.shape[0] // ROW_TILES
    sched = _gmm_schedule(counts, n_rows)
    n_vis = sched[0].shape[0]
    grid_spec = pltpu.PrefetchScalarGridSpec(
        num_scalar_prefetch=len(sched),
        grid=(n_vis,),
        in_specs=[pl.BlockSpec((GMM_ROWS * ROW_TILES, LANES), lambda v, ve, vb, *_: (vb[v], 0)),
                  pl.BlockSpec(memory_space=pl.ANY), pl.BlockSpec(memory_space=pl.ANY),
                  pl.BlockSpec(memory_space=pl.ANY)],
        out_specs=pl.BlockSpec((GMM_ROWS * ROW_TILES, LANES), lambda v, ve, vb, *_: (vb[v], 0)),
        scratch_shapes=[pltpu.VMEM((D_MODEL, 2 * D_EXPERT), BF16), pltpu.VMEM((D_EXPERT, D_MODEL), BF16),
                        pltpu.VMEM((D_MODEL, D_EXPERT), F32), pltpu.VMEM((D_MODEL, D_EXPERT), F32),
                        pltpu.VMEM((D_EXPERT, D_MODEL), F32), pltpu.SemaphoreType.DMA((3,))])
    return pl.pallas_call(
        _gmm_body,
        grid_spec=grid_spec,
        out_shape=jax.ShapeDtypeStruct(xs.shape, U32),
        compiler_params=_cparams(("arbitrary",)),
        name="expert_ffn",
    )(*sched, xs, we_gate, we_up, we_down)


def _combine_body(pos_ref, posn_ref, w_ref, pre_ref, gt2_ref, g2_ref, b2_ref, ys_ref, out_ref, gbuf, sem,
                  *, tt, n_steps):
    step = pl.program_id(0) * pl.num_programs(1) + pl.program_id(1)
    slot = step % 2

    def gather(p_ref, into):
        def body(t, carry):
            for k in range(TOP_K):
                pltpu.make_async_copy(ys_ref.at[_row_lines(p_ref[0, 0, t * TOP_K + k])],
                                      gbuf.at[into, k, _row_lines(t)], sem.at[into]).start(priority=k % 2)
            return carry
        lax.fori_loop(0, tt, body, 0)

    @pl.when(step == 0)
    def _():
        gather(pos_ref, 0)

    @pl.when(step + 1 < n_steps)
    def _():
        gather(posn_ref, 1 - slot)

    pltpu.make_async_copy(gbuf.at[slot], gbuf.at[slot], sem.at[slot]).wait()
    acc_lo = jnp.zeros((tt, HALF), F32)
    acc_hi = jnp.zeros((tt, HALF), F32)
    for k in range(TOP_K):
        y_lo, y_hi = _unpack_bf16_pairs(_load_rows(gbuf, (slot, k), 0, tt))
        wk = w_ref[:, k:k + 1]
        acc_lo = acc_lo + wk * y_lo
        acc_hi = acc_hi + wk * y_hi
    routed = jnp.concatenate([acc_lo, acc_hi], axis=1)
    y = pre_ref[0] + (1.0 + gt2_ref[0]) * routed
    out_ref[0] = _layer_norm(y) * g2_ref[...] + b2_ref[...]


def _combine(pos_tiles, w_tk, pre, gt2, ln2_g, ln2_b, ys, tt, tok0):
    b, l, _ = pre.shape
    nt = l // tt
    blk0 = tok0 // tt
    last = blk0 + b * nt - 1
    pos_spec = lambda nxt: pl.BlockSpec((1, 1, tt * TOP_K),
                                        lambda i, j: (jnp.minimum(blk0 + i * nt + j + nxt, last), 0, 0),
                                        memory_space=pltpu.SMEM)
    return pl.pallas_call(
        functools.partial(_combine_body, tt=tt, n_steps=b * nt),
        grid=(b, nt),
        in_specs=[pos_spec(0), pos_spec(1),
                  pl.BlockSpec((tt, TOP_K), lambda i, j: (blk0 + i * nt + j, 0)),
                  pl.BlockSpec((1, tt, D_MODEL), lambda i, j: (i, j, 0)),
                  _mod_spec(gt2, tt),
                  pl.BlockSpec((1, D_MODEL), lambda i, j: (0, 0)),
                  pl.BlockSpec((1, D_MODEL), lambda i, j: (0, 0)),
                  pl.BlockSpec(memory_space=pl.ANY)],
        out_specs=pl.BlockSpec((1, tt, D_MODEL), lambda i, j: (i, j, 0)),
        out_shape=jax.ShapeDtypeStruct((b, l, D_MODEL), F32),
        scratch_shapes=[pltpu.VMEM((2, TOP_K, tt * ROW_TILES, LANES), U32), pltpu.SemaphoreType.DMA((2,))],
        compiler_params=_cparams(("arbitrary", "arbitrary")),
        name="combine",
    )(pos_tiles, pos_tiles, w_tk, pre, gt2, ln2_g, ln2_b, ys)


def _prep_params(w_in, conv_qkv_w, a_log, dt_bias, dw_w, dw_b, cn_g, cn_b):
    z0 = QKV_WIDTH
    b0 = z0 + GDN_WIDTH
    g0 = b0 + 2 * GDN_HEADS
    w_bg = w_in[:, b0:g0]
    w_in_r = jnp.concatenate(
        [w_in[:, :b0], w_in[:, g0:], w_bg, jnp.zeros((D_MODEL, LANES - 2 * GDN_HEADS), w_in.dtype)], axis=1).astype(BF16)
    pad_h = jnp.zeros((GDN_HEADS,), F32)
    al = jnp.concatenate([pad_h, a_log.astype(F32)])
    db = jnp.concatenate([pad_h, dt_bias.astype(F32)])
    gp_c = jnp.zeros((SUBLANES, LANES), F32).at[0, :2 * GDN_HEADS].set(al).at[1, :2 * GDN_HEADS].set(db)
    gp_r = jnp.zeros((SUBLANES, LANES), F32).at[:, 0].set(al).at[:, 1].set(db)
    return dict(
        w_in_r=w_in_r, w_bgt=w_bg.T.astype(BF16), conv_qkv_w=conv_qkv_w.astype(F32),
        dw_w=jnp.concatenate([dw_w, jnp.zeros((1, CONV_WIDTH), dw_w.dtype)], axis=0).astype(F32),
        dw_b=dw_b.reshape(1, -1).astype(F32), cn_g=cn_g.reshape(1, -1).astype(F32), cn_b=cn_b.reshape(1, -1).astype(F32),
        gp_c=gp_c, gp_r=gp_r)


def _prep_mid_params(w_out, ln1_g, ln1_b, w_router, router_bias, ws_gate, ws_up, ws_down, ln2_g, ln2_b, gdn_norm_w):
    row = lambda a: a.reshape(1, -1).astype(F32)
    wr_t = w_router.astype(F32).T
    wr_hi = wr_t.astype(BF16)
    return dict(
        w_out=w_out.astype(BF16), ln1_g=row(ln1_g), ln1_b=row(ln1_b),
        ws_gu=jnp.concatenate([ws_gate, ws_up], axis=1).astype(BF16), ws_down=ws_down.astype(BF16),
        wr_hi=wr_hi, wr_lo=(wr_t - wr_hi.astype(F32)).astype(BF16),
        bias_b=jnp.broadcast_to(router_bias.astype(F32)[:, None], (N_EXPERTS, LANES)),
        ln2_g=row(ln2_g), ln2_b=row(ln2_b), gnw=row(gdn_norm_w))


def _tile(n, pref):
    t = min(pref, n)
    while n % t:
        t //= 2
    return t


def _token_mixer(x, mod, s_gdn, s_qkv, s_dw, prm, mprm):
    b, l, _ = x.shape
    assert l >= DW_CONV - 1 and l % SUBLANES == 0
    tl = _tile(l, 256)
    sq_pad = jnp.pad(s_qkv.astype(F32), ((0, 0), (QKV_TAIL - (GDN_CONV - 1), 0), (0, 0)))
    sd_pad = jnp.pad(s_dw.astype(F32), ((0, 0), (DW_TAIL - (DW_CONV - 1), 0), (0, 0)))
    q, k, v, z, bgc, bgr, uc, nq, nd = _front(x, mod, sq_pad, sd_pad, prm, tl)
    c = min(CHUNK, l)
    n_chunks = l // c
    g = _tile(n_chunks, 8)
    bgr = bgr.reshape(b, SUBLANES, n_chunks, c).transpose(0, 2, 1, 3)
    o, s_new = _gdn(q, k, v, z, bgc, bgr, s_gdn.astype(F32), mprm["gnw"], c, g)
    return o, uc, s_new, nq[:, QKV_TAIL - (GDN_CONV - 1):], nd[:, DW_TAIL - (DW_CONV - 1):]


def kernel(x_prompt, x_sample, state_gdn, state_qkv_conv, state_dw_conv, c_prompt, c_sample, w_ada, b_ada, w_in, conv_qkv_w, a_log, dt_bias, gdn_norm_w, dw_w, dw_b, cn_g, cn_b, w_out, ln1_g, ln1_b, w_router, router_bias, we_gate, we_up, we_down, ws_gate, ws_up, ws_down, ln2_g, ln2_b):
    bp, lp, _ = x_prompt.shape
    bs, ls, _ = x_sample.shape
    tp, ts = bp * lp, bs * ls
    yp, ys = x_prompt, x_sample
    c_all = jnp.concatenate([c_prompt, c_sample], axis=0)
    new_p, new_s = [], []
    for l in range(w_ada.shape[0]):
        prm = _prep_params(w_in[l], conv_qkv_w[l], a_log[l], dt_bias[l], dw_w[l], dw_b[l], cn_g[l], cn_b[l])
        mprm = _prep_mid_params(w_out[l], ln1_g[l], ln1_b[l], w_router[l], router_bias[l], ws_gate[l], ws_up[l],
                                ws_down[l], ln2_g[l], ln2_b[l], gdn_norm_w[l])
        mod = _ada_mod(c_all, w_ada[l], b_ada[l]).reshape(bp + bs, 6, D_MODEL)
        mod_p, mod_s = mod[:bp], mod[bp:]

        zg = jnp.zeros((bp, GDN_HEADS, GDN_HEAD_DIM, GDN_HEAD_DIM), F32)
        zq = jnp.zeros((bp, GDN_CONV - 1, QKV_WIDTH), F32)
        zd = jnp.zeros((bp, DW_CONV - 1, CONV_WIDTH), F32)
        o_p, uc_p, g_p, q_p, d_p = _token_mixer(yp, mod_p, zg, zq, zd, prm, mprm)
        o_s, uc_s, g_s, q_s, d_s = _token_mixer(ys, mod_s, state_gdn[l], state_qkv_conv[l], state_dw_conv[l], prm, mprm)

        mods_p = tuple(mod_p[:, j:j + 1, :] for j in (2, 3, 4, 5))
        mods_s = tuple(jnp.repeat(mod_s[:, j, :], ls, axis=0)[None] for j in (2, 3, 4, 5))
        flat = lambda a: a.reshape(1, ts, a.shape[-1])
        pre_p, h_p, lg_p = _mid(o_p, uc_p, yp, mods_p, mprm, _tile(lp, 512))
        pre_s, h_s, lg_s = _mid(flat(o_s), flat(uc_s), flat(ys), mods_s, mprm, _tile(ts, 256))

        tt = _tile(ts, 512)
        assert tp % tt == 0 and lp % tt == 0 and (TOP_K * (tp + ts)) % GMM_ROWS == 0
        idx, w, rank, cnt = _route(lg_p, lg_s, mprm["bias_b"], tt)
        pos = _positions(idx, rank, cnt, tt)
        pos_tiles = pos.T.reshape((tp + ts) // tt, 1, tt * TOP_K)
        xs = _dispatch(pos_tiles, h_p.reshape(tp * ROW_TILES, LANES), h_s.reshape(ts * ROW_TILES, LANES), tt)
        ye = _gmm(xs, cnt[:, 0].astype(I32), we_gate[l], we_up[l], we_down[l])
        w_tk = w.T
        yp = _combine(pos_tiles, w_tk, pre_p, mods_p[3], mprm["ln2_g"], mprm["ln2_b"], ye, tt, 0)
        ys = _combine(pos_tiles, w_tk, pre_s, mods_s[3], mprm["ln2_g"], mprm["ln2_b"], ye, tt, tp
                      ).reshape(bs, ls, D_MODEL)
        new_p.append((g_p.astype(state_gdn.dtype), q_p.astype(x_prompt.dtype), d_p.astype(x_prompt.dtype)))
        new_s.append((g_s.astype(state_gdn.dtype), q_s.astype(state_qkv_conv.dtype), d_s.astype(state_dw_conv.dtype)))
    stack = lambda rows, j: jnp.stack([r[j] for r in rows])
    return (yp, ys, stack(new_p, 0), stack(new_p, 1), stack(new_p, 2), stack(new_s, 0), stack(new_s, 1), stack(new_s, 2))
```
